```python
import math
import jax, jax.numpy as jnp
from jax import lax
import numpy as np

D_MODEL = 1024
BATCH = 2
SEQ = 8192
DEPTH = 1

CTX_LEN = 256
GRID_W = 64
EPS = 1e-6

SSD_EXPAND = 2
D_INNER = SSD_EXPAND * D_MODEL
HEAD_DIM = 64
N_HEADS = D_INNER // HEAD_DIM
N_GROUPS = 8
HEADS_PER_GROUP = N_HEADS // N_GROUPS
D_STATE = 128
D_CONV = 5
CHUNK = 128
DT_MIN = 1e-3
DT_MAX = 1e-1
GS = N_GROUPS * D_STATE
CONV_DIM = D_INNER + 2 * GS

FFT_WIDTH = D_MODEL
FFT_GROUPS = 8
FFT_GROUP_DIM = FFT_WIDTH // FFT_GROUPS

N_BRANCH = 2

DT_START = CONV_DIM
Z_START = DT_START + 2 * N_HEADS
FFT_START = Z_START + D_INNER
GATE_START = FFT_START + FFT_WIDTH
IN_WIDTH = GATE_START + N_BRANCH * D_MODEL
SSD_IN_WIDTH = Z_START

D_FF = 256 * ((8 * D_MODEL + 3 * 256 - 1) // (3 * 256))

kernel_name = 'hybrid_ssd_fourier_dit_block'


def rmsnorm(x, g):
    xf = x.astype(jnp.float32)
    y = xf * lax.rsqrt(jnp.mean(xf * xf, axis=-1, keepdims=True) + EPS)
    return (y * g.astype(jnp.float32)).astype(x.dtype)


def modulate(h, shift, scale):
    return h * (1 + scale) + shift


def dwconv_centred(u, w, b):
    out = lax.conv_general_dilated(
        u, w[:, None, :].astype(u.dtype), window_strides=(1,),
        padding=[(D_CONV // 2, D_CONV // 2)],
        dimension_numbers=('NWC', 'WIO', 'NWC'),
        feature_group_count=u.shape[-1])
    return out + b.astype(u.dtype)


def ssd_inputs(u, conv_w, conv_b):
    b, l, _ = u.shape
    xbc = jax.nn.silu(dwconv_centred(u[..., :CONV_DIM], conv_w, conv_b))
    xs = xbc[..., :D_INNER].reshape(b, l, N_GROUPS, HEADS_PER_GROUP, HEAD_DIM)
    bm = xbc[..., D_INNER:D_INNER + GS].reshape(b, l, N_GROUPS, D_STATE)
    cm = xbc[..., D_INNER + GS:CONV_DIM].reshape(b, l, N_GROUPS, D_STATE)
    dt_raw = u[..., DT_START:Z_START].reshape(b, l, 2, N_GROUPS, HEADS_PER_GROUP)
    return xs, bm, cm, dt_raw


def decay_matrix(a_cum):
    q = a_cum.shape[-1]
    diff = a_cum[..., :, None] - a_cum[..., None, :]
    mask = jnp.tril(jnp.ones((q, q), dtype=bool))
    return jnp.exp(jnp.where(mask, diff, -jnp.inf))


def ssd_chunked(xs, dt, a, bm, cm, init_state, need_y):
    b, l = xs.shape[:2]
    nc = l // CHUNK
    xdt = (xs.astype(jnp.float32) * dt[..., None]).reshape(
        b, nc, CHUNK, N_GROUPS, HEADS_PER_GROUP, HEAD_DIM)
    a_cum = jnp.cumsum((dt * a).reshape(b, nc, CHUNK, N_GROUPS, HEADS_PER_GROUP), axis=2)
    bc = bm.astype(jnp.float32).reshape(b, nc, CHUNK, N_GROUPS, D_STATE)
    cc = cm.astype(jnp.float32).reshape(b, nc, CHUNK, N_GROUPS, D_STATE)
    decay_to_end = jnp.exp(a_cum[:, :, -1:] - a_cum)
    states = jnp.einsum('bcsgn,bcsgj,bcsgjp->bcgjpn', bc, decay_to_end, xdt)
    chunk_decay = jnp.exp(a_cum[:, :, -1])

    def step(carry, inp):
        st, dec = inp
        return carry * dec[..., None, None] + st, carry

    final, entering = lax.scan(step, init_state,
                               (jnp.moveaxis(states, 1, 0), jnp.moveaxis(chunk_decay, 1, 0)))
    if not need_y:
        return None, final
    entering = jnp.moveaxis(entering, 0, 1)
    cb = jnp.einsum('bclgn,bcsgn->bcgls', cc, bc)
    lmat = decay_matrix(jnp.moveaxis(a_cum, 2, -1))
    y_diag = jnp.einsum('bcgjls,bcsgjp->bclgjp', cb[:, :, :, None] * lmat, xdt)
    y_off = jnp.einsum('bclgn,bcgjpn,bclgj->bclgjp', cc, entering, jnp.exp(a_cum))
    y = (y_diag + y_off).reshape(b, l, N_GROUPS, HEADS_PER_GROUP, HEAD_DIM)
    return y, final


def ssd_bidirectional(lat_in, ctx_in, dt_bias, a_log, need_ctx_y):
    xs, bm, cm, dtr = lat_in
    xs_c, bm_c, cm_c, dtr_c = ctx_in
    b = xs.shape[0]
    ys_lat, ys_ctx = [], []
    for d in range(2):
        flip = (lambda t: jnp.flip(t, axis=1)) if d == 1 else (lambda t: t)
        a = -jnp.exp(a_log[d].astype(jnp.float32)).reshape(N_GROUPS, HEADS_PER_GROUP)
        bias = dt_bias[d].astype(jnp.float32).reshape(N_GROUPS, HEADS_PER_GROUP)
        dt_c = jax.nn.softplus(dtr_c[:, :, d].astype(jnp.float32) + bias)
        dt_l = jax.nn.softplus(dtr[:, :, d].astype(jnp.float32) + bias)
        init = jnp.zeros((b, N_GROUPS, HEADS_PER_GROUP, HEAD_DIM, D_STATE), jnp.float32)
        y_c, s_ctx = ssd_chunked(flip(xs_c), flip(dt_c), a, flip(bm_c), flip(cm_c), init, need_ctx_y)
        y_l, _ = ssd_chunked(flip(xs), flip(dt_l), a, flip(bm), flip(cm), s_ctx, True)
        ys_lat.append(flip(y_l))
        if need_ctx_y:
            ys_ctx.append(flip(y_c))
    y_ctx = ys_ctx[0] + ys_ctx[1] if need_ctx_y else None
    return ys_lat[0] + ys_lat[1], y_ctx


def ssd_output(y, xs, z, d_skip, gain):
    b, l = y.shape[:2]
    y = y + xs.astype(jnp.float32) * d_skip.astype(jnp.float32).reshape(N_GROUPS, HEADS_PER_GROUP, 1)
    y = y.reshape(b, l, D_INNER) * jax.nn.silu(z.astype(jnp.float32))
    y = y.reshape(b, l, N_GROUPS, D_INNER // N_GROUPS)
    y = y * lax.rsqrt(jnp.mean(y * y, axis=-1, keepdims=True) + EPS)
    return (y.reshape(b, l, D_INNER) * gain.astype(jnp.float32)).astype(z.dtype)


def fourier_grid(f, rows):
    b, l, _ = f.shape
    fg = f.astype(jnp.float32).reshape(b, rows, GRID_W, FFT_GROUPS, FFT_GROUP_DIM)
    out = jnp.fft.fftn(fg, axes=(1, 2, 4), norm='ortho').real
    return out.reshape(b, l, FFT_WIDTH).astype(f.dtype)


def fourier_seq(f):
    b, l, _ = f.shape
    fg = f.astype(jnp.float32).reshape(b, l, FFT_GROUPS, FFT_GROUP_DIM)
    out = jnp.fft.fftn(fg, axes=(1, 3), norm='ortho').real
    return out.reshape(b, l, FFT_WIDTH).astype(f.dtype)


def merge_branches(u_gate, y_ssd, f_mix, w_ssd_out, w_fft_out, w_o):
    b, l, _ = u_gate.shape
    gates = jax.nn.sigmoid(u_gate.astype(jnp.float32)).astype(y_ssd.dtype)
    gates = gates.reshape(b, l, N_BRANCH, D_MODEL)
    merged = gates[:, :, 0] * (y_ssd @ w_ssd_out) + gates[:, :, 1] * (f_mix @ w_fft_out)
    return merged @ w_o


def swiglu(h, w_in, w_out):
    gu = h @ w_in
    return (jax.nn.silu(gu[..., :D_FF]) * gu[..., D_FF:]) @ w_out


def setup_inputs(seed: int = 0) -> dict:
    key = jax.random.key(seed)
    ks = jax.random.split(key, 22)

    def nrm(k, shape, s):
        return jax.random.normal(k, shape, jnp.float32) * s

    x = nrm(ks[0], (BATCH, SEQ, D_MODEL), 1.0)
    c = nrm(ks[1], (BATCH, D_MODEL), 1.0)
    ctx = nrm(ks[2], (BATCH, CTX_LEN, D_MODEL), 1.0)
    c_ctx = nrm(ks[3], (D_MODEL,), 1.0)
    w_ada = nrm(ks[4], (DEPTH, D_MODEL, 6 * D_MODEL), 0.5 * D_MODEL ** -0.5)
    b_ada = nrm(ks[5], (DEPTH, 6 * D_MODEL), 0.01)
    norm1_g = 1.0 + nrm(ks[6], (DEPTH, D_MODEL), 0.01)
    w_in = nrm(ks[7], (DEPTH, D_MODEL, IN_WIDTH), D_MODEL ** -0.5)
    conv_w = nrm(ks[8], (DEPTH, D_CONV, CONV_DIM), D_CONV ** -0.5)
    conv_b = nrm(ks[9], (DEPTH, CONV_DIM), 0.01)
    dt = jnp.exp(jax.random.uniform(ks[10], (DEPTH, 2, N_HEADS), jnp.float32,
                                    minval=math.log(DT_MIN), maxval=math.log(DT_MAX)))
    dt_bias = dt + jnp.log(-jnp.expm1(-dt))
    a_log = jnp.log(jax.random.uniform(ks[11], (DEPTH, 2, N_HEADS), jnp.float32,
                                       minval=1.0, maxval=16.0))
    d_skip = 1.0 + nrm(ks[12], (DEPTH, N_HEADS), 0.01)
    ssd_norm_g = 1.0 + nrm(ks[13], (DEPTH, D_INNER), 0.01)
    w_ssd_out = nrm(ks[14], (DEPTH, D_INNER, D_MODEL), D_INNER ** -0.5)
    w_fft_out = nrm(ks[15], (DEPTH, FFT_WIDTH, D_MODEL), FFT_WIDTH ** -0.5)
    w_o = nrm(ks[16], (DEPTH, D_MODEL, D_MODEL), D_MODEL ** -0.5)
    norm2_g = 1.0 + nrm(ks[17], (DEPTH, D_MODEL), 0.01)
    w_ffn_in = nrm(ks[18], (DEPTH, D_MODEL, 2 * D_FF), D_MODEL ** -0.5)
    w_ffn_out = nrm(ks[19], (DEPTH, D_FF, D_MODEL), D_FF ** -0.5)
    final_g = 1.0 + nrm(ks[20], (D_MODEL,), 0.01)
    return {'x': x, 'c': c, 'ctx': ctx, 'c_ctx': c_ctx, 'w_ada': w_ada, 'b_ada': b_ada,
            'norm1_g': norm1_g, 'w_in': w_in, 'conv_w': conv_w, 'conv_b': conv_b,
            'dt_bias': dt_bias, 'a_log': a_log, 'd_skip': d_skip, 'ssd_norm_g': ssd_norm_g,
            'w_ssd_out': w_ssd_out, 'w_fft_out': w_fft_out, 'w_o': w_o, 'norm2_g': norm2_g,
            'w_ffn_in': w_ffn_in, 'w_ffn_out': w_ffn_out, 'final_g': final_g}


def reference(x, c, ctx, c_ctx, w_ada, b_ada, norm1_g, w_in, conv_w, conv_b, dt_bias, a_log,
              d_skip, ssd_norm_g, w_ssd_out, w_fft_out, w_o, norm2_g, w_ffn_in, w_ffn_out,
              final_g):
    rows = x.shape[1] // GRID_W
    for i in range(DEPTH):
        last = i == DEPTH - 1
        sh1, sc1, g1, sh2, sc2, g2 = jnp.split((jax.nn.silu(c) @ w_ada[i] + b_ada[i])[:, None, :], 6, axis=-1)
        cmods = jnp.split((jax.nn.silu(c_ctx) @ w_ada[i] + b_ada[i])[None, None, :], 6, axis=-1)

        h = modulate(rmsnorm(x, norm1_g[i]), sh1, sc1)
        hc = modulate(rmsnorm(ctx, norm1_g[i]), cmods[0], cmods[1])
        u = h @ w_in[i]
        uc = hc @ (w_in[i][:, :SSD_IN_WIDTH] if last else w_in[i])

        lat_in = ssd_inputs(u, conv_w[i], conv_b[i])
        ctx_in = ssd_inputs(uc, conv_w[i], conv_b[i])
        y_lat, y_ctx = ssd_bidirectional(lat_in, ctx_in, dt_bias[i], a_log[i], not last)
        y_lat = ssd_output(y_lat, lat_in[0], u[..., Z_START:FFT_START], d_skip[i], ssd_norm_g[i])
        f_lat = fourier_grid(u[..., FFT_START:GATE_START], rows)
        mix = merge_branches(u[..., GATE_START:], y_lat, f_lat, w_ssd_out[i], w_fft_out[i], w_o[i])
        x_new = x + g1 * mix

        x_new = x_new + g2 * swiglu(modulate(rmsnorm(x_new, norm2_g[i]), sh2, sc2),
                                    w_ffn_in[i], w_ffn_out[i])

        if not last:
            y_c = ssd_output(y_ctx, ctx_in[0], uc[..., Z_START:FFT_START], d_skip[i], ssd_norm_g[i])
            f_c = fourier_seq(uc[..., FFT_START:GATE_START])
            mix_c = merge_branches(uc[..., GATE_START:], y_c, f_c, w_ssd_out[i], w_fft_out[i], w_o[i])
            ctx = ctx + cmods[2] * mix_c
            ctx = ctx + cmods[5] * swiglu(modulate(rmsnorm(ctx, norm2_g[i]), cmods[3], cmods[4]),
                                          w_ffn_in[i], w_ffn_out[i])
        x = x_new
    return rmsnorm(x, final_g)
```

```python
import functools

import jax
import jax.numpy as jnp
import numpy as np
from jax import lax
from jax.experimental import pallas as pl
from jax.experimental.pallas import tpu as pltpu

F32 = jnp.float32
BF16 = jnp.bfloat16

D_MODEL = 1024
EPS = 1e-6
GRID_W = 64

N_GROUPS = 8
HEADS_PER_GROUP = 4
HEAD_DIM = 64
N_HEADS = N_GROUPS * HEADS_PER_GROUP
D_STATE = 128
D_INNER = N_HEADS * HEAD_DIM
GROUP_W = HEADS_PER_GROUP * HEAD_DIM
GS = N_GROUPS * D_STATE
CONV_DIM = D_INNER + 2 * GS
D_CONV = 5
CHUNK = 128
XBC_GW = GROUP_W + 2 * D_STATE

FFT_GROUPS = 8
FFT_GROUP_DIM = 128
D_FF = 2816

REF_DT = CONV_DIM
REF_Z = REF_DT + 2 * N_HEADS
REF_FFT = REF_Z + D_INNER
REF_GATE = REF_FFT + D_MODEL

U_Z = CONV_DIM
U_FFT = U_Z + D_INNER
U_GATE = U_FFT + D_MODEL
U_WIDTH = U_GATE + 2 * D_MODEL

ROWS_PER_GROUP = 32
Q_CF, Q_CB, Q_WXF, Q_WXB, Q_EYF, Q_EYB, Q_DTF, Q_DTB = range(8)

LANE = 128
HALO = 16
VMEM_LIMIT = 56 * 1024 * 1024


def _dot(a, b):
    return jnp.dot(a, b, preferred_element_type=F32)


def _rmsnorm(x, g):
    return x * lax.rsqrt(jnp.mean(x * x, axis=-1, keepdims=True) + EPS) * g


def _mods_kernel(ct_ref, w_ref, b_ref, o_ref, *, n_vecs):
    s = ct_ref[...]
    s = s * jax.nn.sigmoid(s)
    w = w_ref[...]
    rows = [jnp.sum(w * s[:, r:r + 1], axis=0, keepdims=True) for r in range(n_vecs)]
    rows.append(jnp.zeros((8 - n_vecs, w.shape[1]), F32))
    o_ref[...] = jnp.concatenate(rows, axis=0) + b_ref[...]


def _mods(cvecs, n_vecs, w_ada, b_ada):
    n = w_ada.shape[1]
    tn = 1536
    return pl.pallas_call(
        functools.partial(_mods_kernel, n_vecs=n_vecs),
        grid=(n // tn,),
        in_specs=[pl.BlockSpec((D_MODEL, 8), lambda j: (0, 0)),
                  pl.BlockSpec((D_MODEL, tn), lambda j: (0, j)),
                  pl.BlockSpec((1, tn), lambda j: (0, j))],
        out_specs=pl.BlockSpec((8, tn), lambda j: (0, j)),
        out_shape=jax.ShapeDtypeStruct((8, n), F32),
        compiler_params=pltpu.CompilerParams(dimension_semantics=("arbitrary",),
                                             vmem_limit_bytes=VMEM_LIMIT),
        name="mods",
    )(cvecs.T, w_ada, b_ada.reshape(1, n))


def _inproj_kernel(x_ref, sh_ref, sc_ref, g_ref, w_ref, wdt_ref, u_ref, dt_ref, h_ref):
    @pl.when(pl.program_id(1) == 0)
    def _():
        h = _rmsnorm(x_ref[...], g_ref[...]) * (1.0 + sc_ref[0]) + sh_ref[0]
        hb = h.astype(BF16)
        h_ref[...] = hb
        hl = (h - hb.astype(F32)).astype(BF16)
        a = _dot(hb, wdt_ref[...])
        dt_ref[...] = a[:, :LANE] + a[:, LANE:] + _dot(hl, wdt_ref[:, :LANE])

    u_ref[...] = _dot(h_ref[...], w_ref[...]).astype(BF16)


def _inproj(x2d, shift, scale, gain, w_main, w_dt, n_cols, tm, rows_per_mod):
    m = x2d.shape[0]
    tn = 1024
    bpm = rows_per_mod // tm
    return pl.pallas_call(
        _inproj_kernel,
        grid=(m // tm, n_cols // tn),
        in_specs=[pl.BlockSpec((tm, D_MODEL), lambda i, j: (i, 0)),
                  pl.BlockSpec((1, 1, D_MODEL), lambda i, j: (i // bpm, 0, 0)),
                  pl.BlockSpec((1, 1, D_MODEL), lambda i, j: (i // bpm, 0, 0)),
                  pl.BlockSpec((1, D_MODEL), lambda i, j: (0, 0)),
                  pl.BlockSpec((D_MODEL, tn), lambda i, j: (0, j)),
                  pl.BlockSpec((D_MODEL, 2 * LANE), lambda i, j: (0, 0))],
        out_specs=[pl.BlockSpec((tm, tn), lambda i, j: (i, j)),
                   pl.BlockSpec((tm, LANE), lambda i, j: (i, 0))],
        out_shape=[jax.ShapeDtypeStruct((m, n_cols), BF16),
                   jax.ShapeDtypeStruct((m, LANE), F32)],
        scratch_shapes=[pltpu.VMEM((tm, D_MODEL), BF16)],
        compiler_params=pltpu.CompilerParams(dimension_semantics=("arbitrary", "arbitrary"),
                                             vmem_limit_bytes=VMEM_LIMIT),
        name="inproj",
    )(x2d, shift, scale, gain, w_main, w_dt)


def _dtprep_kernel(raw_ref, bias_ref, alog_ref, row_ref, *, n_chunks):
    a = -jnp.exp(alog_ref[...])
    bias = bias_ref[...]
    ii = lax.broadcasted_iota(jnp.int32, (CHUNK, CHUNK), 0)
    jj = lax.broadcasted_iota(jnp.int32, (CHUNK, CHUNK), 1)
    tril = (jj <= ii).astype(F32)
    triu = (jj >= ii).astype(F32)
    lane = lax.broadcasted_iota(jnp.int32, (1, LANE), 1)
    is_fwd = (lane % 8) < HEADS_PER_GROUP
    for c in range(n_chunks):
        sl = slice(c * CHUNK, (c + 1) * CHUNK)
        dt = jax.nn.softplus(raw_ref[0, sl, :] + bias)
        dta = dt * a
        cf = jnp.dot(tril, dta, precision=lax.Precision.HIGHEST, preferred_element_type=F32)
        cb = jnp.dot(triu, dta, precision=lax.Precision.HIGHEST, preferred_element_type=F32)
        cum = jnp.where(is_fwd, cf, cb)
        tot = jnp.where(is_fwd, cf[CHUNK - 1:CHUNK, :], cb[0:1, :])
        wx = dt * jnp.exp(tot - cum)
        ey = jnp.exp(cum)
        for q2, val in enumerate((cum, wx, ey, dt)):
            vt = val.T
            for g in range(N_GROUPS):
                r0 = g * ROWS_PER_GROUP + q2 * 8
                row_ref[0, r0:r0 + 8, sl] = vt[g * 8:(g + 1) * 8, :]


def _dtprep(raw, bias, alog, tl):
    b, l, _ = raw.shape
    return pl.pallas_call(
        functools.partial(_dtprep_kernel, n_chunks=tl // CHUNK),
        grid=(b, l // tl),
        in_specs=[pl.BlockSpec((1, tl, LANE), lambda i, j: (i, j, 0)),
                  pl.BlockSpec((1, LANE), lambda i, j: (0, 0)),
                  pl.BlockSpec((1, LANE), lambda i, j: (0, 0))],
        out_specs=pl.BlockSpec((1, N_GROUPS * ROWS_PER_GROUP, tl), lambda i, j: (i, 0, j)),
        out_shape=jax.ShapeDtypeStruct((b, N_GROUPS * ROWS_PER_GROUP, l), F32),
        compiler_params=pltpu.CompilerParams(dimension_semantics=("arbitrary", "arbitrary"),
                                             vmem_limit_bytes=VMEM_LIMIT),
        name="dtprep",
    )(raw, bias, alog)


def _expand_heads(colt, q):
    lo = lax.broadcasted_iota(jnp.int32, (CHUNK, LANE), 1) < HEAD_DIM
    c = [jnp.broadcast_to(colt[:, q * 4 + j:q * 4 + j + 1], (CHUNK, LANE)) for j in range(HEADS_PER_GROUP)]
    return jnp.concatenate([jnp.where(lo, c[0], c[1]), jnp.where(lo, c[2], c[3])], axis=1)


def _conv_silu(win, w, b):
    acc = b
    for k in range(D_CONV):
        acc = acc + w[k:k + 1, :] * win[6 + k:6 + k + CHUNK, :]
    return acc * jax.nn.sigmoid(acc)


def _state_update(bm, xs, colt, q_wx):
    xw = (xs * _expand_heads(colt, q_wx)).astype(BF16)
    return lax.dot_general(bm, xw, (((0,), (0,)), ((), ())), preferred_element_type=F32)


def _ssd_kernel(xbc_ref, prev_ref, next_ref, z_ref, row_ref, cxbc_ref, crow_ref,
                cw_ref, cbias_ref, dskip_ref, gain_ref, out_ref,
                cache_ref, yacc_ref, sf_ref, sb_ref, ext_ref, *, tl, n_blk, ctx_len):
    p = pl.program_id(2)
    i = pl.program_id(3)
    n_ch = tl // CHUNK
    cw = cw_ref[...]
    cbias = cbias_ref[...]

    @pl.when(jnp.logical_and(p == 0, i == 0))
    def _init_from_context():
        zpad = jnp.zeros((HALO, XBC_GW), F32)
        cext = jnp.concatenate([zpad, cxbc_ref[0].astype(F32), zpad], axis=0)
        n_cc = ctx_len // CHUNK
        sf = jnp.zeros((D_STATE, GROUP_W), F32)
        st_b = []
        ey_b = []
        for cc in range(n_cc):
            win = cext[cc * CHUNK + 8:cc * CHUNK + 8 + CHUNK + 16, :]
            xbc = _conv_silu(win, cw, cbias)
            xs = xbc[:, :GROUP_W]
            bm = xbc[:, GROUP_W:GROUP_W + D_STATE].astype(BF16)
            colt = crow_ref[0, :, cc * CHUNK:(cc + 1) * CHUNK].T
            ef = _expand_heads(colt, Q_EYF)
            sf = sf * ef[CHUNK - 1:CHUNK, :] + _state_update(bm, xs, colt, Q_WXF)
            st_b.append(_state_update(bm, xs, colt, Q_WXB))
            ey_b.append(_expand_heads(colt, Q_EYB)[0:1, :])
        sb = jnp.zeros((D_STATE, GROUP_W), F32)
        for cc in reversed(range(n_cc)):
            sb = sb * ey_b[cc] + st_b[cc]
        sf_ref[...] = sf
        sb_ref[...] = sb

    ii = lax.broadcasted_iota(jnp.int32, (CHUNK, CHUNK), 0)
    jj = lax.broadcasted_iota(jnp.int32, (CHUNK, CHUNK), 1)
    lower = jj < ii
    upper = jj > ii
    lower_eq = jj <= ii

    @pl.when(p == 0)
    def _forward():
        blk = i
        base = blk * tl
        ext_ref[0:HALO, :] = jnp.where(blk > 0, prev_ref[0].astype(F32), 0.0)
        ext_ref[HALO:HALO + tl, :] = xbc_ref[0].astype(F32)
        ext_ref[HALO + tl:2 * HALO + tl, :] = jnp.where(blk < n_blk - 1, next_ref[0].astype(F32), 0.0)
        dskip = dskip_ref[...]

        def chunk(c, carry):
            r = pl.multiple_of(c * CHUNK, CHUNK)
            win = ext_ref[pl.ds(r + 8, CHUNK + 16), :]
            xbc = _conv_silu(win, cw, cbias)
            xbc_b = xbc.astype(BF16)
            cache_ref[pl.ds(base + r, CHUNK), :] = xbc_b
            xs = xbc[:, :GROUP_W]
            xs_b = xbc_b[:, :GROUP_W]
            bm = xbc_b[:, GROUP_W:GROUP_W + D_STATE]
            cm = xbc_b[:, GROUP_W + D_STATE:]
            rowt = row_ref[0, :, pl.ds(r, CHUNK)]
            colt = rowt.T
            cbm = lax.dot_general(cm, bm, (((1,), (1,)), ((), ())), preferred_element_type=F32)
            yd = []
            for j in range(HEADS_PER_GROUP):
                cf_c = colt[:, Q_CF * 4 + j:Q_CF * 4 + j + 1]
                cb_c = colt[:, Q_CB * 4 + j:Q_CB * 4 + j + 1]
                cf_r = rowt[Q_CF * 4 + j:Q_CF * 4 + j + 1, :]
                cb_r = rowt[Q_CB * 4 + j:Q_CB * 4 + j + 1, :]
                dtf_r = rowt[Q_DTF * 4 + j:Q_DTF * 4 + j + 1, :]
                dtb_r = rowt[Q_DTB * 4 + j:Q_DTB * 4 + j + 1, :]
                arg = jnp.where(lower_eq, cf_c - cf_r, cb_c - cb_r)
                dsel = jnp.where(lower, dtf_r, jnp.where(upper, dtb_r, dtf_r + dtb_r))
                wmat = (cbm * jnp.exp(arg) * dsel).astype(BF16)
                yd.append(_dot(wmat, xs_b[:, j * HEAD_DIM:(j + 1) * HEAD_DIM]))
            y = jnp.concatenate(yd, axis=1)
            ef = _expand_heads(colt, Q_EYF)
            sf = sf_ref[...]
            y = y + _dot(cm, sf.astype(BF16)) * ef + xs * dskip
            yacc_ref[pl.ds(base + r, CHUNK), :] = y
            sf_ref[...] = sf * ef[CHUNK - 1:CHUNK, :] + _state_update(bm, xs, colt, Q_WXF)
            return carry

        lax.fori_loop(0, n_ch, chunk, 0)

    @pl.when(p == 1)
    def _backward():
        blk = n_blk - 1 - i
        base = blk * tl
        gain = gain_ref[...]

        def chunk(t, carry):
            c = n_ch - 1 - t
            r = pl.multiple_of(c * CHUNK, CHUNK)
            xbc_b = cache_ref[pl.ds(base + r, CHUNK), :]
            xs = xbc_b[:, :GROUP_W].astype(F32)
            bm = xbc_b[:, GROUP_W:GROUP_W + D_STATE]
            cm = xbc_b[:, GROUP_W + D_STATE:]
            colt = row_ref[0, :, pl.ds(r, CHUNK)].T
            eb = _expand_heads(colt, Q_EYB)
            sb = sb_ref[...]
            y = yacc_ref[pl.ds(base + r, CHUNK), :] + _dot(cm, sb.astype(BF16)) * eb
            sb_ref[...] = sb * eb[0:1, :] + _state_update(bm, xs, colt, Q_WXB)
            z = z_ref[0, pl.ds(r, CHUNK), :].astype(F32)
            y = y * (z * jax.nn.sigmoid(z))
            out_ref[0, pl.ds(r, CHUNK), :] = _rmsnorm(y, gain).astype(BF16)
            return carry

        lax.fori_loop(0, n_ch, chunk, 0)


def _ssd(u, rows, cu, crows, conv_w, conv_b, dskip, gain, tl):
    b, l, _ = u.shape
    ctx_len = cu.shape[1]
    n_blk = l // tl
    hb = tl // HALO
    last_halo = l // HALO - 1

    def in_blk(p, i):
        return i * (1 - p) + (n_blk - 1) * p

    def seq_blk(p, i):
        return i * (1 - p) + (n_blk - 1 - i) * p

    def out_blk(p, i):
        return (n_blk - 1) * (1 - p) + (n_blk - 1 - i) * p

    grid = (b, N_GROUPS, 2, n_blk)
    in_specs = [
        pl.BlockSpec((1, tl, XBC_GW), lambda bb, g, p, i: (bb, in_blk(p, i), g)),
        pl.BlockSpec((1, HALO, XBC_GW), lambda bb, g, p, i: (bb, jnp.maximum(in_blk(p, i) * hb - 1, 0), g)),
        pl.BlockSpec((1, HALO, XBC_GW),
                     lambda bb, g, p, i: (bb, jnp.minimum((in_blk(p, i) + 1) * hb, last_halo), g)),
        pl.BlockSpec((1, tl, GROUP_W), lambda bb, g, p, i: (bb, out_blk(p, i), U_Z // GROUP_W + g)),
        pl.BlockSpec((1, ROWS_PER_GROUP, tl), lambda bb, g, p, i: (bb, g, seq_blk(p, i))),
        pl.BlockSpec((1, ctx_len, XBC_GW), lambda bb, g, p, i: (bb, 0, g)),
        pl.BlockSpec((1, ROWS_PER_GROUP, ctx_len), lambda bb, g, p, i: (bb, g, 0)),
        pl.BlockSpec((8, XBC_GW), lambda bb, g, p, i: (0, g)),
        pl.BlockSpec((1, XBC_GW), lambda bb, g, p, i: (0, g)),
        pl.BlockSpec((1, GROUP_W), lambda bb, g, p, i: (0, g)),
        pl.BlockSpec((1, GROUP_W), lambda bb, g, p, i: (0, g)),
    ]
    return pl.pallas_call(
        functools.partial(_ssd_kernel, tl=tl, n_blk=n_blk, ctx_len=ctx_len),
        grid=grid,
        in_specs=in_specs,
        out_specs=pl.BlockSpec((1, tl, GROUP_W), lambda bb, g, p, i: (bb, out_blk(p, i), g)),
        out_shape=jax.ShapeDtypeStruct((b, l, D_INNER), BF16),
        scratch_shapes=[pltpu.VMEM((l, XBC_GW), BF16),
                        pltpu.VMEM((l, GROUP_W), F32),
                        pltpu.VMEM((D_STATE, GROUP_W), F32),
                        pltpu.VMEM((D_STATE, GROUP_W), F32),
                        pltpu.VMEM((tl + 2 * HALO, XBC_GW), F32)],
        compiler_params=pltpu.CompilerParams(
            dimension_semantics=("arbitrary", "arbitrary", "arbitrary", "arbitrary"),
            vmem_limit_bytes=VMEM_LIMIT),
        name="ssd",
    )(u, u, u, u, rows, cu, crows, conv_w, conv_b, dskip, gain)


def _dft_mats():
    def cs(n):
        k = np.arange(n)
        ang = 2.0 * np.pi * np.outer(k, k) / n
        return np.cos(ang), np.sin(ang)

    c3, s3 = cs(FFT_GROUP_DIM)
    w1 = np.concatenate([c3, -s3], axis=1)
    c2, s2 = cs(GRID_W)
    m2 = np.block([[c2, s2], [-s2, c2]])
    return w1, m2


def _fft12_kernel(f_ref, w1_ref, m2_ref, o_ref, *, tm):
    f = f_ref[0]
    w1 = w1_ref[...]
    ps, qs = [], []
    for g in range(FFT_GROUPS):
        pq = _dot(f[:, g * FFT_GROUP_DIM:(g + 1) * FFT_GROUP_DIM], w1)
        ps.append(pq[:, :FFT_GROUP_DIM])
        qs.append(pq[:, FFT_GROUP_DIM:])
    pr = jnp.concatenate(ps, axis=1).astype(BF16)
    qr = jnp.concatenate(qs, axis=1).astype(BF16)
    m2 = m2_ref[...]
    for r in range(tm // GRID_W):
        sl = slice(r * GRID_W, (r + 1) * GRID_W)
        o = _dot(m2, jnp.concatenate([pr[sl], qr[sl]], axis=0))
        o_ref[0, 0, sl, :] = o[:GRID_W].astype(BF16)
        o_ref[0, 1, sl, :] = o[GRID_W:].astype(BF16)


def _fft3_kernel(ri_ref, m3_ref, o_ref):
    o_ref[0] = _dot(m3_ref[...], ri_ref[0]).astype(BF16)


def _fourier(u, tm):
    b, l, _ = u.shape
    rows = l // GRID_W
    w1, m2 = _dft_mats()
    ri = pl.pallas_call(
        functools.partial(_fft12_kernel, tm=tm),
        grid=(b, l // tm),
        in_specs=[pl.BlockSpec((1, tm, D_MODEL), lambda i, j: (i, j, U_FFT // D_MODEL)),
                  pl.BlockSpec((FFT_GROUP_DIM, 2 * FFT_GROUP_DIM), lambda i, j: (0, 0)),
                  pl.BlockSpec((2 * GRID_W, 2 * GRID_W), lambda i, j: (0, 0))],
        out_specs=pl.BlockSpec((1, 2, tm, D_MODEL), lambda i, j: (i, 0, j, 0)),
        out_shape=jax.ShapeDtypeStruct((b, 2, l, D_MODEL), BF16),
        compiler_params=pltpu.CompilerParams(dimension_semantics=("arbitrary", "arbitrary"),
                                             vmem_limit_bytes=VMEM_LIMIT),
        name="fft12",
    )(u, jnp.asarray(w1, BF16), jnp.asarray(m2, BF16))

    k = np.arange(rows)
    ang = 2.0 * np.pi * np.outer(k, k) / rows
    scale = 1.0 / np.sqrt(float(rows * GRID_W * FFT_GROUP_DIM))
    m3 = np.concatenate([np.cos(ang), np.sin(ang)], axis=1) * scale
    wide = GRID_W * D_MODEL
    tn = 4096
    out = pl.pallas_call(
        _fft3_kernel,
        grid=(b, wide // tn),
        in_specs=[pl.BlockSpec((1, 2 * rows, tn), lambda i, j: (i, 0, j)),
                  pl.BlockSpec((rows, 2 * rows), lambda i, j: (0, 0))],
        out_specs=pl.BlockSpec((1, rows, tn), lambda i, j: (i, 0, j)),
        out_shape=jax.ShapeDtypeStruct((b, rows, wide), BF16),
        compiler_params=pltpu.CompilerParams(dimension_semantics=("arbitrary", "arbitrary"),
                                             vmem_limit_bytes=VMEM_LIMIT),
        name="fft3",
    )(ri.reshape(b, 2 * rows, wide), jnp.asarray(m3, BF16))
    return out.reshape(b, l, D_MODEL)


FF_SPLIT = 2


def _tail_kernel(y_ref, f_ref, ga_ref, gb_ref, x_ref, g1_ref, sh2_ref, sc2_ref, g2_ref,
                 wso_ref, wfo_ref, wo_ref, n2_ref, wfi_ref, wfu_ref, wfd_ref, fg_ref, o_ref):
    a = _dot(y_ref[...], wso_ref[...])
    bq = _dot(f_ref[...], wfo_ref[...])
    merged = jax.nn.sigmoid(ga_ref[...].astype(F32)) * a + jax.nn.sigmoid(gb_ref[...].astype(F32)) * bq
    xn = x_ref[...] + g1_ref[0] * _dot(merged.astype(BF16), wo_ref[...])
    h = (_rmsnorm(xn, n2_ref[...]) * (1.0 + sc2_ref[0]) + sh2_ref[0]).astype(BF16)
    ffw = D_FF // FF_SPLIT
    acc = None
    for s in range(FF_SPLIT):
        sl = slice(s * ffw, (s + 1) * ffw)
        gate = _dot(h, wfi_ref[:, sl])
        up = _dot(h, wfu_ref[:, sl])
        act = (gate * jax.nn.sigmoid(gate) * up).astype(BF16)
        part = _dot(act, wfd_ref[sl, :])
        acc = part if acc is None else acc + part
    o_ref[...] = _rmsnorm(xn + g2_ref[0] * acc, fg_ref[...])


def _tail(y2d, f2d, u2d, x2d, g1, sh2, sc2, g2, wso, wfo, wo, n2, wfi, wfu, wfd, fg, tm, rows_per_mod):
    m = x2d.shape[0]
    bpm = rows_per_mod // tm

    def const(shape):
        return pl.BlockSpec(shape, lambda i: (0, 0), pipeline_mode=pl.Buffered(1))

    def mod():
        return pl.BlockSpec((1, 1, D_MODEL), lambda i: (i // bpm, 0, 0))

    return pl.pallas_call(
        _tail_kernel,
        grid=(m // tm,),
        in_specs=[pl.BlockSpec((tm, D_INNER), lambda i: (i, 0)),
                  pl.BlockSpec((tm, D_MODEL), lambda i: (i, 0)),
                  pl.BlockSpec((tm, D_MODEL), lambda i: (i, U_GATE // D_MODEL)),
                  pl.BlockSpec((tm, D_MODEL), lambda i: (i, U_GATE // D_MODEL + 1)),
                  pl.BlockSpec((tm, D_MODEL), lambda i: (i, 0)),
                  mod(), mod(), mod(), mod(),
                  const((D_INNER, D_MODEL)), const((D_MODEL, D_MODEL)), const((D_MODEL, D_MODEL)),
                  const((1, D_MODEL)),
                  const((D_MODEL, D_FF)), const((D_MODEL, D_FF)), const((D_FF, D_MODEL)),
                  const((1, D_MODEL))],
        out_specs=pl.BlockSpec((tm, D_MODEL), lambda i: (i, 0)),
        out_shape=jax.ShapeDtypeStruct((m, D_MODEL), F32),
        compiler_params=pltpu.CompilerParams(dimension_semantics=("arbitrary",),
                                             vmem_limit_bytes=VMEM_LIMIT),
        name="tail",
    )(y2d, f2d, u2d, u2d, x2d, g1, sh2, sc2, g2, wso, wfo, wo, n2, wfi, wfu, wfd, fg)


def _xbc_perm():
    cols = []
    for g in range(N_GROUPS):
        cols += list(range(g * GROUP_W, (g + 1) * GROUP_W))
        cols += list(range(D_INNER + g * D_STATE, D_INNER + (g + 1) * D_STATE))
        cols += list(range(D_INNER + GS + g * D_STATE, D_INNER + GS + (g + 1) * D_STATE))
    return np.asarray(cols, np.int32)


def _dt_perm():
    cols = [d * N_HEADS + g * HEADS_PER_GROUP + j
            for g in range(N_GROUPS) for d in range(2) for j in range(HEADS_PER_GROUP)]
    return np.asarray(cols, np.int32)


def _block(x, c, ctx, c_ctx, w_ada, b_ada, norm1_g, w_in, conv_w, conv_b, dt_bias, a_log, d_skip,
           ssd_norm_g, w_ssd_out, w_fft_out, w_o, norm2_g, w_ffn_in, w_ffn_out, final_g,
           proj_tm, prep_tl, ssd_tl, fft_tm, tail_tm):
    b, l, d = x.shape
    ctx_len = ctx.shape[1]
    assert d == D_MODEL and l % (GRID_W * 2) == 0 and ctx_len % CHUNK == 0 and b + 1 <= 8

    w_in0 = w_in[0]
    xperm = _xbc_perm()
    dperm = _dt_perm()
    w_main = jnp.concatenate([w_in0[:, :CONV_DIM][:, xperm], w_in0[:, REF_Z:]], axis=1).astype(BF16)
    w_dt = w_in0[:, REF_DT:REF_Z][:, dperm]
    w_dt_hi = w_dt.astype(BF16)
    w_dt_lo = (w_dt - w_dt_hi.astype(F32)).astype(BF16)
    pad = jnp.zeros((D_MODEL, LANE - 2 * N_HEADS), BF16)
    w_dt2 = jnp.concatenate([w_dt_hi, pad, w_dt_lo, pad], axis=1)
    conv_w_p = jnp.concatenate([conv_w[0][:, xperm], jnp.zeros((8 - D_CONV, CONV_DIM), F32)], axis=0)
    conv_b_p = conv_b[0][xperm].reshape(1, CONV_DIM)
    lane_pad = jnp.zeros((LANE - 2 * N_HEADS,), F32)
    bias_v = jnp.concatenate([dt_bias[0].reshape(-1)[dperm], lane_pad]).reshape(1, LANE)
    alog_v = jnp.concatenate([a_log[0].reshape(-1)[dperm], lane_pad]).reshape(1, LANE)
    dskip_v = jnp.repeat(d_skip[0], HEAD_DIM).reshape(1, D_INNER)
    gain_v = ssd_norm_g[0].reshape(1, D_INNER)
    wso = w_ssd_out[0].astype(BF16)
    wfo = w_fft_out[0].astype(BF16)
    wo = w_o[0].astype(BF16)
    wfi = w_ffn_in[0][:, :D_FF].astype(BF16)
    wfu = w_ffn_in[0][:, D_FF:].astype(BF16)
    wfd = w_ffn_out[0].astype(BF16)

    cvecs = jnp.concatenate([c, c_ctx[None, :], jnp.zeros((8 - b - 1, D_MODEL), F32)], axis=0)
    mods = _mods(cvecs, b + 1, w_ada[0], b_ada[0])
    lat = mods[:b].reshape(b, 6, 1, D_MODEL)
    sh1, sc1, g1, sh2, sc2, g2 = (lat[:, k] for k in range(6))
    cm = mods[b].reshape(6, 1, 1, D_MODEL)

    x2d = x.reshape(b * l, D_MODEL)
    n1 = norm1_g[0].reshape(1, D_MODEL)
    u2d, dt_raw = _inproj(x2d, sh1, sc1, n1, w_main, w_dt2, U_WIDTH, proj_tm, l)
    u = u2d.reshape(b, l, U_WIDTH)
    cu2d, cdt_raw = _inproj(ctx.reshape(b * ctx_len, D_MODEL), cm[0], cm[1], n1, w_main, w_dt2,
                            CONV_DIM, b * ctx_len, b * ctx_len)
    cu = cu2d.reshape(b, ctx_len, CONV_DIM)

    rows = _dtprep(dt_raw.reshape(b, l, LANE), bias_v, alog_v, prep_tl)
    crows = _dtprep(cdt_raw.reshape(b, ctx_len, LANE), bias_v, alog_v, ctx_len)
    y_ssd = _ssd(u, rows, cu, crows, conv_w_p, conv_b_p, dskip_v, gain_v, ssd_tl)

    f_mix = _fourier(u, fft_tm)

    out = _tail(y_ssd.reshape(b * l, D_INNER), f_mix.reshape(b * l, D_MODEL), u2d, x2d,
                g1, sh2, sc2, g2, wso, wfo, wo, norm2_g[0].reshape(1, D_MODEL), wfi, wfu, wfd,
                final_g.reshape(1, D_MODEL), tail_tm, l)
    return out.reshape(b, l, D_MODEL)


def kernel(x, c, ctx, c_ctx, w_ada, b_ada, norm1_g, w_in, conv_w, conv_b, dt_bias, a_log, d_skip,
           ssd_norm_g, w_ssd_out, w_fft_out, w_o, norm2_g, w_ffn_in, w_ffn_out, final_g):
    return _block(x, c, ctx, c_ctx, w_ada, b_ada, norm1_g, w_in, conv_w, conv_b, dt_bias, a_log, d_skip,
                  ssd_norm_g, w_ssd_out, w_fft_out, w_o, norm2_g, w_ffn_in, w_ffn_out, final_g,
                  proj_tm=1024, prep_tl=1024, ssd_tl=2048, fft_tm=512, tail_tm=512)
```

```python
import functools

import jax
import jax.numpy as jnp
import numpy as np
from jax import lax
from jax.experimental import pallas as pl
from jax.experimental.pallas import tpu as pltpu

F32 = jnp.float32
BF16 = jnp.bfloat16

D_MODEL = 1024
EPS = 1e-6
GRID_W = 64

N_GROUPS = 8
HEADS_PER_GROUP = 4
HEAD_DIM = 64
N_HEADS = N_GROUPS * HEADS_PER_GROUP
D_STATE = 128
D_INNER = N_HEADS * HEAD_DIM
GROUP_W = HEADS_PER_GROUP * HEAD_DIM
GS = N_GROUPS * D_STATE
CONV_DIM = D_INNER + 2 * GS
D_CONV = 5
CHUNK = 128
XBC_GW = GROUP_W + 2 * D_STATE

FFT_GROUPS = 8
FFT_GROUP_DIM = 128
D_FF = 2816

REF_DT = CONV_DIM
REF_Z = REF_DT + 2 * N_HEADS
REF_FFT = REF_Z + D_INNER
REF_GATE = REF_FFT + D_MODEL

U_Z = CONV_DIM
U_FFT = U_Z + D_INNER
U_GATE = U_FFT + D_MODEL
U_WIDTH = U_GATE + 2 * D_MODEL

LANE = 128
SUBLANE = 8
HALO = 16
VMEM_LIMIT = 56 * 1024 * 1024

N_DH = 2 * HEADS_PER_GROUP
R_CUM, R_NCX, R_EY, R_WX, R_DIAG = 0, 8, 16, 24, 32
ROWS_PER_GROUP = 40
N_SLAB = XBC_GW // LANE
DT_FLOOR = 1e-37


def _dot(a, b):
    return jnp.dot(a, b, preferred_element_type=F32)


def _dot_tn(a, b):
    return lax.dot_general(a, b, (((0,), (0,)), ((), ())), preferred_element_type=F32)


def _rmsnorm(x, g):
    return x * lax.rsqrt(jnp.mean(x * x, axis=-1, keepdims=True) + EPS) * g


def _mods_kernel(ct_ref, w_ref, b_ref, o_ref, *, n_vecs):
    s = ct_ref[...]
    s = s * jax.nn.sigmoid(s)
    w = w_ref[...]
    rows = [jnp.sum(w * s[:, r:r + 1], axis=0, keepdims=True) for r in range(n_vecs)]
    rows.append(jnp.zeros((SUBLANE - n_vecs, w.shape[1]), F32))
    o_ref[...] = jnp.concatenate(rows, axis=0) + b_ref[...]


def _mods(cvecs, n_vecs, w_ada, b_ada):
    n = w_ada.shape[1]
    tn = 1536
    return pl.pallas_call(
        functools.partial(_mods_kernel, n_vecs=n_vecs),
        grid=(n // tn,),
        in_specs=[pl.BlockSpec((D_MODEL, SUBLANE), lambda j: (0, 0)),
                  pl.BlockSpec((D_MODEL, tn), lambda j: (0, j)),
                  pl.BlockSpec((1, tn), lambda j: (0, j))],
        out_specs=pl.BlockSpec((SUBLANE, tn), lambda j: (0, j)),
        out_shape=jax.ShapeDtypeStruct((SUBLANE, n), F32),
        compiler_params=pltpu.CompilerParams(dimension_semantics=("arbitrary",),
                                             vmem_limit_bytes=VMEM_LIMIT),
        name="mods",
    )(cvecs.T, w_ada, b_ada.reshape(1, n))


def _inproj_kernel(x_ref, sh_ref, sc_ref, g_ref, w_ref, wdt_ref, u_ref, dt_ref, h_ref):
    @pl.when(pl.program_id(1) == 0)
    def _():
        h = _rmsnorm(x_ref[...], g_ref[...]) * (1.0 + sc_ref[0]) + sh_ref[0]
        hb = h.astype(BF16)
        h_ref[...] = hb
        hl = (h - hb.astype(F32)).astype(BF16)
        a = _dot(hb, wdt_ref[...])
        dt_ref[...] = a[:, :LANE] + a[:, LANE:] + _dot(hl, wdt_ref[:, :LANE])

    u_ref[...] = _dot(h_ref[...], w_ref[...]).astype(BF16)


def _inproj(x2d, shift, scale, gain, w_main, w_dt, n_cols, tm, rows_per_mod):
    m = x2d.shape[0]
    tn = 1024
    bpm = rows_per_mod // tm
    return pl.pallas_call(
        _inproj_kernel,
        grid=(m // tm, n_cols // tn),
        in_specs=[pl.BlockSpec((tm, D_MODEL), lambda i, j: (i, 0)),
                  pl.BlockSpec((1, 1, D_MODEL), lambda i, j: (i // bpm, 0, 0)),
                  pl.BlockSpec((1, 1, D_MODEL), lambda i, j: (i // bpm, 0, 0)),
                  pl.BlockSpec((1, D_MODEL), lambda i, j: (0, 0)),
                  pl.BlockSpec((D_MODEL, tn), lambda i, j: (0, j)),
                  pl.BlockSpec((D_MODEL, 2 * LANE), lambda i, j: (0, 0))],
        out_specs=[pl.BlockSpec((tm, tn), lambda i, j: (i, j)),
                   pl.BlockSpec((tm, LANE), lambda i, j: (i, 0))],
        out_shape=[jax.ShapeDtypeStruct((m, n_cols), BF16),
                   jax.ShapeDtypeStruct((m, LANE), F32)],
        scratch_shapes=[pltpu.VMEM((tm, D_MODEL), BF16)],
        compiler_params=pltpu.CompilerParams(dimension_semantics=("arbitrary", "arbitrary"),
                                             vmem_limit_bytes=VMEM_LIMIT),
        name="inproj",
    )(x2d, shift, scale, gain, w_main, w_dt)


def _dtprep_kernel(raw_ref, bias_ref, alog_ref, row_ref, *, n_chunks):
    a = -jnp.exp(alog_ref[...])
    bias = bias_ref[...]
    ii = lax.broadcasted_iota(jnp.int32, (CHUNK, CHUNK), 0)
    jj = lax.broadcasted_iota(jnp.int32, (CHUNK, CHUNK), 1)
    tril = (jj <= ii).astype(F32)
    triu = (jj >= ii).astype(F32)
    lane = lax.broadcasted_iota(jnp.int32, (1, LANE), 1)
    is_fwd = (lane % N_DH) < HEADS_PER_GROUP
    for c in range(n_chunks):
        sl = slice(c * CHUNK, (c + 1) * CHUNK)
        dt = jax.nn.softplus(raw_ref[0, sl, :] + bias)
        dta = dt * a
        cf = jnp.dot(tril, dta, precision=lax.Precision.HIGHEST, preferred_element_type=F32)
        cb = jnp.dot(triu, dta, precision=lax.Precision.HIGHEST, preferred_element_type=F32)
        cum = jnp.where(is_fwd, cf, cb)
        tot = jnp.where(is_fwd, cf[CHUNK - 1:CHUNK, :], cb[0:1, :])
        wx = dt * jnp.exp(tot - cum)
        ey = jnp.exp(cum)
        ncx = jnp.log(jnp.maximum(dt, DT_FLOOR)) - cum
        diag = jnp.log(dt + pltpu.roll(dt, LANE - HEADS_PER_GROUP, 1))
        for r_off, val in ((R_CUM, cum), (R_NCX, ncx), (R_EY, ey), (R_WX, wx), (R_DIAG, diag)):
            vt = val.T
            for g in range(N_GROUPS):
                r0 = g * ROWS_PER_GROUP + r_off
                row_ref[0, r0:r0 + N_DH, sl] = vt[g * N_DH:(g + 1) * N_DH, :]


def _dtprep(raw, bias, alog, tl):
    b, l, _ = raw.shape
    return pl.pallas_call(
        functools.partial(_dtprep_kernel, n_chunks=tl // CHUNK),
        grid=(b, l // tl),
        in_specs=[pl.BlockSpec((1, tl, LANE), lambda i, j: (i, j, 0)),
                  pl.BlockSpec((1, LANE), lambda i, j: (0, 0)),
                  pl.BlockSpec((1, LANE), lambda i, j: (0, 0))],
        out_specs=pl.BlockSpec((1, N_GROUPS * ROWS_PER_GROUP, tl), lambda i, j: (i, 0, j)),
        out_shape=jax.ShapeDtypeStruct((b, N_GROUPS * ROWS_PER_GROUP, l), F32),
        compiler_params=pltpu.CompilerParams(dimension_semantics=("arbitrary", "arbitrary"),
                                             vmem_limit_bytes=VMEM_LIMIT),
        name="dtprep",
    )(raw, bias, alog)


def _split_terms(x, n):
    terms = []
    for _ in range(n - 1):
        t = x.astype(BF16).astype(F32)
        terms.append(t)
        x = x - t
    terms.append(x)
    return terms


def _decay_args(rowt, onesel):
    ones = jnp.ones((N_DH, CHUNK), F32)
    at = jnp.concatenate(_split_terms(rowt[R_CUM:R_CUM + N_DH], 3) + [ones] * 3, axis=0).astype(BF16)
    wide = [jnp.concatenate([t] * N_DH, axis=1) * onesel for t in _split_terms(rowt[R_NCX:R_NCX + N_DH], 3)]
    bmat = jnp.concatenate([onesel] * 3 + wide, axis=0).astype(BF16)
    return _dot_tn(at, bmat)


def _expand(rowt, sel):
    rt = jnp.concatenate(_split_terms(rowt[R_EY:R_EY + N_DH], 2) + _split_terms(rowt[R_WX:R_WX + N_DH], 2),
                         axis=0).astype(BF16)
    ew = _dot_tn(rt, sel)
    return ew[:, :GROUP_W], ew[:, GROUP_W:]


def _conv_silu(ext_ref, s, r0, cw, cbias):
    sl = slice(s * LANE, (s + 1) * LANE)
    acc = cbias[:, sl]
    for k in range(D_CONV):
        acc = acc + cw[k:k + 1, sl] * ext_ref[s, pl.ds(r0 + k, CHUNK), :]
    return acc * jax.nn.sigmoid(acc)


def _ssd_kernel(xbc_ref, prev_ref, next_ref, z_ref, row_ref, cxbc_ref, crow_ref,
                cw_ref, cbias_ref, dskip_ref, gain_ref, onesel_ref, sel_ref, out_ref,
                cache_ref, yacc_ref, sf_ref, sb_ref, ext_ref, *, tl, n_blk, ctx_len):
    p = pl.program_id(2)
    i = pl.program_id(3)
    n_ch = tl // CHUNK
    cw = cw_ref[...]
    cbias = cbias_ref[...]
    tap0 = HALO - D_CONV // 2

    def fill_ext(src, first, last, n_rows):
        for s in range(N_SLAB):
            sl = slice(s * LANE, (s + 1) * LANE)
            ext_ref[s, 0:HALO, :] = first[:, sl]
            ext_ref[s, HALO:HALO + n_rows, :] = src[:, sl].astype(F32)
            ext_ref[s, HALO + n_rows:2 * HALO + n_rows, :] = last[:, sl]

    @pl.when(jnp.logical_and(p == 0, i == 0))
    def _init_from_context():
        zpad = jnp.zeros((HALO, XBC_GW), F32)
        fill_ext(cxbc_ref[0], zpad, zpad, ctx_len)
        n_cc = ctx_len // CHUNK
        sf = jnp.zeros((D_STATE, GROUP_W), F32)
        st_b, ey_b = [], []
        for cc in range(n_cc):
            slabs = [_conv_silu(ext_ref, s, cc * CHUNK + tap0, cw, cbias) for s in range(N_SLAB - 1)]
            xs = jnp.concatenate(slabs[:2], axis=1)
            bm = slabs[2].astype(BF16)
            rowt = crow_ref[0, :, cc * CHUNK:(cc + 1) * CHUNK]
            ef, wxf = _expand(rowt, sel_ref[0])
            eb, wxb = _expand(rowt, sel_ref[1])
            sf = sf * ef[CHUNK - 1:CHUNK, :] + _dot_tn(bm, (xs * wxf).astype(BF16))
            st_b.append(_dot_tn(bm, (xs * wxb).astype(BF16)))
            ey_b.append(eb[0:1, :])
        sb = jnp.zeros((D_STATE, GROUP_W), F32)
        for cc in reversed(range(n_cc)):
            sb = sb * ey_b[cc] + st_b[cc]
        sf_ref[...] = sf
        sb_ref[...] = sb

    @pl.when(p == 0)
    def _forward():
        blk = i
        base = blk * tl
        fill_ext(xbc_ref[0],
                 jnp.where(blk > 0, prev_ref[0].astype(F32), 0.0),
                 jnp.where(blk < n_blk - 1, next_ref[0].astype(F32), 0.0), tl)
        ii = lax.broadcasted_iota(jnp.int32, (CHUNK, CHUNK), 0)
        jj = lax.broadcasted_iota(jnp.int32, (CHUNK, CHUNK), 1)
        lower = jj < ii
        upper = jj > ii

        def chunk(c, carry):
            r = pl.multiple_of(c * CHUNK, CHUNK)
            slabs = [_conv_silu(ext_ref, s, r + tap0, cw, cbias) for s in range(N_SLAB)]
            slabs_b = [v.astype(BF16) for v in slabs]
            for s in range(N_SLAB):
                cache_ref[pl.ds(base + r, CHUNK), s * LANE:(s + 1) * LANE] = slabs_b[s]
            xs = jnp.concatenate(slabs[:2], axis=1)
            xs_b = jnp.concatenate(slabs_b[:2], axis=1)
            bm, cm = slabs_b[2], slabs_b[3]
            rowt = row_ref[0, :, pl.ds(r, CHUNK)]
            args = _decay_args(rowt, onesel_ref[...])
            ef, wxf = _expand(rowt, sel_ref[0])
            cbm = lax.dot_general(cm, bm, (((1,), (1,)), ((), ())), preferred_element_type=F32)
            yd = []
            for j in range(HEADS_PER_GROUP):
                argf = args[:, j * CHUNK:(j + 1) * CHUNK]
                argb = args[:, (HEADS_PER_GROUP + j) * CHUNK:(HEADS_PER_GROUP + j + 1) * CHUNK]
                diag = rowt[R_DIAG + j:R_DIAG + j + 1, :]
                arg = jnp.where(lower, argf, jnp.where(upper, argb, diag))
                wmat = (cbm * jnp.exp(arg)).astype(BF16)
                yd.append(_dot(wmat, xs_b[:, j * HEAD_DIM:(j + 1) * HEAD_DIM]))
            sf = sf_ref[...]
            y = jnp.concatenate(yd, axis=1) + _dot(cm, sf.astype(BF16)) * ef + xs * dskip_ref[...]
            yacc_ref[pl.ds(base + r, CHUNK), :] = y
            sf_ref[...] = sf * ef[CHUNK - 1:CHUNK, :] + _dot_tn(bm, (xs * wxf).astype(BF16))
            return carry

        lax.fori_loop(0, n_ch, chunk, 0, unroll=2)

    @pl.when(p == 1)
    def _backward():
        blk = n_blk - 1 - i
        base = blk * tl

        def chunk(t, carry):
            c = n_ch - 1 - t
            r = pl.multiple_of(c * CHUNK, CHUNK)
            xbc_b = cache_ref[pl.ds(base + r, CHUNK), :]
            xs = xbc_b[:, :GROUP_W].astype(F32)
            bm = xbc_b[:, GROUP_W:GROUP_W + D_STATE]
            cm = xbc_b[:, GROUP_W + D_STATE:]
            eb, wxb = _expand(row_ref[0, :, pl.ds(r, CHUNK)], sel_ref[1])
            sb = sb_ref[...]
            y = yacc_ref[pl.ds(base + r, CHUNK), :] + _dot(cm, sb.astype(BF16)) * eb
            sb_ref[...] = sb * eb[0:1, :] + _dot_tn(bm, (xs * wxb).astype(BF16))
            z = z_ref[0, pl.ds(r, CHUNK), :].astype(F32)
            y = y * (z * jax.nn.sigmoid(z))
            out_ref[0, pl.ds(r, CHUNK), :] = _rmsnorm(y, gain_ref[...]).astype(BF16)
            return carry

        lax.fori_loop(0, n_ch, chunk, 0, unroll=2)


def _ssd_selectors():
    onesel = (np.arange(N_DH * CHUNK)[None, :] // CHUNK == np.arange(N_DH)[:, None]).astype(np.float32)
    sel = np.zeros((2, 4 * N_DH, 2 * GROUP_W), np.float32)
    for d in range(2):
        for j in range(HEADS_PER_GROUP):
            dh = d * HEADS_PER_GROUP + j
            for term in range(2):
                sel[d, term * N_DH + dh, j * HEAD_DIM:(j + 1) * HEAD_DIM] = 1.0
                sel[d, (2 + term) * N_DH + dh, GROUP_W + j * HEAD_DIM:GROUP_W + (j + 1) * HEAD_DIM] = 1.0
    return jnp.asarray(onesel, F32), jnp.asarray(sel, BF16)


def _ssd(u, rows, cu, crows, conv_w, conv_b, dskip, gain, tl):
    b, l, _ = u.shape
    ctx_len = cu.shape[1]
    assert ctx_len <= tl
    n_blk = l // tl
    hb = tl // HALO
    last_halo = l // HALO - 1
    onesel, sel = _ssd_selectors()

    def in_blk(p, i):
        return i * (1 - p) + (n_blk - 1) * p

    def seq_blk(p, i):
        return i * (1 - p) + (n_blk - 1 - i) * p

    def out_blk(p, i):
        return (n_blk - 1) * (1 - p) + (n_blk - 1 - i) * p

    grid = (b, N_GROUPS, 2, n_blk)
    in_specs = [
        pl.BlockSpec((1, tl, XBC_GW), lambda bb, g, p, i: (bb, in_blk(p, i), g)),
        pl.BlockSpec((1, HALO, XBC_GW), lambda bb, g, p, i: (bb, jnp.maximum(in_blk(p, i) * hb - 1, 0), g)),
        pl.BlockSpec((1, HALO, XBC_GW),
                     lambda bb, g, p, i: (bb, jnp.minimum((in_blk(p, i) + 1) * hb, last_halo), g)),
        pl.BlockSpec((1, tl, GROUP_W), lambda bb, g, p, i: (bb, out_blk(p, i), U_Z // GROUP_W + g)),
        pl.BlockSpec((1, ROWS_PER_GROUP, tl), lambda bb, g, p, i: (bb, g, seq_blk(p, i))),
        pl.BlockSpec((1, ctx_len, XBC_GW), lambda bb, g, p, i: (bb, 0, g)),
        pl.BlockSpec((1, ROWS_PER_GROUP, ctx_len), lambda bb, g, p, i: (bb, g, 0)),
        pl.BlockSpec((SUBLANE, XBC_GW), lambda bb, g, p, i: (0, g)),
        pl.BlockSpec((1, XBC_GW), lambda bb, g, p, i: (0, g)),
        pl.BlockSpec((1, GROUP_W), lambda bb, g, p, i: (0, g)),
        pl.BlockSpec((1, GROUP_W), lambda bb, g, p, i: (0, g)),
        pl.BlockSpec(onesel.shape, lambda bb, g, p, i: (0, 0)),
        pl.BlockSpec(sel.shape, lambda bb, g, p, i: (0, 0, 0)),
    ]
    return pl.pallas_call(
        functools.partial(_ssd_kernel, tl=tl, n_blk=n_blk, ctx_len=ctx_len),
        grid=grid,
        in_specs=in_specs,
        out_specs=pl.BlockSpec((1, tl, GROUP_W), lambda bb, g, p, i: (bb, out_blk(p, i), g)),
        out_shape=jax.ShapeDtypeStruct((b, l, D_INNER), BF16),
        scratch_shapes=[pltpu.VMEM((l, XBC_GW), BF16),
                        pltpu.VMEM((l, GROUP_W), F32),
                        pltpu.VMEM((D_STATE, GROUP_W), F32),
                        pltpu.VMEM((D_STATE, GROUP_W), F32),
                        pltpu.VMEM((N_SLAB, tl + 2 * HALO, LANE), F32)],
        compiler_params=pltpu.CompilerParams(
            dimension_semantics=("arbitrary", "arbitrary", "arbitrary", "arbitrary"),
            vmem_limit_bytes=VMEM_LIMIT),
        name="ssd",
    )(u, u, u, u, rows, cu, crows, conv_w, conv_b, dskip, gain, onesel, sel)


def _dft_mats():
    def cs(n):
        k = np.arange(n)
        ang = 2.0 * np.pi * np.outer(k, k) / n
        return np.cos(ang), np.sin(ang)

    c3, s3 = cs(FFT_GROUP_DIM)
    w1 = np.concatenate([c3, -s3], axis=1)
    c2, s2 = cs(GRID_W)
    m2 = np.block([[c2, s2], [-s2, c2]])
    return w1, m2


def _fft12_kernel(f_ref, w1_ref, m2_ref, o_ref, *, tm):
    f = f_ref[0]
    w1 = w1_ref[...]
    ps, qs = [], []
    for g in range(FFT_GROUPS):
        pq = _dot(f[:, g * FFT_GROUP_DIM:(g + 1) * FFT_GROUP_DIM], w1)
        ps.append(pq[:, :FFT_GROUP_DIM])
        qs.append(pq[:, FFT_GROUP_DIM:])
    pr = jnp.concatenate(ps, axis=1).astype(BF16)
    qr = jnp.concatenate(qs, axis=1).astype(BF16)
    m2 = m2_ref[...]
    for r in range(tm // GRID_W):
        sl = slice(r * GRID_W, (r + 1) * GRID_W)
        o = _dot(m2, jnp.concatenate([pr[sl], qr[sl]], axis=0))
        o_ref[0, 0, sl, :] = o[:GRID_W].astype(BF16)
        o_ref[0, 1, sl, :] = o[GRID_W:].astype(BF16)


def _fft3_kernel(ri_ref, m3_ref, o_ref):
    o_ref[0] = _dot(m3_ref[...], ri_ref[0]).astype(BF16)


def _fourier(u, tm):
    b, l, _ = u.shape
    rows = l // GRID_W
    w1, m2 = _dft_mats()
    ri = pl.pallas_call(
        functools.partial(_fft12_kernel, tm=tm),
        grid=(b, l // tm),
        in_specs=[pl.BlockSpec((1, tm, D_MODEL), lambda i, j: (i, j, U_FFT // D_MODEL)),
                  pl.BlockSpec((FFT_GROUP_DIM, 2 * FFT_GROUP_DIM), lambda i, j: (0, 0)),
                  pl.BlockSpec((2 * GRID_W, 2 * GRID_W), lambda i, j: (0, 0))],
        out_specs=pl.BlockSpec((1, 2, tm, D_MODEL), lambda i, j: (i, 0, j, 0)),
        out_shape=jax.ShapeDtypeStruct((b, 2, l, D_MODEL), BF16),
        compiler_params=pltpu.CompilerParams(dimension_semantics=("arbitrary", "arbitrary"),
                                             vmem_limit_bytes=VMEM_LIMIT),
        name="fft12",
    )(u, jnp.asarray(w1, BF16), jnp.asarray(m2, BF16))

    k = np.arange(rows)
    ang = 2.0 * np.pi * np.outer(k, k) / rows
    scale = 1.0 / np.sqrt(float(rows * GRID_W * FFT_GROUP_DIM))
    m3 = np.concatenate([np.cos(ang), np.sin(ang)], axis=1) * scale
    wide = GRID_W * D_MODEL
    tn = 4096
    out = pl.pallas_call(
        _fft3_kernel,
        grid=(b, wide // tn),
        in_specs=[pl.BlockSpec((1, 2 * rows, tn), lambda i, j: (i, 0, j)),
                  pl.BlockSpec((rows, 2 * rows), lambda i, j: (0, 0))],
        out_specs=pl.BlockSpec((1, rows, tn), lambda i, j: (i, 0, j)),
        out_shape=jax.ShapeDtypeStruct((b, rows, wide), BF16),
        compiler_params=pltpu.CompilerParams(dimension_semantics=("arbitrary", "arbitrary"),
                                             vmem_limit_bytes=VMEM_LIMIT),
        name="fft3",
    )(ri.reshape(b, 2 * rows, wide), jnp.asarray(m3, BF16))
    return out.reshape(b, l, D_MODEL)


FF_SPLIT = 2


def _tail_kernel(y_ref, f_ref, ga_ref, gb_ref, x_ref, g1_ref, sh2_ref, sc2_ref, g2_ref,
                 wso_ref, wfo_ref, wo_ref, n2_ref, wfi_ref, wfu_ref, wfd_ref, fg_ref, o_ref):
    a = _dot(y_ref[...], wso_ref[...])
    bq = _dot(f_ref[...], wfo_ref[...])
    merged = jax.nn.sigmoid(ga_ref[...].astype(F32)) * a + jax.nn.sigmoid(gb_ref[...].astype(F32)) * bq
    xn = x_ref[...] + g1_ref[0] * _dot(merged.astype(BF16), wo_ref[...])
    h = (_rmsnorm(xn, n2_ref[...]) * (1.0 + sc2_ref[0]) + sh2_ref[0]).astype(BF16)
    ffw = D_FF // FF_SPLIT
    acc = None
    for s in range(FF_SPLIT):
        sl = slice(s * ffw, (s + 1) * ffw)
        gate = _dot(h, wfi_ref[:, sl])
        up = _dot(h, wfu_ref[:, sl])
        act = (gate * jax.nn.sigmoid(gate) * up).astype(BF16)
        part = _dot(act, wfd_ref[sl, :])
        acc = part if acc is None else acc + part
    o_ref[...] = _rmsnorm(xn + g2_ref[0] * acc, fg_ref[...])


def _tail(y2d, f2d, u2d, x2d, g1, sh2, sc2, g2, wso, wfo, wo, n2, wfi, wfu, wfd, fg, tm, rows_per_mod):
    m = x2d.shape[0]
    bpm = rows_per_mod // tm

    def const(shape):
        return pl.BlockSpec(shape, lambda i: (0, 0), pipeline_mode=pl.Buffered(1))

    def mod():
        return pl.BlockSpec((1, 1, D_MODEL), lambda i: (i // bpm, 0, 0))

    return pl.pallas_call(
        _tail_kernel,
        grid=(m // tm,),
        in_specs=[pl.BlockSpec((tm, D_INNER), lambda i: (i, 0)),
                  pl.BlockSpec((tm, D_MODEL), lambda i: (i, 0)),
                  pl.BlockSpec((tm, D_MODEL), lambda i: (i, U_GATE // D_MODEL)),
                  pl.BlockSpec((tm, D_MODEL), lambda i: (i, U_GATE // D_MODEL + 1)),
                  pl.BlockSpec((tm, D_MODEL), lambda i: (i, 0)),
                  mod(), mod(), mod(), mod(),
                  const((D_INNER, D_MODEL)), const((D_MODEL, D_MODEL)), const((D_MODEL, D_MODEL)),
                  const((1, D_MODEL)),
                  const((D_MODEL, D_FF)), const((D_MODEL, D_FF)), const((D_FF, D_MODEL)),
                  const((1, D_MODEL))],
        out_specs=pl.BlockSpec((tm, D_MODEL), lambda i: (i, 0)),
        out_shape=jax.ShapeDtypeStruct((m, D_MODEL), F32),
        compiler_params=pltpu.CompilerParams(dimension_semantics=("arbitrary",),
                                             vmem_limit_bytes=VMEM_LIMIT),
        name="tail",
    )(y2d, f2d, u2d, u2d, x2d, g1, sh2, sc2, g2, wso, wfo, wo, n2, wfi, wfu, wfd, fg)


def _group_major(a):
    parts = []
    for g in range(N_GROUPS):
        parts.append(a[..., g * GROUP_W:(g + 1) * GROUP_W])
        parts.append(a[..., D_INNER + g * D_STATE:D_INNER + (g + 1) * D_STATE])
        parts.append(a[..., D_INNER + GS + g * D_STATE:D_INNER + GS + (g + 1) * D_STATE])
    return parts


def _dt_lanes(a):
    lead = a.shape[:-1]
    a = a.reshape(lead + (2, N_GROUPS, HEADS_PER_GROUP))
    a = jnp.swapaxes(a, -3, -2).reshape(lead + (2 * N_HEADS,))
    return jnp.concatenate([a, jnp.zeros(lead + (LANE - 2 * N_HEADS,), a.dtype)], axis=-1)


def _block(x, c, ctx, c_ctx, w_ada, b_ada, norm1_g, w_in, conv_w, conv_b, dt_bias, a_log, d_skip,
           ssd_norm_g, w_ssd_out, w_fft_out, w_o, norm2_g, w_ffn_in, w_ffn_out, final_g,
           proj_tm, prep_tl, ssd_tl, fft_tm, tail_tm):
    b, l, d = x.shape
    ctx_len = ctx.shape[1]
    assert d == D_MODEL and l % (GRID_W * 2) == 0 and ctx_len % CHUNK == 0 and b + 1 <= SUBLANE

    w_in0 = w_in[0]
    w_main = jnp.concatenate(_group_major(w_in0[:, :CONV_DIM]) + [w_in0[:, REF_Z:]], axis=1).astype(BF16)
    w_dt = _dt_lanes(w_in0[:, REF_DT:REF_Z])
    w_dt_hi = w_dt.astype(BF16)
    w_dt_lo = (w_dt - w_dt_hi.astype(F32)).astype(BF16)
    w_dt2 = jnp.concatenate([w_dt_hi, w_dt_lo], axis=1)
    conv_w_p = jnp.concatenate(_group_major(conv_w[0]) + [], axis=1)
    conv_w_p = jnp.concatenate([conv_w_p, jnp.zeros((SUBLANE - D_CONV, CONV_DIM), F32)], axis=0)
    conv_b_p = jnp.concatenate(_group_major(conv_b[0]), axis=0).reshape(1, CONV_DIM)
    bias_v = _dt_lanes(dt_bias[0].reshape(-1)).reshape(1, LANE)
    alog_v = _dt_lanes(a_log[0].reshape(-1)).reshape(1, LANE)
    dskip_v = jnp.repeat(d_skip[0], HEAD_DIM).reshape(1, D_INNER)
    gain_v = ssd_norm_g[0].reshape(1, D_INNER)
    wso = w_ssd_out[0].astype(BF16)
    wfo = w_fft_out[0].astype(BF16)
    wo = w_o[0].astype(BF16)
    wfi = w_ffn_in[0][:, :D_FF].astype(BF16)
    wfu = w_ffn_in[0][:, D_FF:].astype(BF16)
    wfd = w_ffn_out[0].astype(BF16)

    cvecs = jnp.concatenate([c, c_ctx[None, :], jnp.zeros((SUBLANE - b - 1, D_MODEL), F32)], axis=0)
    mods = _mods(cvecs, b + 1, w_ada[0], b_ada[0])
    lat = mods[:b].reshape(b, 6, 1, D_MODEL)
    sh1, sc1, g1, sh2, sc2, g2 = (lat[:, k] for k in range(6))
    cm = mods[b].reshape(6, 1, 1, D_MODEL)

    x2d = x.reshape(b * l, D_MODEL)
    n1 = norm1_g[0].reshape(1, D_MODEL)
    u2d, dt_raw = _inproj(x2d, sh1, sc1, n1, w_main, w_dt2, U_WIDTH, proj_tm, l)
    u = u2d.reshape(b, l, U_WIDTH)
    cu2d, cdt_raw = _inproj(ctx.reshape(b * ctx_len, D_MODEL), cm[0], cm[1], n1, w_main, w_dt2,
                            CONV_DIM, b * ctx_len, b * ctx_len)
    cu = cu2d.reshape(b, ctx_len, CONV_DIM)

    rows = _dtprep(dt_raw.reshape(b, l, LANE), bias_v, alog_v, prep_tl)
    crows = _dtprep(cdt_raw.reshape(b, ctx_len, LANE), bias_v, alog_v, ctx_len)
    y_ssd = _ssd(u, rows, cu, crows, conv_w_p, conv_b_p, dskip_v, gain_v, ssd_tl)

    f_mix = _fourier(u, fft_tm)

    out = _tail(y_ssd.reshape(b * l, D_INNER), f_mix.reshape(b * l, D_MODEL), u2d, x2d,
                g1, sh2, sc2, g2, wso, wfo, wo, norm2_g[0].reshape(1, D_MODEL), wfi, wfu, wfd,
                final_g.reshape(1, D_MODEL), tail_tm, l)
    return out.reshape(b, l, D_MODEL)


def kernel(x, c, ctx, c_ctx, w_ada, b_ada, norm1_g, w_in, conv_w, conv_b, dt_bias, a_log, d_skip,
           ssd_norm_g, w_ssd_out, w_fft_out, w_o, norm2_g, w_ffn_in, w_ffn_out, final_g):
    return _block(x, c, ctx, c_ctx, w_ada, b_ada, norm1_g, w_in, conv_w, conv_b, dt_bias, a_log, d_skip,
                  ssd_norm_g, w_ssd_out, w_fft_out, w_o, norm2_g, w_ffn_in, w_ffn_out, final_g,
                  proj_tm=1024, prep_tl=1024, ssd_tl=2048, fft_tm=512, tail_tm=512)
```

```python
import functools

import jax
import jax.numpy as jnp
import numpy as np
from jax import lax
from jax.experimental import pallas as pl
from jax.experimental.pallas import tpu as pltpu

F32 = jnp.float32
BF16 = jnp.bfloat16

D_MODEL = 1024
EPS = 1e-6
GRID_W = 64

N_GROUPS = 8
HEADS_PER_GROUP = 4
HEAD_DIM = 64
N_HEADS = N_GROUPS * HEADS_PER_GROUP
D_STATE = 128
D_INNER = N_HEADS * HEAD_DIM
GROUP_W = HEADS_PER_GROUP * HEAD_DIM
GS = N_GROUPS * D_STATE
CONV_DIM = D_INNER + 2 * GS
D_CONV = 5
CHUNK = 128

FFT_GROUPS = 8
FFT_GROUP_DIM = 128
D_FF = 2816

REF_DT = CONV_DIM
REF_Z = REF_DT + 2 * N_HEADS

U_B = D_INNER
U_C = U_B + GS
U_Z = CONV_DIM
U_FFT = U_Z + D_INNER
U_GATE = U_FFT + D_MODEL
U_WIDTH = U_GATE + 2 * D_MODEL

LANE = 128
SUBLANE = 8
HALO = 16
VMEM_LIMIT = 56 * 1024 * 1024

N_DH = 2 * HEADS_PER_GROUP
R_CUM, R_NCX, R_EY, R_WX, R_DIAG = 0, 8, 16, 24, 32
ROWS_PER_GROUP = 40
N_SLAB = (GROUP_W + 2 * D_STATE) // LANE
DT_FLOOR = 1e-37
N_CUM_TERMS = 3
K_ARGS = 2 * N_CUM_TERMS * N_DH
FFT_PITCH = GRID_W + 4


def _dot(a, b):
    return jnp.dot(a, b, preferred_element_type=F32)


def _dot_tn(a, b):
    return lax.dot_general(a, b, (((0,), (0,)), ((), ())), preferred_element_type=F32)


def _rmsnorm(x, g):
    return x * lax.rsqrt(jnp.mean(x * x, axis=-1, keepdims=True) + EPS) * g


def _mods_kernel(ct_ref, w_ref, b_ref, o_ref, *, n_vecs):
    s = ct_ref[...]
    s = s * jax.nn.sigmoid(s)
    w = w_ref[...]
    rows = [jnp.sum(w * s[:, r:r + 1], axis=0, keepdims=True) for r in range(n_vecs)]
    rows.append(jnp.zeros((SUBLANE - n_vecs, w.shape[1]), F32))
    o_ref[...] = jnp.concatenate(rows, axis=0) + b_ref[...]


def _mods(cvecs, n_vecs, w_ada, b_ada):
    n = w_ada.shape[1]
    tn = 1536
    return pl.pallas_call(
        functools.partial(_mods_kernel, n_vecs=n_vecs),
        grid=(n // tn,),
        in_specs=[pl.BlockSpec((D_MODEL, SUBLANE), lambda j: (0, 0)),
                  pl.BlockSpec((D_MODEL, tn), lambda j: (0, j)),
                  pl.BlockSpec((1, tn), lambda j: (0, j))],
        out_specs=pl.BlockSpec((SUBLANE, tn), lambda j: (0, j)),
        out_shape=jax.ShapeDtypeStruct((SUBLANE, n), F32),
        compiler_params=pltpu.CompilerParams(dimension_semantics=("arbitrary",),
                                             vmem_limit_bytes=VMEM_LIMIT),
        name="mods",
    )(cvecs.T, w_ada, b_ada.reshape(1, n))


def _inproj_kernel(x_ref, sh_ref, sc_ref, g_ref, wa_ref, wb_ref, wdt_ref, u_ref, dt_ref, h_ref, *, n_a):
    j = pl.program_id(1)

    @pl.when(j == 0)
    def _():
        h = _rmsnorm(x_ref[...], g_ref[...]) * (1.0 + sc_ref[0]) + sh_ref[0]
        hb = h.astype(BF16)
        h_ref[...] = hb
        hl = (h - hb.astype(F32)).astype(BF16)
        a = _dot(hb, wdt_ref[...])
        dt_ref[...] = a[:, :LANE] + a[:, LANE:] + _dot(hl, wdt_ref[:, :LANE])

    @pl.when(j < n_a)
    def _():
        u_ref[...] = _dot(h_ref[...], wa_ref[...].astype(BF16)).astype(BF16)

    @pl.when(j >= n_a)
    def _():
        u_ref[...] = _dot(h_ref[...], wb_ref[...]).astype(BF16)


def _inproj(x2d, shift, scale, gain, w_f32, w_rest, w_dt, n_cols, tm, rows_per_mod):
    m = x2d.shape[0]
    tn = 1024
    n_a = CONV_DIM // tn
    bpm = rows_per_mod // tm
    return pl.pallas_call(
        functools.partial(_inproj_kernel, n_a=n_a),
        grid=(m // tm, n_cols // tn),
        in_specs=[pl.BlockSpec((tm, D_MODEL), lambda i, j: (i, 0)),
                  pl.BlockSpec((1, 1, D_MODEL), lambda i, j: (i // bpm, 0, 0)),
                  pl.BlockSpec((1, 1, D_MODEL), lambda i, j: (i // bpm, 0, 0)),
                  pl.BlockSpec((1, D_MODEL), lambda i, j: (0, 0)),
                  pl.BlockSpec((D_MODEL, tn), lambda i, j: (0, jnp.minimum(j, n_a - 1))),
                  pl.BlockSpec((D_MODEL, tn), lambda i, j: (0, jnp.maximum(j - n_a, 0))),
                  pl.BlockSpec((D_MODEL, 2 * LANE), lambda i, j: (0, 0))],
        out_specs=[pl.BlockSpec((tm, tn), lambda i, j: (i, j)),
                   pl.BlockSpec((tm, LANE), lambda i, j: (i, 0))],
        out_shape=[jax.ShapeDtypeStruct((m, n_cols), BF16),
                   jax.ShapeDtypeStruct((m, LANE), F32)],
        scratch_shapes=[pltpu.VMEM((tm, D_MODEL), BF16)],
        compiler_params=pltpu.CompilerParams(dimension_semantics=("arbitrary", "arbitrary"),
                                             vmem_limit_bytes=VMEM_LIMIT),
        name="inproj",
    )(x2d, shift, scale, gain, w_f32, w_rest, w_dt)


def _dtprep_kernel(raw_ref, bias_ref, alog_ref, row_ref, *, n_chunks):
    a = -jnp.exp(alog_ref[...])
    bias = bias_ref[...]
    ii = lax.broadcasted_iota(jnp.int32, (CHUNK, CHUNK), 0)
    jj = lax.broadcasted_iota(jnp.int32, (CHUNK, CHUNK), 1)
    tril = (jj <= ii).astype(F32)
    triu = (jj >= ii).astype(F32)
    lane = lax.broadcasted_iota(jnp.int32, (1, LANE), 1)
    is_fwd = (lane % N_DH) < HEADS_PER_GROUP
    for c in range(n_chunks):
        sl = slice(c * CHUNK, (c + 1) * CHUNK)
        dt = jax.nn.softplus(raw_ref[0, sl, :] + bias)
        dta = dt * a
        cf = jnp.dot(tril, dta, precision=lax.Precision.HIGHEST, preferred_element_type=F32)
        cb = jnp.dot(triu, dta, precision=lax.Precision.HIGHEST, preferred_element_type=F32)
        cum = jnp.where(is_fwd, cf, cb)
        tot = jnp.where(is_fwd, cf[CHUNK - 1:CHUNK, :], cb[0:1, :])
        wx = dt * jnp.exp(tot - cum)
        ey = jnp.exp(cum)
        ncx = jnp.log(jnp.maximum(dt, DT_FLOOR)) - cum
        diag = jnp.log(dt + pltpu.roll(dt, LANE - HEADS_PER_GROUP, 1))
        for r_off, val in ((R_CUM, cum), (R_NCX, ncx), (R_EY, ey), (R_WX, wx), (R_DIAG, diag)):
            vt = val.T
            for g in range(N_GROUPS):
                r0 = g * ROWS_PER_GROUP + r_off
                row_ref[0, r0:r0 + N_DH, sl] = vt[g * N_DH:(g + 1) * N_DH, :]


def _dtprep(raw, bias, alog, tl):
    b, l, _ = raw.shape
    return pl.pallas_call(
        functools.partial(_dtprep_kernel, n_chunks=tl // CHUNK),
        grid=(b, l // tl),
        in_specs=[pl.BlockSpec((1, tl, LANE), lambda i, j: (i, j, 0)),
                  pl.BlockSpec((1, LANE), lambda i, j: (0, 0)),
                  pl.BlockSpec((1, LANE), lambda i, j: (0, 0))],
        out_specs=pl.BlockSpec((1, N_GROUPS * ROWS_PER_GROUP, tl), lambda i, j: (i, 0, j)),
        out_shape=jax.ShapeDtypeStruct((b, N_GROUPS * ROWS_PER_GROUP, l), F32),
        compiler_params=pltpu.CompilerParams(dimension_semantics=("arbitrary", "arbitrary"),
                                             vmem_limit_bytes=VMEM_LIMIT),
        name="dtprep",
    )(raw, bias, alog)


def _split_terms(x, n):
    terms = []
    for _ in range(n - 1):
        t = x.astype(BF16).astype(F32)
        terms.append(t)
        x = x - t
    terms.append(x)
    return terms


def _scale_rows(rows):
    return jnp.concatenate(_split_terms(rows[R_EY:R_EY + N_DH], 2) + _split_terms(rows[R_WX:R_WX + N_DH], 2),
                           axis=0).astype(BF16)


def _conv_silu(ext_ref, s, r0, cw, cbias):
    sl = slice(s * LANE, (s + 1) * LANE)
    acc = cbias[:, sl]
    for k in range(D_CONV):
        acc = acc + cw[k:k + 1, sl] * ext_ref[s, pl.ds(r0 + k, CHUNK), :]
    return acc * jax.nn.sigmoid(acc)


def _ssd_kernel(x_ref, b_ref, c_ref, xp_ref, bp_ref, cp_ref, xn_ref, bn_ref, cn_ref, z_ref, row_ref,
                cx_ref, cb_ref, crow_ref, cwx_ref, cwb_ref, cwc_ref, cbx_ref, cbb_ref, cbc_ref,
                dskip_ref, gain_ref, onesel_ref, sel_ref, pmat_ref, out_ref,
                cache_ref, yacc_ref, sf_ref, sb_ref, ext_ref, cola_ref, ew_ref, *, tl, n_blk, ctx_len):
    p = pl.program_id(2)
    i = pl.program_id(3)
    n_ch = tl // CHUNK
    cw = jnp.concatenate([cwx_ref[...], cwb_ref[...], cwc_ref[...]], axis=1)
    cbias = jnp.concatenate([cbx_ref[...], cbb_ref[...], cbc_ref[...]], axis=1)
    tap0 = HALO - D_CONV // 2

    def slabs_of(x, bm, cm):
        return [x[:, :LANE], x[:, LANE:], bm, cm]

    def fill_ext(src, first, last, n_rows):
        for s in range(len(src)):
            ext_ref[s, 0:HALO, :] = first[s]
            ext_ref[s, HALO:HALO + n_rows, :] = src[s].astype(F32)
            ext_ref[s, HALO + n_rows:2 * HALO + n_rows, :] = last[s]

    def expand_block(rows, d, n):
        ew_ref[0:n, :] = _dot_tn(_scale_rows(rows), sel_ref[d])

    @pl.when(jnp.logical_and(p == 0, i == 0))
    def _init_from_context():
        zpad = [jnp.zeros((HALO, LANE), F32)] * (N_SLAB - 1)
        cx = cx_ref[0]
        fill_ext([cx[:, :LANE], cx[:, LANE:], cb_ref[0]], zpad, zpad, ctx_len)
        n_cc = ctx_len // CHUNK
        crows = crow_ref[0]
        xs, bm = [], []
        for cc in range(n_cc):
            slabs = [_conv_silu(ext_ref, s, cc * CHUNK + tap0, cw, cbias) for s in range(N_SLAB - 1)]
            xs.append(jnp.concatenate(slabs[:2], axis=1))
            bm.append(slabs[2].astype(BF16))
        expand_block(crows, 0, ctx_len)
        sf = jnp.zeros((D_STATE, GROUP_W), F32)
        for cc in range(n_cc):
            rs = slice(cc * CHUNK, (cc + 1) * CHUNK)
            ef_last = ew_ref[(cc + 1) * CHUNK - 1:(cc + 1) * CHUNK, :GROUP_W]
            sf = sf * ef_last + _dot_tn(bm[cc], (xs[cc] * ew_ref[rs, GROUP_W:]).astype(BF16))
        sf_ref[...] = sf
        expand_block(crows, 1, ctx_len)
        sb = jnp.zeros((D_STATE, GROUP_W), F32)
        for cc in reversed(range(n_cc)):
            rs = slice(cc * CHUNK, (cc + 1) * CHUNK)
            eb_first = ew_ref[cc * CHUNK:cc * CHUNK + 1, :GROUP_W]
            sb = sb * eb_first + _dot_tn(bm[cc], (xs[cc] * ew_ref[rs, GROUP_W:]).astype(BF16))
        sb_ref[...] = sb

    @pl.when(p == 0)
    def _forward():
        blk = i
        base = blk * tl
        has_prev = blk > 0
        has_next = blk < n_blk - 1
        fill_ext(slabs_of(x_ref[0], b_ref[0], c_ref[0]),
                 [jnp.where(has_prev, v.astype(F32), 0.0) for v in slabs_of(xp_ref[0], bp_ref[0], cp_ref[0])],
                 [jnp.where(has_next, v.astype(F32), 0.0) for v in slabs_of(xn_ref[0], bn_ref[0], cn_ref[0])],
                 tl)
        rows = row_ref[0]
        expand_block(rows, 0, tl)
        at = jnp.concatenate(_split_terms(rows[R_CUM:R_CUM + N_DH], N_CUM_TERMS) + [jnp.ones((N_DH, tl), F32)],
                             axis=0).astype(BF16)
        cola_ref[...] = _dot_tn(at, pmat_ref[...]).astype(BF16)
        ii = lax.broadcasted_iota(jnp.int32, (CHUNK, CHUNK), 0)
        jj = lax.broadcasted_iota(jnp.int32, (CHUNK, CHUNK), 1)
        lower = jj < ii
        upper = jj > ii

        def chunk(c, carry):
            r = pl.multiple_of(c * CHUNK, CHUNK)
            slabs = [_conv_silu(ext_ref, s, r + tap0, cw, cbias) for s in range(N_SLAB)]
            slabs_b = [v.astype(BF16) for v in slabs]
            for s in range(N_SLAB):
                cache_ref[pl.ds(base + r, CHUNK), s * LANE:(s + 1) * LANE] = slabs_b[s]
            xs = jnp.concatenate(slabs[:2], axis=1)
            xs_b = jnp.concatenate(slabs_b[:2], axis=1)
            bm, cm = slabs_b[2], slabs_b[3]
            onesel = onesel_ref[...]
            ncx = row_ref[0, R_NCX:R_NCX + N_DH, pl.ds(r, CHUNK)]
            wide = [jnp.concatenate([t] * N_DH, axis=1) * onesel for t in _split_terms(ncx, N_CUM_TERMS)]
            bmat = jnp.concatenate([onesel] * N_CUM_TERMS + wide, axis=0).astype(BF16)
            args = _dot(cola_ref[pl.ds(r, CHUNK), :][:, :K_ARGS], bmat)
            cbm = lax.dot_general(cm, bm, (((1,), (1,)), ((), ())), preferred_element_type=F32)
            yd = []
            for j in range(HEADS_PER_GROUP):
                argf = args[:, j * CHUNK:(j + 1) * CHUNK]
                argb = args[:, (HEADS_PER_GROUP + j) * CHUNK:(HEADS_PER_GROUP + j + 1) * CHUNK]
                diag = row_ref[0, R_DIAG + j:R_DIAG + j + 1, pl.ds(r, CHUNK)]
                arg = jnp.where(lower, argf, jnp.where(upper, argb, diag))
                wmat = (cbm * jnp.exp(arg)).astype(BF16)
                yd.append(_dot(wmat, xs_b[:, j * HEAD_DIM:(j + 1) * HEAD_DIM]))
            ef = ew_ref[pl.ds(r, CHUNK), :GROUP_W]
            wxf = ew_ref[pl.ds(r, CHUNK), GROUP_W:]
            sf = sf_ref[...]
            y = jnp.concatenate(yd, axis=1) + _dot(cm, sf.astype(BF16)) * ef + xs * dskip_ref[...]
            yacc_ref[pl.ds(base + r, CHUNK), :] = y
            sf_ref[...] = sf * ef[CHUNK - 1:CHUNK, :] + _dot_tn(bm, (xs * wxf).astype(BF16))
            return carry

        lax.fori_loop(0, n_ch, chunk, 0, unroll=2)

    @pl.when(p == 1)
    def _backward():
        blk = n_blk - 1 - i
        base = blk * tl
        expand_block(row_ref[0], 1, tl)

        def chunk(t, carry):
            c = n_ch - 1 - t
            r = pl.multiple_of(c * CHUNK, CHUNK)
            xbc_b = cache_ref[pl.ds(base + r, CHUNK), :]
            xs = xbc_b[:, :GROUP_W].astype(F32)
            bm = xbc_b[:, GROUP_W:GROUP_W + D_STATE]
            cm = xbc_b[:, GROUP_W + D_STATE:]
            eb = ew_ref[pl.ds(r, CHUNK), :GROUP_W]
            wxb = ew_ref[pl.ds(r, CHUNK), GROUP_W:]
            sb = sb_ref[...]
            y = yacc_ref[pl.ds(base + r, CHUNK), :] + _dot(cm, sb.astype(BF16)) * eb
            sb_ref[...] = sb * eb[0:1, :] + _dot_tn(bm, (xs * wxb).astype(BF16))
            z = z_ref[0, pl.ds(r, CHUNK), :].astype(F32)
            y = y * (z * jax.nn.sigmoid(z))
            out_ref[0, pl.ds(r, CHUNK), :] = _rmsnorm(y, gain_ref[...]).astype(BF16)
            return carry

        lax.fori_loop(0, n_ch, chunk, 0, unroll=2)


def _ssd_selectors():
    onesel = (np.arange(N_DH * CHUNK)[None, :] // CHUNK == np.arange(N_DH)[:, None]).astype(np.float32)
    sel = np.zeros((2, 4 * N_DH, 2 * GROUP_W), np.float32)
    for d in range(2):
        for j in range(HEADS_PER_GROUP):
            dh = d * HEADS_PER_GROUP + j
            for term in range(2):
                sel[d, term * N_DH + dh, j * HEAD_DIM:(j + 1) * HEAD_DIM] = 1.0
                sel[d, (2 + term) * N_DH + dh, GROUP_W + j * HEAD_DIM:GROUP_W + (j + 1) * HEAD_DIM] = 1.0
    pmat = np.zeros(((N_CUM_TERMS + 1) * N_DH, LANE), np.float32)
    for k in range(N_CUM_TERMS * N_DH):
        pmat[k, k] = 1.0
    for t in range(N_CUM_TERMS):
        for dh in range(N_DH):
            pmat[N_CUM_TERMS * N_DH + dh, (N_CUM_TERMS + t) * N_DH + dh] = 1.0
    return jnp.asarray(onesel, F32), jnp.asarray(sel, BF16), jnp.asarray(pmat, BF16)


def _ssd(u, rows, cu, crows, conv_w, conv_b, dskip, gain, tl):
    b, l, _ = u.shape
    ctx_len = cu.shape[1]
    assert ctx_len <= tl
    n_blk = l // tl
    hb = tl // HALO
    last_halo = l // HALO - 1
    onesel, sel, pmat = _ssd_selectors()
    xg, bg, cg = 0, U_B // D_STATE, U_C // D_STATE

    def in_blk(p, i):
        return i * (1 - p) + (n_blk - 1) * p

    def seq_blk(p, i):
        return i * (1 - p) + (n_blk - 1 - i) * p

    def out_blk(p, i):
        return (n_blk - 1) * (1 - p) + (n_blk - 1 - i) * p

    def prev_halo(p, i):
        return jnp.maximum(in_blk(p, i) * hb - 1, 0)

    def next_halo(p, i):
        return jnp.minimum((in_blk(p, i) + 1) * hb, last_halo)

    def xbc_specs(rows_blk, row_index):
        return [pl.BlockSpec((1, rows_blk, GROUP_W), lambda bb, g, p, i: (bb, row_index(p, i), xg + g)),
                pl.BlockSpec((1, rows_blk, D_STATE), lambda bb, g, p, i: (bb, row_index(p, i), bg + g)),
                pl.BlockSpec((1, rows_blk, D_STATE), lambda bb, g, p, i: (bb, row_index(p, i), cg + g))]

    def param_specs(n_rows):
        return [pl.BlockSpec((n_rows, GROUP_W), lambda bb, g, p, i: (0, xg + g)),
                pl.BlockSpec((n_rows, D_STATE), lambda bb, g, p, i: (0, bg + g)),
                pl.BlockSpec((n_rows, D_STATE), lambda bb, g, p, i: (0, cg + g))]

    grid = (b, N_GROUPS, 2, n_blk)
    in_specs = (
        xbc_specs(tl, in_blk) + xbc_specs(HALO, prev_halo) + xbc_specs(HALO, next_halo)
        + [pl.BlockSpec((1, tl, GROUP_W), lambda bb, g, p, i: (bb, out_blk(p, i), U_Z // GROUP_W + g)),
           pl.BlockSpec((1, ROWS_PER_GROUP, tl), lambda bb, g, p, i: (bb, g, seq_blk(p, i)))]
        + xbc_specs(ctx_len, lambda p, i: 0)[:2]
        + [pl.BlockSpec((1, ROWS_PER_GROUP, ctx_len), lambda bb, g, p, i: (bb, g, 0))]
        + param_specs(SUBLANE) + param_specs(1)
        + [pl.BlockSpec((1, GROUP_W), lambda bb, g, p, i: (0, g)),
           pl.BlockSpec((1, GROUP_W), lambda bb, g, p, i: (0, g)),
           pl.BlockSpec(onesel.shape, lambda bb, g, p, i: (0, 0)),
           pl.BlockSpec(sel.shape, lambda bb, g, p, i: (0, 0, 0)),
           pl.BlockSpec(pmat.shape, lambda bb, g, p, i: (0, 0))])
    return pl.pallas_call(
        functools.partial(_ssd_kernel, tl=tl, n_blk=n_blk, ctx_len=ctx_len),
        grid=grid,
        in_specs=in_specs,
        out_specs=pl.BlockSpec((1, tl, GROUP_W), lambda bb, g, p, i: (bb, out_blk(p, i), g)),
        out_shape=jax.ShapeDtypeStruct((b, l, D_INNER), BF16),
        scratch_shapes=[pltpu.VMEM((l, N_SLAB * LANE), BF16),
                        pltpu.VMEM((l, GROUP_W), F32),
                        pltpu.VMEM((D_STATE, GROUP_W), F32),
                        pltpu.VMEM((D_STATE, GROUP_W), F32),
                        pltpu.VMEM((N_SLAB, tl + 2 * HALO, LANE), F32),
                        pltpu.VMEM((tl, LANE), BF16),
                        pltpu.VMEM((tl, 2 * GROUP_W), F32)],
        compiler_params=pltpu.CompilerParams(
            dimension_semantics=("arbitrary", "arbitrary", "arbitrary", "arbitrary"),
            vmem_limit_bytes=VMEM_LIMIT),
        name="ssd",
    )(u, u, u, u, u, u, u, u, u, u, rows, cu, cu, crows,
      conv_w, conv_w, conv_w, conv_b, conv_b, conv_b, dskip, gain, onesel, sel, pmat)


def _dft_mats():
    def cs(n):
        k = np.arange(n)
        ang = 2.0 * np.pi * np.outer(k, k) / n
        return np.cos(ang), np.sin(ang)

    c3, s3 = cs(FFT_GROUP_DIM)
    w1 = np.concatenate([c3, -s3], axis=1)
    c2, s2 = cs(GRID_W)
    m2 = np.block([[c2, s2], [-s2, c2]])
    return w1, m2


def _fft12_kernel(f_ref, w1_ref, m2_ref, o_ref, *, tm):
    f = f_ref[0]
    w1 = w1_ref[...]
    ps, qs = [], []
    for g in range(FFT_GROUPS):
        pq = _dot(f[:, g * FFT_GROUP_DIM:(g + 1) * FFT_GROUP_DIM], w1)
        ps.append(pq[:, :FFT_GROUP_DIM])
        qs.append(pq[:, FFT_GROUP_DIM:])
    pr = jnp.concatenate(ps, axis=1).astype(BF16)
    qr = jnp.concatenate(qs, axis=1).astype(BF16)
    m2 = m2_ref[...]
    pad = jnp.zeros((FFT_PITCH - GRID_W, D_MODEL), F32)
    for r in range(tm // GRID_W):
        sl = slice(r * GRID_W, (r + 1) * GRID_W)
        o = _dot(m2, jnp.concatenate([pr[sl], qr[sl]], axis=0))
        r0 = r * FFT_PITCH
        for part in range(2):
            o_ref[0, part, r0:r0 + GRID_W, :] = o[part * GRID_W:(part + 1) * GRID_W]
            o_ref[0, part, r0 + GRID_W:r0 + FFT_PITCH, :] = pad


def _fft3_kernel(ri_ref, m3_ref, o_ref, scr_ref, *, rows):
    m3 = m3_ref[...]

    def column(w, carry):
        re = ri_ref[0, 0, pl.ds(w, rows, stride=FFT_PITCH), :]
        im = ri_ref[0, 1, pl.ds(w, rows, stride=FFT_PITCH), :]
        x = jnp.concatenate([re, im], axis=0).astype(BF16)
        scr_ref[pl.ds(w, rows, stride=FFT_PITCH), :] = _dot(m3, x)
        return carry

    lax.fori_loop(0, GRID_W, column, 0, unroll=2)
    for k in range(rows):
        o_ref[0, k * GRID_W:(k + 1) * GRID_W, :] = scr_ref[k * FFT_PITCH:k * FFT_PITCH + GRID_W, :].astype(BF16)


def _fourier(u, tm):
    b, l, _ = u.shape
    rows = l // GRID_W
    w1, m2 = _dft_mats()
    ri = pl.pallas_call(
        functools.partial(_fft12_kernel, tm=tm),
        grid=(b, l // tm),
        in_specs=[pl.BlockSpec((1, tm, D_MODEL), lambda i, j: (i, j, U_FFT // D_MODEL)),
                  pl.BlockSpec((FFT_GROUP_DIM, 2 * FFT_GROUP_DIM), lambda i, j: (0, 0)),
                  pl.BlockSpec((2 * GRID_W, 2 * GRID_W), lambda i, j: (0, 0))],
        out_specs=pl.BlockSpec((1, 2, tm // GRID_W * FFT_PITCH, D_MODEL), lambda i, j: (i, 0, j, 0)),
        out_shape=jax.ShapeDtypeStruct((b, 2, rows * FFT_PITCH, D_MODEL), F32),
        compiler_params=pltpu.CompilerParams(dimension_semantics=("arbitrary", "arbitrary"),
                                             vmem_limit_bytes=VMEM_LIMIT),
        name="fft12",
    )(u, jnp.asarray(w1, BF16), jnp.asarray(m2, BF16))

    k = np.arange(rows)
    ang = 2.0 * np.pi * np.outer(k, k) / rows
    scale = 1.0 / np.sqrt(float(rows * GRID_W * FFT_GROUP_DIM))
    m3 = np.concatenate([np.cos(ang), np.sin(ang)], axis=1) * scale
    return pl.pallas_call(
        functools.partial(_fft3_kernel, rows=rows),
        grid=(b, D_MODEL // LANE),
        in_specs=[pl.BlockSpec((1, 2, rows * FFT_PITCH, LANE), lambda i, j: (i, 0, 0, j)),
                  pl.BlockSpec((rows, 2 * rows), lambda i, j: (0, 0))],
        out_specs=pl.BlockSpec((1, l, LANE), lambda i, j: (i, 0, j)),
        out_shape=jax.ShapeDtypeStruct((b, l, D_MODEL), BF16),
        scratch_shapes=[pltpu.VMEM((rows * FFT_PITCH, LANE), F32)],
        compiler_params=pltpu.CompilerParams(dimension_semantics=("arbitrary", "arbitrary"),
                                             vmem_limit_bytes=VMEM_LIMIT),
        name="fft3",
    )(ri, jnp.asarray(m3, BF16))


FF_SPLIT = 2


def _tail_kernel(y_ref, f_ref, ga_ref, gb_ref, x_ref, g1_ref, sh2_ref, sc2_ref, g2_ref,
                 wso_ref, wfo_ref, wo_ref, n2_ref, wfi_ref, wfu_ref, wfd_ref, fg_ref, o_ref):
    a = _dot(y_ref[...], wso_ref[...])
    bq = _dot(f_ref[...], wfo_ref[...])
    merged = jax.nn.sigmoid(ga_ref[...].astype(F32)) * a + jax.nn.sigmoid(gb_ref[...].astype(F32)) * bq
    xn = x_ref[...] + g1_ref[0] * _dot(merged.astype(BF16), wo_ref[...])
    h = (_rmsnorm(xn, n2_ref[...]) * (1.0 + sc2_ref[0]) + sh2_ref[0]).astype(BF16)
    ffw = D_FF // FF_SPLIT
    acc = None
    for s in range(FF_SPLIT):
        sl = slice(s * ffw, (s + 1) * ffw)
        gate = _dot(h, wfi_ref[:, sl])
        up = _dot(h, wfu_ref[:, sl])
        act = (gate * jax.nn.sigmoid(gate) * up).astype(BF16)
        part = _dot(act, wfd_ref[sl, :])
        acc = part if acc is None else acc + part
    o_ref[...] = _rmsnorm(xn + g2_ref[0] * acc, fg_ref[...])


def _tail(y2d, f2d, u2d, x2d, g1, sh2, sc2, g2, wso, wfo, wo, n2, wfi, wfu, wfd, fg, tm, rows_per_mod):
    m = x2d.shape[0]
    bpm = rows_per_mod // tm

    def const(shape):
        return pl.BlockSpec(shape, lambda i: (0, 0), pipeline_mode=pl.Buffered(1))

    def mod():
        return pl.BlockSpec((1, 1, D_MODEL), lambda i: (i // bpm, 0, 0))

    return pl.pallas_call(
        _tail_kernel,
        grid=(m // tm,),
        in_specs=[pl.BlockSpec((tm, D_INNER), lambda i: (i, 0)),
                  pl.BlockSpec((tm, D_MODEL), lambda i: (i, 0)),
                  pl.BlockSpec((tm, D_MODEL), lambda i: (i, U_GATE // D_MODEL)),
                  pl.BlockSpec((tm, D_MODEL), lambda i: (i, U_GATE // D_MODEL + 1)),
                  pl.BlockSpec((tm, D_MODEL), lambda i: (i, 0)),
                  mod(), mod(), mod(), mod(),
                  const((D_INNER, D_MODEL)), const((D_MODEL, D_MODEL)), const((D_MODEL, D_MODEL)),
                  const((1, D_MODEL)),
                  const((D_MODEL, D_FF)), const((D_MODEL, D_FF)), const((D_FF, D_MODEL)),
                  const((1, D_MODEL))],
        out_specs=pl.BlockSpec((tm, D_MODEL), lambda i: (i, 0)),
        out_shape=jax.ShapeDtypeStruct((m, D_MODEL), F32),
        compiler_params=pltpu.CompilerParams(dimension_semantics=("arbitrary",),
                                             vmem_limit_bytes=VMEM_LIMIT),
        name="tail",
    )(y2d, f2d, u2d, u2d, x2d, g1, sh2, sc2, g2, wso, wfo, wo, n2, wfi, wfu, wfd, fg)


def _dt_lanes(a):
    lead = a.shape[:-1]
    a = a.reshape(lead + (2, N_GROUPS, HEADS_PER_GROUP))
    a = jnp.swapaxes(a, -3, -2).reshape(lead + (2 * N_HEADS,))
    return jnp.concatenate([a, jnp.zeros(lead + (LANE - 2 * N_HEADS,), a.dtype)], axis=-1)


def _block(x, c, ctx, c_ctx, w_ada, b_ada, norm1_g, w_in, conv_w, conv_b, dt_bias, a_log, d_skip,
           ssd_norm_g, w_ssd_out, w_fft_out, w_o, norm2_g, w_ffn_in, w_ffn_out, final_g,
           proj_tm, prep_tl, ssd_tl, fft_tm, tail_tm):
    b, l, d = x.shape
    ctx_len = ctx.shape[1]
    assert d == D_MODEL and l % (GRID_W * 2) == 0 and ctx_len % CHUNK == 0 and b + 1 <= SUBLANE

    w_in0 = w_in[0]
    w_rest = w_in0[:, REF_Z:].astype(BF16)
    w_dt = _dt_lanes(w_in0[:, REF_DT:REF_Z])
    w_dt_hi = w_dt.astype(BF16)
    w_dt_lo = (w_dt - w_dt_hi.astype(F32)).astype(BF16)
    w_dt2 = jnp.concatenate([w_dt_hi, w_dt_lo], axis=1)
    conv_w_p = jnp.concatenate([conv_w[0], jnp.zeros((SUBLANE - D_CONV, CONV_DIM), F32)], axis=0)
    conv_b_p = conv_b[0].reshape(1, CONV_DIM)
    bias_v = _dt_lanes(dt_bias[0].reshape(-1)).reshape(1, LANE)
    alog_v = _dt_lanes(a_log[0].reshape(-1)).reshape(1, LANE)
    dskip_v = jnp.repeat(d_skip[0], HEAD_DIM).reshape(1, D_INNER)
    gain_v = ssd_norm_g[0].reshape(1, D_INNER)
    wso = w_ssd_out[0].astype(BF16)
    wfo = w_fft_out[0].astype(BF16)
    wo = w_o[0].astype(BF16)
    wfi = w_ffn_in[0][:, :D_FF].astype(BF16)
    wfu = w_ffn_in[0][:, D_FF:].astype(BF16)
    wfd = w_ffn_out[0].astype(BF16)

    cvecs = jnp.concatenate([c, c_ctx[None, :], jnp.zeros((SUBLANE - b - 1, D_MODEL), F32)], axis=0)
    mods = _mods(cvecs, b + 1, w_ada[0], b_ada[0])
    lat = mods[:b].reshape(b, 6, 1, D_MODEL)
    sh1, sc1, g1, sh2, sc2, g2 = (lat[:, k] for k in range(6))
    cm = mods[b].reshape(6, 1, 1, D_MODEL)

    x2d = x.reshape(b * l, D_MODEL)
    n1 = norm1_g[0].reshape(1, D_MODEL)
    u2d, dt_raw = _inproj(x2d, sh1, sc1, n1, w_in0, w_rest, w_dt2, U_WIDTH, proj_tm, l)
    u = u2d.reshape(b, l, U_WIDTH)
    cu2d, cdt_raw = _inproj(ctx.reshape(b * ctx_len, D_MODEL), cm[0], cm[1], n1, w_in0, w_rest, w_dt2,
                            CONV_DIM, b * ctx_len, b * ctx_len)
    cu = cu2d.reshape(b, ctx_len, CONV_DIM)

    rows = _dtprep(dt_raw.reshape(b, l, LANE), bias_v, alog_v, prep_tl)
    crows = _dtprep(cdt_raw.reshape(b, ctx_len, LANE), bias_v, alog_v, ctx_len)
    y_ssd = _ssd(u, rows, cu, crows, conv_w_p, conv_b_p, dskip_v, gain_v, ssd_tl)

    f_mix = _fourier(u, fft_tm)

    out = _tail(y_ssd.reshape(b * l, D_INNER), f_mix.reshape(b * l, D_MODEL), u2d, x2d,
                g1, sh2, sc2, g2, wso, wfo, wo, norm2_g[0].reshape(1, D_MODEL), wfi, wfu, wfd,
                final_g.reshape(1, D_MODEL), tail_tm, l)
    return out.reshape(b, l, D_MODEL)


def kernel(x, c, ctx, c_ctx, w_ada, b_ada, norm1_g, w_in, conv_w, conv_b, dt_bias, a_log, d_skip,
           ssd_norm_g, w_ssd_out, w_fft_out, w_o, norm2_g, w_ffn_in, w_ffn_out, final_g):
    return _block(x, c, ctx, c_ctx, w_ada, b_ada, norm1_g, w_in, conv_w, conv_b, dt_bias, a_log, d_skip,
                  ssd_norm_g, w_ssd_out, w_fft_out, w_o, norm2_g, w_ffn_in, w_ffn_out, final_g,
                  proj_tm=1024, prep_tl=1024, ssd_tl=2048, fft_tm=512, tail_tm=512)
```

```python
import functools

import jax
import jax.numpy as jnp
import numpy as np
from jax import lax
from jax.experimental import pallas as pl
from jax.experimental.pallas import tpu as pltpu

F32 = jnp.float32
BF16 = jnp.bfloat16

D_MODEL = 1024
EPS = 1e-6
GRID_W = 64

N_GROUPS = 8
HEADS_PER_GROUP = 4
HEAD_DIM = 64
N_HEADS = N_GROUPS * HEADS_PER_GROUP
D_STATE = 128
D_INNER = N_HEADS * HEAD_DIM
GROUP_W = HEADS_PER_GROUP * HEAD_DIM
GS = N_GROUPS * D_STATE
CONV_DIM = D_INNER + 2 * GS
D_CONV = 5
CHUNK = 128

FFT_GROUPS = 8
FFT_GROUP_DIM = 128
D_FF = 2816

REF_DT = CONV_DIM
REF_Z = REF_DT + 2 * N_HEADS

U_B = D_INNER
U_C = U_B + GS
U_Z = CONV_DIM
U_FFT = U_Z + D_INNER
U_GATE = U_FFT + D_MODEL
U_WIDTH = U_GATE + 2 * D_MODEL

LANE = 128
SUBLANE = 8
HALO = 16
VMEM_LIMIT = 56 * 1024 * 1024

N_DH = 2 * HEADS_PER_GROUP
R_CUM, R_NCX, R_EY, R_WX, R_DIAG = 0, 8, 16, 24, 32
ROWS_PER_GROUP = 40
N_SLAB = (GROUP_W + 2 * D_STATE) // LANE
DT_FLOOR = 1e-37
N_CUM_TERMS = 3
K_ARGS = 2 * N_CUM_TERMS * N_DH
FFT_PITCH = GRID_W + 4


def _dot(a, b):
    return jnp.dot(a, b, preferred_element_type=F32)


def _dot_tn(a, b):
    return lax.dot_general(a, b, (((0,), (0,)), ((), ())), preferred_element_type=F32)


def _dot_nt(a, b):
    return lax.dot_general(a, b, (((1,), (1,)), ((), ())), preferred_element_type=F32)


def _rmsnorm(x, g):
    return x * lax.rsqrt(jnp.mean(x * x, axis=-1, keepdims=True) + EPS) * g


def _mods_kernel(ct_ref, w_ref, b_ref, o_ref, *, n_vecs):
    s = ct_ref[...]
    s = s * jax.nn.sigmoid(s)
    w = w_ref[...]
    rows = [jnp.sum(w * s[:, r:r + 1], axis=0, keepdims=True) for r in range(n_vecs)]
    rows.append(jnp.zeros((SUBLANE - n_vecs, w.shape[1]), F32))
    o_ref[...] = jnp.concatenate(rows, axis=0) + b_ref[...]


def _mods(cvecs, n_vecs, w_ada, b_ada):
    n = w_ada.shape[1]
    tn = 1536
    return pl.pallas_call(
        functools.partial(_mods_kernel, n_vecs=n_vecs),
        grid=(n // tn,),
        in_specs=[pl.BlockSpec((D_MODEL, SUBLANE), lambda j: (0, 0)),
                  pl.BlockSpec((D_MODEL, tn), lambda j: (0, j)),
                  pl.BlockSpec((1, tn), lambda j: (0, j))],
        out_specs=pl.BlockSpec((SUBLANE, tn), lambda j: (0, j)),
        out_shape=jax.ShapeDtypeStruct((SUBLANE, n), F32),
        compiler_params=pltpu.CompilerParams(dimension_semantics=("arbitrary",),
                                             vmem_limit_bytes=VMEM_LIMIT),
        name="mods",
    )(cvecs.T, w_ada, b_ada.reshape(1, n))


def _inproj_kernel(x_ref, sh_ref, sc_ref, g_ref, w_ref, wdt_ref, u_ref, dt_ref, h_ref):
    @pl.when(pl.program_id(1) == 0)
    def _():
        h = _rmsnorm(x_ref[...], g_ref[...]) * (1.0 + sc_ref[0]) + sh_ref[0]
        hb = h.astype(BF16)
        h_ref[...] = hb
        hl = (h - hb.astype(F32)).astype(BF16)
        a = _dot_nt(hb, wdt_ref[...])
        dt_ref[...] = a[:, :LANE] + a[:, LANE:] + _dot_nt(hl, wdt_ref[:LANE, :])

    u_ref[...] = _dot_nt(h_ref[...], w_ref[...]).astype(BF16)


def _inproj(x2d, shift, scale, gain, w_t, w_dt_t, n_cols, tm, rows_per_mod):
    m = x2d.shape[0]
    tn = 1024
    bpm = rows_per_mod // tm
    return pl.pallas_call(
        _inproj_kernel,
        grid=(m // tm, n_cols // tn),
        in_specs=[pl.BlockSpec((tm, D_MODEL), lambda i, j: (i, 0)),
                  pl.BlockSpec((1, 1, D_MODEL), lambda i, j: (i // bpm, 0, 0)),
                  pl.BlockSpec((1, 1, D_MODEL), lambda i, j: (i // bpm, 0, 0)),
                  pl.BlockSpec((1, D_MODEL), lambda i, j: (0, 0)),
                  pl.BlockSpec((tn, D_MODEL), lambda i, j: (j, 0)),
                  pl.BlockSpec((2 * LANE, D_MODEL), lambda i, j: (0, 0))],
        out_specs=[pl.BlockSpec((tm, tn), lambda i, j: (i, j)),
                   pl.BlockSpec((tm, LANE), lambda i, j: (i, 0))],
        out_shape=[jax.ShapeDtypeStruct((m, n_cols), BF16),
                   jax.ShapeDtypeStruct((m, LANE), F32)],
        scratch_shapes=[pltpu.VMEM((tm, D_MODEL), BF16)],
        compiler_params=pltpu.CompilerParams(dimension_semantics=("arbitrary", "arbitrary"),
                                             vmem_limit_bytes=VMEM_LIMIT),
        name="inproj",
    )(x2d, shift, scale, gain, w_t, w_dt_t)


def _dtprep_kernel(raw_ref, bias_ref, alog_ref, row_ref, *, n_chunks):
    a = -jnp.exp(alog_ref[...])
    bias = bias_ref[...]
    ii = lax.broadcasted_iota(jnp.int32, (CHUNK, CHUNK), 0)
    jj = lax.broadcasted_iota(jnp.int32, (CHUNK, CHUNK), 1)
    tril = (jj <= ii).astype(F32)
    triu = (jj >= ii).astype(F32)
    lane = lax.broadcasted_iota(jnp.int32, (1, LANE), 1)
    is_fwd = (lane % N_DH) < HEADS_PER_GROUP
    for c in range(n_chunks):
        sl = slice(c * CHUNK, (c + 1) * CHUNK)
        dt = jax.nn.softplus(raw_ref[0, sl, :] + bias)
        dta = dt * a
        cf = jnp.dot(tril, dta, precision=lax.Precision.HIGHEST, preferred_element_type=F32)
        cb = jnp.dot(triu, dta, precision=lax.Precision.HIGHEST, preferred_element_type=F32)
        cum = jnp.where(is_fwd, cf, cb)
        tot = jnp.where(is_fwd, cf[CHUNK - 1:CHUNK, :], cb[0:1, :])
        wx = dt * jnp.exp(tot - cum)
        ey = jnp.exp(cum)
        ncx = jnp.log(jnp.maximum(dt, DT_FLOOR)) - cum
        diag = jnp.log(dt + pltpu.roll(dt, LANE - HEADS_PER_GROUP, 1))
        for r_off, val in ((R_CUM, cum), (R_NCX, ncx), (R_EY, ey), (R_WX, wx), (R_DIAG, diag)):
            vt = val.T
            for g in range(N_GROUPS):
                r0 = g * ROWS_PER_GROUP + r_off
                row_ref[0, r0:r0 + N_DH, sl] = vt[g * N_DH:(g + 1) * N_DH, :]


def _dtprep(raw, bias, alog, tl):
    b, l, _ = raw.shape
    return pl.pallas_call(
        functools.partial(_dtprep_kernel, n_chunks=tl // CHUNK),
        grid=(b, l // tl),
        in_specs=[pl.BlockSpec((1, tl, LANE), lambda i, j: (i, j, 0)),
                  pl.BlockSpec((1, LANE), lambda i, j: (0, 0)),
                  pl.BlockSpec((1, LANE), lambda i, j: (0, 0))],
        out_specs=pl.BlockSpec((1, N_GROUPS * ROWS_PER_GROUP, tl), lambda i, j: (i, 0, j)),
        out_shape=jax.ShapeDtypeStruct((b, N_GROUPS * ROWS_PER_GROUP, l), F32),
        compiler_params=pltpu.CompilerParams(dimension_semantics=("arbitrary", "arbitrary"),
                                             vmem_limit_bytes=VMEM_LIMIT),
        name="dtprep",
    )(raw, bias, alog)


def _split_terms(x, n):
    terms = []
    for _ in range(n - 1):
        t = x.astype(BF16).astype(F32)
        terms.append(t)
        x = x - t
    terms.append(x)
    return terms


def _scale_rows(rows):
    return jnp.concatenate(_split_terms(rows[R_EY:R_EY + N_DH], 2) + _split_terms(rows[R_WX:R_WX + N_DH], 2),
                           axis=0).astype(BF16)


def _conv_silu(ext_ref, s, r0, cw, cbias):
    sl = slice(s * LANE, (s + 1) * LANE)
    acc = cbias[:, sl]
    for k in range(D_CONV):
        acc = acc + cw[k:k + 1, sl] * ext_ref[s, pl.ds(r0 + k, CHUNK), :]
    return acc * jax.nn.sigmoid(acc)


def _ssd_kernel(x_ref, b_ref, c_ref, xp_ref, bp_ref, cp_ref, xn_ref, bn_ref, cn_ref, z_ref, row_ref,
                cx_ref, cb_ref, crow_ref, cwx_ref, cwb_ref, cwc_ref, cbx_ref, cbb_ref, cbc_ref,
                dskip_ref, gain_ref, onesel_ref, sel_ref, out_ref,
                cache_ref, yacc_ref, sf_ref, sb_ref, ext_ref, args0_ref, args1_ref, ew0_ref, ew1_ref,
                *, tl, n_blk, ctx_len):
    p = pl.program_id(2)
    i = pl.program_id(3)
    n_ch = tl // CHUNK
    args_bufs = (args0_ref, args1_ref)
    ew_bufs = (ew0_ref, ew1_ref)
    cw = jnp.concatenate([cwx_ref[...], cwb_ref[...], cwc_ref[...]], axis=1)
    cbias = jnp.concatenate([cbx_ref[...], cbb_ref[...], cbc_ref[...]], axis=1)
    tap0 = HALO - D_CONV // 2

    def slabs_of(x, bm, cm):
        return [x[:, :LANE], x[:, LANE:], bm, cm]

    def fill_ext(src, first, last, n_rows):
        for s in range(len(src)):
            ext_ref[s, 0:HALO, :] = first[s]
            ext_ref[s, HALO:HALO + n_rows, :] = src[s].astype(F32)
            ext_ref[s, HALO + n_rows:2 * HALO + n_rows, :] = last[s]

    def row_of(c):
        return c * CHUNK if isinstance(c, int) else pl.multiple_of(c * CHUNK, CHUNK)

    def expand(rowt, d):
        return _dot_tn(_scale_rows(rowt), sel_ref[d])

    def run_pipelined(stage_a, stage_b, order):
        stage_a(order(0), 0)

        def two_chunks(t, carry):
            k = 2 * t
            stage_b(order(k), 0)
            stage_a(order(k + 1), 1)
            stage_b(order(k + 1), 1)
            stage_a(order(k + 2), 0)
            return carry

        lax.fori_loop(0, n_ch // 2 - 1, two_chunks, 0)
        stage_b(order(n_ch - 2), 0)
        stage_a(order(n_ch - 1), 1)
        stage_b(order(n_ch - 1), 1)

    @pl.when(jnp.logical_and(p == 0, i == 0))
    def _init_from_context():
        zpad = [jnp.zeros((HALO, LANE), F32)] * (N_SLAB - 1)
        cx = cx_ref[0]
        fill_ext([cx[:, :LANE], cx[:, LANE:], cb_ref[0]], zpad, zpad, ctx_len)
        n_cc = ctx_len // CHUNK
        sf = jnp.zeros((D_STATE, GROUP_W), F32)
        st_b, ey_b = [], []
        for cc in range(n_cc):
            slabs = [_conv_silu(ext_ref, s, cc * CHUNK + tap0, cw, cbias) for s in range(N_SLAB - 1)]
            xs = jnp.concatenate(slabs[:2], axis=1)
            bm = slabs[2].astype(BF16)
            rowt = crow_ref[0, :, cc * CHUNK:(cc + 1) * CHUNK]
            ewf = expand(rowt, 0)
            ewb = expand(rowt, 1)
            sf = sf * ewf[CHUNK - 1:CHUNK, :GROUP_W] + _dot_tn(bm, (xs * ewf[:, GROUP_W:]).astype(BF16))
            st_b.append(_dot_tn(bm, (xs * ewb[:, GROUP_W:]).astype(BF16)))
            ey_b.append(ewb[0:1, :GROUP_W])
        sb = jnp.zeros((D_STATE, GROUP_W), F32)
        for cc in reversed(range(n_cc)):
            sb = sb * ey_b[cc] + st_b[cc]
        sf_ref[...] = sf
        sb_ref[...] = sb

    @pl.when(p == 0)
    def _forward():
        blk = i
        base = blk * tl
        has_prev = blk > 0
        has_next = blk < n_blk - 1
        fill_ext(slabs_of(x_ref[0], b_ref[0], c_ref[0]),
                 [jnp.where(has_prev, v.astype(F32), 0.0) for v in slabs_of(xp_ref[0], bp_ref[0], cp_ref[0])],
                 [jnp.where(has_next, v.astype(F32), 0.0) for v in slabs_of(xn_ref[0], bn_ref[0], cn_ref[0])],
                 tl)
        ii = lax.broadcasted_iota(jnp.int32, (CHUNK, CHUNK), 0)
        jj = lax.broadcasted_iota(jnp.int32, (CHUNK, CHUNK), 1)
        lower = jj < ii
        upper = jj > ii

        def prepare(c, buf):
            r = row_of(c)
            slabs = [_conv_silu(ext_ref, s, r + tap0, cw, cbias) for s in range(N_SLAB)]
            for s in range(N_SLAB):
                cache_ref[pl.ds(base + r, CHUNK), s * LANE:(s + 1) * LANE] = slabs[s].astype(BF16)
            yacc_ref[pl.ds(base + r, CHUNK), :] = jnp.concatenate(slabs[:2], axis=1) * dskip_ref[...]
            rowt = row_ref[0, :, pl.ds(r, CHUNK)]
            onesel = onesel_ref[...]
            ones = jnp.ones((N_DH, CHUNK), F32)
            at = jnp.concatenate(_split_terms(rowt[R_CUM:R_CUM + N_DH], N_CUM_TERMS) + [ones] * N_CUM_TERMS,
                                 axis=0).astype(BF16)
            wide = [jnp.concatenate([t] * N_DH, axis=1) * onesel
                    for t in _split_terms(rowt[R_NCX:R_NCX + N_DH], N_CUM_TERMS)]
            bmat = jnp.concatenate([onesel] * N_CUM_TERMS + wide, axis=0).astype(BF16)
            args_bufs[buf][...] = _dot_tn(at, bmat)
            ew_bufs[buf][...] = expand(rowt, 0)

        def consume(c, buf):
            r = row_of(c)
            xbc_b = cache_ref[pl.ds(base + r, CHUNK), :]
            xs_b = xbc_b[:, :GROUP_W]
            bm = xbc_b[:, GROUP_W:GROUP_W + D_STATE]
            cm = xbc_b[:, GROUP_W + D_STATE:]
            cbm = _dot_nt(cm, bm)
            yd = []
            for j in range(HEADS_PER_GROUP):
                argf = args_bufs[buf][:, j * CHUNK:(j + 1) * CHUNK]
                argb = args_bufs[buf][:, (HEADS_PER_GROUP + j) * CHUNK:(HEADS_PER_GROUP + j + 1) * CHUNK]
                diag = row_ref[0, R_DIAG + j:R_DIAG + j + 1, pl.ds(r, CHUNK)]
                arg = jnp.where(lower, argf, jnp.where(upper, argb, diag))
                wmat = (cbm * jnp.exp(arg)).astype(BF16)
                yd.append(_dot(wmat, xs_b[:, j * HEAD_DIM:(j + 1) * HEAD_DIM]))
            ef = ew_bufs[buf][:, :GROUP_W]
            wxf = ew_bufs[buf][:, GROUP_W:]
            sf = sf_ref[...]
            yacc_ref[pl.ds(base + r, CHUNK), :] += jnp.concatenate(yd, axis=1) + _dot(cm, sf.astype(BF16)) * ef
            sf_ref[...] = sf * ef[CHUNK - 1:CHUNK, :] + _dot_tn(bm, (xs_b.astype(F32) * wxf).astype(BF16))

        run_pipelined(prepare, consume, lambda k: k)

    @pl.when(p == 1)
    def _backward():
        blk = n_blk - 1 - i
        base = blk * tl

        def prepare(c, buf):
            ew_bufs[buf][...] = expand(row_ref[0, :, pl.ds(row_of(c), CHUNK)], 1)

        def consume(c, buf):
            r = row_of(c)
            xbc_b = cache_ref[pl.ds(base + r, CHUNK), :]
            xs = xbc_b[:, :GROUP_W].astype(F32)
            bm = xbc_b[:, GROUP_W:GROUP_W + D_STATE]
            cm = xbc_b[:, GROUP_W + D_STATE:]
            eb = ew_bufs[buf][:, :GROUP_W]
            wxb = ew_bufs[buf][:, GROUP_W:]
            sb = sb_ref[...]
            y = yacc_ref[pl.ds(base + r, CHUNK), :] + _dot(cm, sb.astype(BF16)) * eb
            sb_ref[...] = sb * eb[0:1, :] + _dot_tn(bm, (xs * wxb).astype(BF16))
            z = z_ref[0, pl.ds(r, CHUNK), :].astype(F32)
            y = y * (z * jax.nn.sigmoid(z))
            out_ref[0, pl.ds(r, CHUNK), :] = _rmsnorm(y, gain_ref[...]).astype(BF16)

        run_pipelined(prepare, consume, lambda k: n_ch - 1 - k)


def _ssd_selectors():
    onesel = (np.arange(N_DH * CHUNK)[None, :] // CHUNK == np.arange(N_DH)[:, None]).astype(np.float32)
    sel = np.zeros((2, 4 * N_DH, 2 * GROUP_W), np.float32)
    for d in range(2):
        for j in range(HEADS_PER_GROUP):
            dh = d * HEADS_PER_GROUP + j
            for term in range(2):
                sel[d, term * N_DH + dh, j * HEAD_DIM:(j + 1) * HEAD_DIM] = 1.0
                sel[d, (2 + term) * N_DH + dh, GROUP_W + j * HEAD_DIM:GROUP_W + (j + 1) * HEAD_DIM] = 1.0
    return jnp.asarray(onesel, F32), jnp.asarray(sel, BF16)


def _ssd(u, rows, cu, crows, conv_w, conv_b, dskip, gain, tl):
    b, l, _ = u.shape
    ctx_len = cu.shape[1]
    assert ctx_len <= tl
    n_blk = l // tl
    hb = tl // HALO
    last_halo = l // HALO - 1
    assert (tl // CHUNK) % 2 == 0
    onesel, sel = _ssd_selectors()
    xg, bg, cg = 0, U_B // D_STATE, U_C // D_STATE

    def in_blk(p, i):
        return i * (1 - p) + (n_blk - 1) * p

    def seq_blk(p, i):
        return i * (1 - p) + (n_blk - 1 - i) * p

    def out_blk(p, i):
        return (n_blk - 1) * (1 - p) + (n_blk - 1 - i) * p

    def prev_halo(p, i):
        return jnp.maximum(in_blk(p, i) * hb - 1, 0)

    def next_halo(p, i):
        return jnp.minimum((in_blk(p, i) + 1) * hb, last_halo)

    def xbc_specs(rows_blk, row_index):
        return [pl.BlockSpec((1, rows_blk, GROUP_W), lambda bb, g, p, i: (bb, row_index(p, i), xg + g)),
                pl.BlockSpec((1, rows_blk, D_STATE), lambda bb, g, p, i: (bb, row_index(p, i), bg + g)),
                pl.BlockSpec((1, rows_blk, D_STATE), lambda bb, g, p, i: (bb, row_index(p, i), cg + g))]

    def param_specs(n_rows):
        return [pl.BlockSpec((n_rows, GROUP_W), lambda bb, g, p, i: (0, xg + g)),
                pl.BlockSpec((n_rows, D_STATE), lambda bb, g, p, i: (0, bg + g)),
                pl.BlockSpec((n_rows, D_STATE), lambda bb, g, p, i: (0, cg + g))]

    grid = (b, N_GROUPS, 2, n_blk)
    in_specs = (
        xbc_specs(tl, in_blk) + xbc_specs(HALO, prev_halo) + xbc_specs(HALO, next_halo)
        + [pl.BlockSpec((1, tl, GROUP_W), lambda bb, g, p, i: (bb, out_blk(p, i), U_Z // GROUP_W + g)),
           pl.BlockSpec((1, ROWS_PER_GROUP, tl), lambda bb, g, p, i: (bb, g, seq_blk(p, i)))]
        + xbc_specs(ctx_len, lambda p, i: 0)[:2]
        + [pl.BlockSpec((1, ROWS_PER_GROUP, ctx_len), lambda bb, g, p, i: (bb, g, 0))]
        + param_specs(SUBLANE) + param_specs(1)
        + [pl.BlockSpec((1, GROUP_W), lambda bb, g, p, i: (0, g)),
           pl.BlockSpec((1, GROUP_W), lambda bb, g, p, i: (0, g)),
           pl.BlockSpec(onesel.shape, lambda bb, g, p, i: (0, 0)),
           pl.BlockSpec(sel.shape, lambda bb, g, p, i: (0, 0, 0))])
    return pl.pallas_call(
        functools.partial(_ssd_kernel, tl=tl, n_blk=n_blk, ctx_len=ctx_len),
        grid=grid,
        in_specs=in_specs,
        out_specs=pl.BlockSpec((1, tl, GROUP_W), lambda bb, g, p, i: (bb, out_blk(p, i), g)),
        out_shape=jax.ShapeDtypeStruct((b, l, D_INNER), BF16),
        scratch_shapes=[pltpu.VMEM((l, N_SLAB * LANE), BF16),
                        pltpu.VMEM((l, GROUP_W), F32),
                        pltpu.VMEM((D_STATE, GROUP_W), F32),
                        pltpu.VMEM((D_STATE, GROUP_W), F32),
                        pltpu.VMEM((N_SLAB, tl + 2 * HALO, LANE), F32),
                        pltpu.VMEM((CHUNK, N_DH * CHUNK), F32),
                        pltpu.VMEM((CHUNK, N_DH * CHUNK), F32),
                        pltpu.VMEM((CHUNK, 2 * GROUP_W), F32),
                        pltpu.VMEM((CHUNK, 2 * GROUP_W), F32)],
        compiler_params=pltpu.CompilerParams(
            dimension_semantics=("arbitrary", "arbitrary", "arbitrary", "arbitrary"),
            vmem_limit_bytes=VMEM_LIMIT),
        name="ssd",
    )(u, u, u, u, u, u, u, u, u, u, rows, cu, cu, crows,
      conv_w, conv_w, conv_w, conv_b, conv_b, conv_b, dskip, gain, onesel, sel)


def _dft_mats():
    def cs(n):
        k = np.arange(n)
        ang = 2.0 * np.pi * np.outer(k, k) / n
        return np.cos(ang), np.sin(ang)

    c3, s3 = cs(FFT_GROUP_DIM)
    w1 = np.concatenate([c3, -s3], axis=1)
    c2, s2 = cs(GRID_W)
    m2 = np.block([[c2, s2], [-s2, c2]])
    return w1, m2


def _fft12_kernel(f_ref, w1_ref, m2_ref, o_ref, *, tm):
    f = f_ref[0]
    w1 = w1_ref[...]
    ps, qs = [], []
    for g in range(FFT_GROUPS):
        pq = _dot(f[:, g * FFT_GROUP_DIM:(g + 1) * FFT_GROUP_DIM], w1)
        ps.append(pq[:, :FFT_GROUP_DIM])
        qs.append(pq[:, FFT_GROUP_DIM:])
    pr = jnp.concatenate(ps, axis=1).astype(BF16)
    qr = jnp.concatenate(qs, axis=1).astype(BF16)
    m2 = m2_ref[...]
    pad = jnp.zeros((FFT_PITCH - GRID_W, D_MODEL), F32)
    for r in range(tm // GRID_W):
        sl = slice(r * GRID_W, (r + 1) * GRID_W)
        o = _dot(m2, jnp.concatenate([pr[sl], qr[sl]], axis=0))
        r0 = r * FFT_PITCH
        for part in range(2):
            o_ref[0, part, r0:r0 + GRID_W, :] = o[part * GRID_W:(part + 1) * GRID_W]
            o_ref[0, part, r0 + GRID_W:r0 + FFT_PITCH, :] = pad


def _fft3_kernel(ri_ref, m3_ref, o_ref, scr_ref, *, rows):
    m3 = m3_ref[...]

    def column_pair(t, carry):
        w = 2 * t
        cols = [jnp.concatenate([ri_ref[0, part, pl.ds(w + dw, rows, stride=FFT_PITCH), :] for part in range(2)],
                                axis=0) for dw in range(2)]
        o = _dot(m3, jnp.concatenate(cols, axis=1).astype(BF16))
        for dw in range(2):
            scr_ref[pl.ds(w + dw, rows, stride=FFT_PITCH), :] = o[:, dw * LANE:(dw + 1) * LANE]
        return carry

    lax.fori_loop(0, GRID_W // 2, column_pair, 0, unroll=4)
    for k in range(rows):
        o_ref[0, k * GRID_W:(k + 1) * GRID_W, :] = scr_ref[k * FFT_PITCH:k * FFT_PITCH + GRID_W, :].astype(BF16)


def _fourier(u, tm):
    b, l, _ = u.shape
    rows = l // GRID_W
    w1, m2 = _dft_mats()
    ri = pl.pallas_call(
        functools.partial(_fft12_kernel, tm=tm),
        grid=(b, l // tm),
        in_specs=[pl.BlockSpec((1, tm, D_MODEL), lambda i, j: (i, j, U_FFT // D_MODEL)),
                  pl.BlockSpec((FFT_GROUP_DIM, 2 * FFT_GROUP_DIM), lambda i, j: (0, 0)),
                  pl.BlockSpec((2 * GRID_W, 2 * GRID_W), lambda i, j: (0, 0))],
        out_specs=pl.BlockSpec((1, 2, tm // GRID_W * FFT_PITCH, D_MODEL), lambda i, j: (i, 0, j, 0)),
        out_shape=jax.ShapeDtypeStruct((b, 2, rows * FFT_PITCH, D_MODEL), F32),
        compiler_params=pltpu.CompilerParams(dimension_semantics=("arbitrary", "arbitrary"),
                                             vmem_limit_bytes=VMEM_LIMIT),
        name="fft12",
    )(u, jnp.asarray(w1, BF16), jnp.asarray(m2, BF16))

    k = np.arange(rows)
    ang = 2.0 * np.pi * np.outer(k, k) / rows
    scale = 1.0 / np.sqrt(float(rows * GRID_W * FFT_GROUP_DIM))
    m3 = np.concatenate([np.cos(ang), np.sin(ang)], axis=1) * scale
    return pl.pallas_call(
        functools.partial(_fft3_kernel, rows=rows),
        grid=(b, D_MODEL // LANE),
        in_specs=[pl.BlockSpec((1, 2, rows * FFT_PITCH, LANE), lambda i, j: (i, 0, 0, j)),
                  pl.BlockSpec((rows, 2 * rows), lambda i, j: (0, 0))],
        out_specs=pl.BlockSpec((1, l, LANE), lambda i, j: (i, 0, j)),
        out_shape=jax.ShapeDtypeStruct((b, l, D_MODEL), BF16),
        scratch_shapes=[pltpu.VMEM((rows * FFT_PITCH, LANE), F32)],
        compiler_params=pltpu.CompilerParams(dimension_semantics=("arbitrary", "arbitrary"),
                                             vmem_limit_bytes=VMEM_LIMIT),
        name="fft3",
    )(ri, jnp.asarray(m3, BF16))


FF_SPLIT = 2


def _tail_kernel(y_ref, f_ref, ga_ref, gb_ref, x_ref, g1_ref, sh2_ref, sc2_ref, g2_ref,
                 wso_ref, wfo_ref, wo_ref, n2_ref, wfi_ref, wfu_ref, wfd_ref, fg_ref, o_ref):
    a = _dot(y_ref[...], wso_ref[...])
    bq = _dot(f_ref[...], wfo_ref[...])
    merged = jax.nn.sigmoid(ga_ref[...].astype(F32)) * a + jax.nn.sigmoid(gb_ref[...].astype(F32)) * bq
    xn = x_ref[...] + g1_ref[0] * _dot(merged.astype(BF16), wo_ref[...])
    h = (_rmsnorm(xn, n2_ref[...]) * (1.0 + sc2_ref[0]) + sh2_ref[0]).astype(BF16)
    ffw = D_FF // FF_SPLIT
    acc = None
    for s in range(FF_SPLIT):
        sl = slice(s * ffw, (s + 1) * ffw)
        gate = _dot(h, wfi_ref[:, sl])
        up = _dot(h, wfu_ref[:, sl])
        act = (gate * jax.nn.sigmoid(gate) * up).astype(BF16)
        part = _dot(act, wfd_ref[sl, :])
        acc = part if acc is None else acc + part
    o_ref[...] = _rmsnorm(xn + g2_ref[0] * acc, fg_ref[...])


def _tail(y2d, f2d, u2d, x2d, g1, sh2, sc2, g2, wso, wfo, wo, n2, wfi, wfu, wfd, fg, tm, rows_per_mod):
    m = x2d.shape[0]
    bpm = rows_per_mod // tm

    def const(shape):
        return pl.BlockSpec(shape, lambda i: (0, 0), pipeline_mode=pl.Buffered(1))

    def mod():
        return pl.BlockSpec((1, 1, D_MODEL), lambda i: (i // bpm, 0, 0))

    return pl.pallas_call(
        _tail_kernel,
        grid=(m // tm,),
        in_specs=[pl.BlockSpec((tm, D_INNER), lambda i: (i, 0)),
                  pl.BlockSpec((tm, D_MODEL), lambda i: (i, 0)),
                  pl.BlockSpec((tm, D_MODEL), lambda i: (i, U_GATE // D_MODEL)),
                  pl.BlockSpec((tm, D_MODEL), lambda i: (i, U_GATE // D_MODEL + 1)),
                  pl.BlockSpec((tm, D_MODEL), lambda i: (i, 0)),
                  mod(), mod(), mod(), mod(),
                  const((D_INNER, D_MODEL)), const((D_MODEL, D_MODEL)), const((D_MODEL, D_MODEL)),
                  const((1, D_MODEL)),
                  const((D_MODEL, D_FF)), const((D_MODEL, D_FF)), const((D_FF, D_MODEL)),
                  const((1, D_MODEL))],
        out_specs=pl.BlockSpec((tm, D_MODEL), lambda i: (i, 0)),
        out_shape=jax.ShapeDtypeStruct((m, D_MODEL), F32),
        compiler_params=pltpu.CompilerParams(dimension_semantics=("arbitrary",),
                                             vmem_limit_bytes=VMEM_LIMIT),
        name="tail",
    )(y2d, f2d, u2d, u2d, x2d, g1, sh2, sc2, g2, wso, wfo, wo, n2, wfi, wfu, wfd, fg)


def _dt_lanes(a):
    lead = a.shape[:-1]
    a = a.reshape(lead + (2, N_GROUPS, HEADS_PER_GROUP))
    a = jnp.swapaxes(a, -3, -2).reshape(lead + (2 * N_HEADS,))
    return jnp.concatenate([a, jnp.zeros(lead + (LANE - 2 * N_HEADS,), a.dtype)], axis=-1)


def _block(x, c, ctx, c_ctx, w_ada, b_ada, norm1_g, w_in, conv_w, conv_b, dt_bias, a_log, d_skip,
           ssd_norm_g, w_ssd_out, w_fft_out, w_o, norm2_g, w_ffn_in, w_ffn_out, final_g,
           proj_tm, prep_tl, ssd_tl, fft_tm, tail_tm):
    b, l, d = x.shape
    ctx_len = ctx.shape[1]
    assert d == D_MODEL and l % (GRID_W * 2) == 0 and ctx_len % CHUNK == 0 and b + 1 <= SUBLANE

    w_in_t = w_in[0].T
    w_main_t = jnp.concatenate([w_in_t[:REF_DT], w_in_t[REF_Z:]], axis=0).astype(BF16)
    w_dt = _dt_lanes(w_in_t[REF_DT:REF_Z].T).T
    w_dt_hi = w_dt.astype(BF16)
    w_dt_lo = (w_dt - w_dt_hi.astype(F32)).astype(BF16)
    w_dt2 = jnp.concatenate([w_dt_hi, w_dt_lo], axis=0)
    conv_w_p = jnp.concatenate([conv_w[0], jnp.zeros((SUBLANE - D_CONV, CONV_DIM), F32)], axis=0)
    conv_b_p = conv_b[0].reshape(1, CONV_DIM)
    bias_v = _dt_lanes(dt_bias[0].reshape(-1)).reshape(1, LANE)
    alog_v = _dt_lanes(a_log[0].reshape(-1)).reshape(1, LANE)
    dskip_v = jnp.repeat(d_skip[0], HEAD_DIM).reshape(1, D_INNER)
    gain_v = ssd_norm_g[0].reshape(1, D_INNER)
    wso = w_ssd_out[0].astype(BF16)
    wfo = w_fft_out[0].astype(BF16)
    wo = w_o[0].astype(BF16)
    wfi = w_ffn_in[0][:, :D_FF].astype(BF16)
    wfu = w_ffn_in[0][:, D_FF:].astype(BF16)
    wfd = w_ffn_out[0].astype(BF16)

    cvecs = jnp.concatenate([c, c_ctx[None, :], jnp.zeros((SUBLANE - b - 1, D_MODEL), F32)], axis=0)
    mods = _mods(cvecs, b + 1, w_ada[0], b_ada[0])
    lat = mods[:b].reshape(b, 6, 1, D_MODEL)
    sh1, sc1, g1, sh2, sc2, g2 = (lat[:, k] for k in range(6))
    cm = mods[b].reshape(6, 1, 1, D_MODEL)

    x2d = x.reshape(b * l, D_MODEL)
    n1 = norm1_g[0].reshape(1, D_MODEL)
    u2d, dt_raw = _inproj(x2d, sh1, sc1, n1, w_main_t, w_dt2, U_WIDTH, proj_tm, l)
    u = u2d.reshape(b, l, U_WIDTH)
    cu2d, cdt_raw = _inproj(ctx.reshape(b * ctx_len, D_MODEL), cm[0], cm[1], n1, w_main_t, w_dt2,
                            CONV_DIM, b * ctx_len, b * ctx_len)
    cu = cu2d.reshape(b, ctx_len, CONV_DIM)

    rows = _dtprep(dt_raw.reshape(b, l, LANE), bias_v, alog_v, prep_tl)
    crows = _dtprep(cdt_raw.reshape(b, ctx_len, LANE), bias_v, alog_v, ctx_len)
    y_ssd = _ssd(u, rows, cu, crows, conv_w_p, conv_b_p, dskip_v, gain_v, ssd_tl)

    f_mix = _fourier(u, fft_tm)

    out = _tail(y_ssd.reshape(b * l, D_INNER), f_mix.reshape(b * l, D_MODEL), u2d, x2d,
                g1, sh2, sc2, g2, wso, wfo, wo, norm2_g[0].reshape(1, D_MODEL), wfi, wfu, wfd,
                final_g.reshape(1, D_MODEL), tail_tm, l)
    return out.reshape(b, l, D_MODEL)


def kernel(x, c, ctx, c_ctx, w_ada, b_ada, norm1_g, w_in, conv_w, conv_b, dt_bias, a_log, d_skip,
           ssd_norm_g, w_ssd_out, w_fft_out, w_o, norm2_g, w_ffn_in, w_ffn_out, final_g):
    return _block(x, c, ctx, c_ctx, w_ada, b_ada, norm1_g, w_in, conv_w, conv_b, dt_bias, a_log, d_skip,
                  ssd_norm_g, w_ssd_out, w_fft_out, w_o, norm2_g, w_ffn_in, w_ffn_out, final_g,
                  proj_tm=1024, prep_tl=1024, ssd_tl=2048, fft_tm=512, tail_tm=512)
```

```python
import functools
import math

import jax
import jax.numpy as jnp
import numpy as np
from jax import lax
from jax.experimental import pallas as pl
from jax.experimental.pallas import tpu as pltpu

F32 = jnp.float32
BF16 = jnp.bfloat16

D_MODEL = 1024
EPS = 1e-6
GRID_W = 64

N_GROUPS = 8
HEADS_PER_GROUP = 4
HEAD_DIM = 64
N_HEADS = N_GROUPS * HEADS_PER_GROUP
D_STATE = 128
D_INNER = N_HEADS * HEAD_DIM
GROUP_W = HEADS_PER_GROUP * HEAD_DIM
GS = N_GROUPS * D_STATE
CONV_DIM = D_INNER + 2 * GS
D_CONV = 5
CHUNK = 128

FFT_GROUPS = 8
FFT_GROUP_DIM = 128
D_FF = 2816

REF_DT = CONV_DIM
REF_Z = REF_DT + 2 * N_HEADS

U_B = D_INNER
U_C = U_B + GS
U_Z = CONV_DIM
U_FFT = U_Z + D_INNER
U_GATE = U_FFT + D_MODEL
U_WIDTH = U_GATE + 2 * D_MODEL

LANE = 128
SUBLANE = 8
HALO = 16
VMEM_LIMIT = 56 * 1024 * 1024

N_DH = 2 * HEADS_PER_GROUP
R_CUM, R_NCX, R_EY, R_WX, R_DIAG = 0, 8, 16, 24, 32
ROWS_PER_GROUP = 40
N_SLAB = (GROUP_W + 2 * D_STATE) // LANE
DT_FLOOR = 1e-37
LOG2E = math.log2(math.e)
N_CUM_TERMS = 3
K_ARGS = 2 * N_CUM_TERMS * N_DH
FFT_PITCH = GRID_W + 4


def _dot(a, b):
    return jnp.dot(a, b, preferred_element_type=F32)


def _dot_tn(a, b):
    return lax.dot_general(a, b, (((0,), (0,)), ((), ())), preferred_element_type=F32)


def _dot_nt(a, b):
    return lax.dot_general(a, b, (((1,), (1,)), ((), ())), preferred_element_type=F32)


def _rmsnorm(x, g):
    return x * lax.rsqrt(jnp.mean(x * x, axis=-1, keepdims=True) + EPS) * g


def _mods_kernel(ct_ref, w_ref, b_ref, o_ref, *, n_vecs):
    s = ct_ref[...]
    s = s * jax.nn.sigmoid(s)
    w = w_ref[...]
    rows = [jnp.sum(w * s[:, r:r + 1], axis=0, keepdims=True) for r in range(n_vecs)]
    rows.append(jnp.zeros((SUBLANE - n_vecs, w.shape[1]), F32))
    o_ref[...] = jnp.concatenate(rows, axis=0) + b_ref[...]


def _mods(cvecs, n_vecs, w_ada, b_ada):
    n = w_ada.shape[1]
    tn = 1536
    return pl.pallas_call(
        functools.partial(_mods_kernel, n_vecs=n_vecs),
        grid=(n // tn,),
        in_specs=[pl.BlockSpec((D_MODEL, SUBLANE), lambda j: (0, 0)),
                  pl.BlockSpec((D_MODEL, tn), lambda j: (0, j)),
                  pl.BlockSpec((1, tn), lambda j: (0, j))],
        out_specs=pl.BlockSpec((SUBLANE, tn), lambda j: (0, j)),
        out_shape=jax.ShapeDtypeStruct((SUBLANE, n), F32),
        compiler_params=pltpu.CompilerParams(dimension_semantics=("arbitrary",),
                                             vmem_limit_bytes=VMEM_LIMIT),
        name="mods",
    )(cvecs.T, w_ada, b_ada.reshape(1, n))


def _inproj_kernel(x_ref, sh_ref, sc_ref, g_ref, w_ref, wdt_ref, u_ref, dt_ref, h_ref):
    @pl.when(pl.program_id(1) == 0)
    def _():
        h = _rmsnorm(x_ref[...], g_ref[...]) * (1.0 + sc_ref[0]) + sh_ref[0]
        hb = h.astype(BF16)
        h_ref[...] = hb
        hl = (h - hb.astype(F32)).astype(BF16)
        a = _dot_nt(hb, wdt_ref[...])
        dt_ref[...] = a[:, :LANE] + a[:, LANE:] + _dot_nt(hl, wdt_ref[:LANE, :])

    u_ref[...] = _dot_nt(h_ref[...], w_ref[...]).astype(BF16)


def _inproj(x2d, shift, scale, gain, w_t, w_dt_t, n_cols, tm, rows_per_mod):
    m = x2d.shape[0]
    tn = 1024
    bpm = rows_per_mod // tm

    def w_rows(i, j):
        return pl.multiple_of(j * tn + jnp.where(j * tn >= REF_DT, REF_Z - REF_DT, 0), 2 * HALO), 0

    return pl.pallas_call(
        _inproj_kernel,
        grid=(m // tm, n_cols // tn),
        in_specs=[pl.BlockSpec((tm, D_MODEL), lambda i, j: (i, 0)),
                  pl.BlockSpec((1, 1, D_MODEL), lambda i, j: (i // bpm, 0, 0)),
                  pl.BlockSpec((1, 1, D_MODEL), lambda i, j: (i // bpm, 0, 0)),
                  pl.BlockSpec((1, D_MODEL), lambda i, j: (0, 0)),
                  pl.BlockSpec((pl.Element(tn), pl.Element(D_MODEL)), w_rows),
                  pl.BlockSpec((2 * LANE, D_MODEL), lambda i, j: (0, 0))],
        out_specs=[pl.BlockSpec((tm, tn), lambda i, j: (i, j)),
                   pl.BlockSpec((tm, LANE), lambda i, j: (i, 0))],
        out_shape=[jax.ShapeDtypeStruct((m, n_cols), BF16),
                   jax.ShapeDtypeStruct((m, LANE), F32)],
        scratch_shapes=[pltpu.VMEM((tm, D_MODEL), BF16)],
        compiler_params=pltpu.CompilerParams(dimension_semantics=("arbitrary", "arbitrary"),
                                             vmem_limit_bytes=VMEM_LIMIT),
        name="inproj",
    )(x2d, shift, scale, gain, w_t, w_dt_t)


def _dtprep_kernel(raw_ref, bias_ref, alog_ref, row_ref, *, n_chunks):
    a = -jnp.exp(alog_ref[...])
    bias = bias_ref[...]
    ii = lax.broadcasted_iota(jnp.int32, (CHUNK, CHUNK), 0)
    jj = lax.broadcasted_iota(jnp.int32, (CHUNK, CHUNK), 1)
    tril = (jj <= ii).astype(F32)
    triu = (jj >= ii).astype(F32)
    lane = lax.broadcasted_iota(jnp.int32, (1, LANE), 1)
    is_fwd = (lane % N_DH) < HEADS_PER_GROUP
    for c in range(n_chunks):
        sl = slice(c * CHUNK, (c + 1) * CHUNK)
        dt = jax.nn.softplus(raw_ref[0, sl, :] + bias)
        dta = dt * a
        cf = jnp.dot(tril, dta, precision=lax.Precision.HIGHEST, preferred_element_type=F32)
        cb = jnp.dot(triu, dta, precision=lax.Precision.HIGHEST, preferred_element_type=F32)
        cum = jnp.where(is_fwd, cf, cb)
        tot = jnp.where(is_fwd, cf[CHUNK - 1:CHUNK, :], cb[0:1, :])
        wx = dt * jnp.exp(tot - cum)
        ey = jnp.exp(cum)
        ncx = jnp.log(jnp.maximum(dt, DT_FLOOR)) - cum
        diag = jnp.log(dt + pltpu.roll(dt, LANE - HEADS_PER_GROUP, 1))
        for r_off, val in ((R_CUM, cum * LOG2E), (R_NCX, ncx * LOG2E), (R_EY, ey), (R_WX, wx),
                           (R_DIAG, diag * LOG2E)):
            vt = val.T
            for g in range(N_GROUPS):
                r0 = g * ROWS_PER_GROUP + r_off
                row_ref[0, r0:r0 + N_DH, sl] = vt[g * N_DH:(g + 1) * N_DH, :]


def _dtprep(raw, bias, alog, tl):
    b, l, _ = raw.shape
    return pl.pallas_call(
        functools.partial(_dtprep_kernel, n_chunks=tl // CHUNK),
        grid=(b, l // tl),
        in_specs=[pl.BlockSpec((1, tl, LANE), lambda i, j: (i, j, 0)),
                  pl.BlockSpec((1, LANE), lambda i, j: (0, 0)),
                  pl.BlockSpec((1, LANE), lambda i, j: (0, 0))],
        out_specs=pl.BlockSpec((1, N_GROUPS * ROWS_PER_GROUP, tl), lambda i, j: (i, 0, j)),
        out_shape=jax.ShapeDtypeStruct((b, N_GROUPS * ROWS_PER_GROUP, l), F32),
        compiler_params=pltpu.CompilerParams(dimension_semantics=("arbitrary", "arbitrary"),
                                             vmem_limit_bytes=VMEM_LIMIT),
        name="dtprep",
    )(raw, bias, alog)


def _split_terms(x, n):
    terms = []
    for _ in range(n - 1):
        t = x.astype(BF16).astype(F32)
        terms.append(t)
        x = x - t
    terms.append(x)
    return terms


def _scale_rows(rows):
    return jnp.concatenate(_split_terms(rows[R_EY:R_EY + N_DH], 2) + _split_terms(rows[R_WX:R_WX + N_DH], 2),
                           axis=0).astype(BF16)


def _conv_silu(ext_ref, s, r0, cw, cbias):
    sl = slice(s * LANE, (s + 1) * LANE)
    acc = cbias[:, sl]
    for k in range(D_CONV):
        acc = acc + cw[k:k + 1, sl] * ext_ref[s, pl.ds(r0 + k, CHUNK), :]
    return acc * jax.nn.sigmoid(acc)


def _ssd_kernel(x_ref, b_ref, c_ref, xp_ref, bp_ref, cp_ref, xn_ref, bn_ref, cn_ref, z_ref, row_ref,
                cx_ref, cb_ref, crow_ref, cwx_ref, cwb_ref, cwc_ref, cbx_ref, cbb_ref, cbc_ref,
                dskip_ref, gain_ref, onesel_ref, sel_ref, out_ref,
                cache_ref, yacc_ref, sf_ref, sb_ref, ext_ref, args0_ref, args1_ref, ew0_ref, ew1_ref,
                *, tl, n_blk, ctx_len):
    p = pl.program_id(2)
    i = pl.program_id(3)
    n_ch = tl // CHUNK
    args_bufs = (args0_ref, args1_ref)
    ew_bufs = (ew0_ref, ew1_ref)
    cw = jnp.concatenate([cwx_ref[...], cwb_ref[...], cwc_ref[...]], axis=1)
    cbias = jnp.concatenate([cbx_ref[...], cbb_ref[...], cbc_ref[...]], axis=1)
    tap0 = HALO - D_CONV // 2

    def slabs_of(x, bm, cm):
        return [x[:, :LANE], x[:, LANE:], bm, cm]

    def fill_ext(src, first, last, n_rows):
        for s in range(len(src)):
            ext_ref[s, 0:HALO, :] = first[s]
            ext_ref[s, HALO:HALO + n_rows, :] = src[s].astype(F32)
            ext_ref[s, HALO + n_rows:2 * HALO + n_rows, :] = last[s]

    def row_of(c):
        return c * CHUNK if isinstance(c, int) else pl.multiple_of(c * CHUNK, CHUNK)

    def expand(rowt, d):
        return _dot_tn(_scale_rows(rowt), sel_ref[d])

    def run_pipelined(stage_a, stage_b, order):
        stage_a(order(0), 0)

        def two_chunks(t, carry):
            k = 2 * t
            stage_b(order(k), 0)
            stage_a(order(k + 1), 1)
            stage_b(order(k + 1), 1)
            stage_a(order(k + 2), 0)
            return carry

        lax.fori_loop(0, n_ch // 2 - 1, two_chunks, 0)
        stage_b(order(n_ch - 2), 0)
        stage_a(order(n_ch - 1), 1)
        stage_b(order(n_ch - 1), 1)

    @pl.when(jnp.logical_and(p == 0, i == 0))
    def _init_from_context():
        zpad = [jnp.zeros((HALO, LANE), F32)] * (N_SLAB - 1)
        cx = cx_ref[0]
        fill_ext([cx[:, :LANE], cx[:, LANE:], cb_ref[0]], zpad, zpad, ctx_len)
        n_cc = ctx_len // CHUNK
        sf = jnp.zeros((D_STATE, GROUP_W), F32)
        st_b, ey_b = [], []
        for cc in range(n_cc):
            slabs = [_conv_silu(ext_ref, s, cc * CHUNK + tap0, cw, cbias) for s in range(N_SLAB - 1)]
            xs = jnp.concatenate(slabs[:2], axis=1)
            bm = slabs[2].astype(BF16)
            rowt = crow_ref[0, :, cc * CHUNK:(cc + 1) * CHUNK]
            ewf = expand(rowt, 0)
            ewb = expand(rowt, 1)
            sf = sf * ewf[CHUNK - 1:CHUNK, :GROUP_W] + _dot_tn(bm, (xs * ewf[:, GROUP_W:]).astype(BF16))
            st_b.append(_dot_tn(bm, (xs * ewb[:, GROUP_W:]).astype(BF16)))
            ey_b.append(ewb[0:1, :GROUP_W])
        sb = jnp.zeros((D_STATE, GROUP_W), F32)
        for cc in reversed(range(n_cc)):
            sb = sb * ey_b[cc] + st_b[cc]
        sf_ref[...] = sf
        sb_ref[...] = sb

    @pl.when(p == 0)
    def _forward():
        blk = i
        base = blk * tl
        has_prev = blk > 0
        has_next = blk < n_blk - 1
        fill_ext(slabs_of(x_ref[0], b_ref[0], c_ref[0]),
                 [jnp.where(has_prev, v.astype(F32), 0.0) for v in slabs_of(xp_ref[0], bp_ref[0], cp_ref[0])],
                 [jnp.where(has_next, v.astype(F32), 0.0) for v in slabs_of(xn_ref[0], bn_ref[0], cn_ref[0])],
                 tl)
        ii = lax.broadcasted_iota(jnp.int32, (CHUNK, CHUNK), 0)
        jj = lax.broadcasted_iota(jnp.int32, (CHUNK, CHUNK), 1)
        lower = jj < ii
        upper = jj > ii

        def prepare(c, buf):
            r = row_of(c)
            slabs = [_conv_silu(ext_ref, s, r + tap0, cw, cbias) for s in range(N_SLAB)]
            for s in range(N_SLAB):
                cache_ref[pl.ds(base + r, CHUNK), s * LANE:(s + 1) * LANE] = slabs[s].astype(BF16)
            yacc_ref[pl.ds(base + r, CHUNK), :] = jnp.concatenate(slabs[:2], axis=1) * dskip_ref[...]
            rowt = row_ref[0, :, pl.ds(r, CHUNK)]
            onesel = onesel_ref[...]
            ones = jnp.ones((N_DH, CHUNK), F32)
            at = jnp.concatenate(_split_terms(rowt[R_CUM:R_CUM + N_DH], N_CUM_TERMS) + [ones] * N_CUM_TERMS,
                                 axis=0).astype(BF16)
            wide = [jnp.concatenate([t] * N_DH, axis=1) * onesel
                    for t in _split_terms(rowt[R_NCX:R_NCX + N_DH], N_CUM_TERMS)]
            bmat = jnp.concatenate([onesel] * N_CUM_TERMS + wide, axis=0).astype(BF16)
            args_bufs[buf][...] = _dot_tn(at, bmat)
            ew_bufs[buf][...] = expand(rowt, 0)

        def consume(c, buf):
            r = row_of(c)
            xbc_b = cache_ref[pl.ds(base + r, CHUNK), :]
            xs_b = xbc_b[:, :GROUP_W]
            bm = xbc_b[:, GROUP_W:GROUP_W + D_STATE]
            cm = xbc_b[:, GROUP_W + D_STATE:]
            cbm = _dot_nt(cm, bm)
            yd = []
            for j in range(HEADS_PER_GROUP):
                argf = args_bufs[buf][:, j * CHUNK:(j + 1) * CHUNK]
                argb = args_bufs[buf][:, (HEADS_PER_GROUP + j) * CHUNK:(HEADS_PER_GROUP + j + 1) * CHUNK]
                diag = row_ref[0, R_DIAG + j:R_DIAG + j + 1, pl.ds(r, CHUNK)]
                arg = jnp.where(lower, argf, jnp.where(upper, argb, diag))
                wmat = (cbm * jnp.exp2(arg)).astype(BF16)
                yd.append(_dot(wmat, xs_b[:, j * HEAD_DIM:(j + 1) * HEAD_DIM]))
            ef = ew_bufs[buf][:, :GROUP_W]
            wxf = ew_bufs[buf][:, GROUP_W:]
            sf = sf_ref[...]
            yacc_ref[pl.ds(base + r, CHUNK), :] += jnp.concatenate(yd, axis=1) + _dot(cm, sf.astype(BF16)) * ef
            sf_ref[...] = sf * ef[CHUNK - 1:CHUNK, :] + _dot_tn(bm, (xs_b.astype(F32) * wxf).astype(BF16))

        run_pipelined(prepare, consume, lambda k: k)

    @pl.when(p == 1)
    def _backward():
        blk = n_blk - 1 - i
        base = blk * tl

        def prepare(c, buf):
            ew_bufs[buf][...] = expand(row_ref[0, :, pl.ds(row_of(c), CHUNK)], 1)

        def consume(c, buf):
            r = row_of(c)
            xbc_b = cache_ref[pl.ds(base + r, CHUNK), :]
            xs = xbc_b[:, :GROUP_W].astype(F32)
            bm = xbc_b[:, GROUP_W:GROUP_W + D_STATE]
            cm = xbc_b[:, GROUP_W + D_STATE:]
            eb = ew_bufs[buf][:, :GROUP_W]
            wxb = ew_bufs[buf][:, GROUP_W:]
            sb = sb_ref[...]
            y = yacc_ref[pl.ds(base + r, CHUNK), :] + _dot(cm, sb.astype(BF16)) * eb
            sb_ref[...] = sb * eb[0:1, :] + _dot_tn(bm, (xs * wxb).astype(BF16))
            z = z_ref[0, pl.ds(r, CHUNK), :].astype(F32)
            y = y * (z * jax.nn.sigmoid(z))
            out_ref[0, pl.ds(r, CHUNK), :] = _rmsnorm(y, gain_ref[...]).astype(BF16)

        run_pipelined(prepare, consume, lambda k: n_ch - 1 - k)


def _ssd_selectors():
    onesel = (np.arange(N_DH * CHUNK)[None, :] // CHUNK == np.arange(N_DH)[:, None]).astype(np.float32)
    sel = np.zeros((2, 4 * N_DH, 2 * GROUP_W), np.float32)
    for d in range(2):
        for j in range(HEADS_PER_GROUP):
            dh = d * HEADS_PER_GROUP + j
            for term in range(2):
                sel[d, term * N_DH + dh, j * HEAD_DIM:(j + 1) * HEAD_DIM] = 1.0
                sel[d, (2 + term) * N_DH + dh, GROUP_W + j * HEAD_DIM:GROUP_W + (j + 1) * HEAD_DIM] = 1.0
    return jnp.asarray(onesel, F32), jnp.asarray(sel, BF16)


def _ssd(u, rows, cu, crows, conv_w, conv_b, dskip, gain, tl):
    b, l, _ = u.shape
    ctx_len = cu.shape[1]
    assert ctx_len <= tl
    n_blk = l // tl
    hb = tl // HALO
    last_halo = l // HALO - 1
    assert (tl // CHUNK) % 2 == 0
    onesel, sel = _ssd_selectors()
    xg, bg, cg = 0, U_B // D_STATE, U_C // D_STATE

    def in_blk(p, i):
        return i * (1 - p) + (n_blk - 1) * p

    def seq_blk(p, i):
        return i * (1 - p) + (n_blk - 1 - i) * p

    def out_blk(p, i):
        return (n_blk - 1) * (1 - p) + (n_blk - 1 - i) * p

    def prev_halo(p, i):
        return jnp.maximum(in_blk(p, i) * hb - 1, 0)

    def next_halo(p, i):
        return jnp.minimum((in_blk(p, i) + 1) * hb, last_halo)

    def xbc_specs(rows_blk, row_index):
        return [pl.BlockSpec((1, rows_blk, GROUP_W), lambda bb, g, p, i: (bb, row_index(p, i), xg + g)),
                pl.BlockSpec((1, rows_blk, D_STATE), lambda bb, g, p, i: (bb, row_index(p, i), bg + g)),
                pl.BlockSpec((1, rows_blk, D_STATE), lambda bb, g, p, i: (bb, row_index(p, i), cg + g))]

    def param_specs(n_rows):
        return [pl.BlockSpec((n_rows, GROUP_W), lambda bb, g, p, i: (0, xg + g)),
                pl.BlockSpec((n_rows, D_STATE), lambda bb, g, p, i: (0, bg + g)),
                pl.BlockSpec((n_rows, D_STATE), lambda bb, g, p, i: (0, cg + g))]

    grid = (b, N_GROUPS, 2, n_blk)
    in_specs = (
        xbc_specs(tl, in_blk) + xbc_specs(HALO, prev_halo) + xbc_specs(HALO, next_halo)
        + [pl.BlockSpec((1, tl, GROUP_W), lambda bb, g, p, i: (bb, out_blk(p, i), U_Z // GROUP_W + g)),
           pl.BlockSpec((1, ROWS_PER_GROUP, tl), lambda bb, g, p, i: (bb, g, seq_blk(p, i)))]
        + xbc_specs(ctx_len, lambda p, i: 0)[:2]
        + [pl.BlockSpec((1, ROWS_PER_GROUP, ctx_len), lambda bb, g, p, i: (bb, g, 0))]
        + param_specs(SUBLANE) + param_specs(1)
        + [pl.BlockSpec((1, GROUP_W), lambda bb, g, p, i: (0, g)),
           pl.BlockSpec((1, GROUP_W), lambda bb, g, p, i: (0, g)),
           pl.BlockSpec(onesel.shape, lambda bb, g, p, i: (0, 0)),
           pl.BlockSpec(sel.shape, lambda bb, g, p, i: (0, 0, 0))])
    return pl.pallas_call(
        functools.partial(_ssd_kernel, tl=tl, n_blk=n_blk, ctx_len=ctx_len),
        grid=grid,
        in_specs=in_specs,
        out_specs=pl.BlockSpec((1, tl, GROUP_W), lambda bb, g, p, i: (bb, out_blk(p, i), g)),
        out_shape=jax.ShapeDtypeStruct((b, l, D_INNER), BF16),
        scratch_shapes=[pltpu.VMEM((l, N_SLAB * LANE), BF16),
                        pltpu.VMEM((l, GROUP_W), F32),
                        pltpu.VMEM((D_STATE, GROUP_W), F32),
                        pltpu.VMEM((D_STATE, GROUP_W), F32),
                        pltpu.VMEM((N_SLAB, tl + 2 * HALO, LANE), F32),
                        pltpu.VMEM((CHUNK, N_DH * CHUNK), F32),
                        pltpu.VMEM((CHUNK, N_DH * CHUNK), F32),
                        pltpu.VMEM((CHUNK, 2 * GROUP_W), F32),
                        pltpu.VMEM((CHUNK, 2 * GROUP_W), F32)],
        compiler_params=pltpu.CompilerParams(
            dimension_semantics=("arbitrary", "arbitrary", "arbitrary", "arbitrary"),
            vmem_limit_bytes=VMEM_LIMIT),
        name="ssd",
    )(u, u, u, u, u, u, u, u, u, u, rows, cu, cu, crows,
      conv_w, conv_w, conv_w, conv_b, conv_b, conv_b, dskip, gain, onesel, sel)


def _dft_mats():
    def cs(n):
        k = np.arange(n)
        ang = 2.0 * np.pi * np.outer(k, k) / n
        return np.cos(ang), np.sin(ang)

    c3, s3 = cs(FFT_GROUP_DIM)
    w1 = np.concatenate([c3, -s3], axis=1)
    c2, s2 = cs(GRID_W)
    m2 = np.block([[c2, s2], [-s2, c2]])
    return w1, m2


def _fft12_kernel(f_ref, w1_ref, m2_ref, o_ref, *, tm):
    f = f_ref[0]
    w1 = w1_ref[...]
    ps, qs = [], []
    for g in range(FFT_GROUPS):
        pq = _dot(f[:, g * FFT_GROUP_DIM:(g + 1) * FFT_GROUP_DIM], w1)
        ps.append(pq[:, :FFT_GROUP_DIM])
        qs.append(pq[:, FFT_GROUP_DIM:])
    pr = jnp.concatenate(ps, axis=1).astype(BF16)
    qr = jnp.concatenate(qs, axis=1).astype(BF16)
    m2 = m2_ref[...]
    pad = jnp.zeros((FFT_PITCH - GRID_W, D_MODEL), F32)
    for r in range(tm // GRID_W):
        sl = slice(r * GRID_W, (r + 1) * GRID_W)
        o = _dot(m2, jnp.concatenate([pr[sl], qr[sl]], axis=0))
        r0 = r * FFT_PITCH
        for part in range(2):
            o_ref[0, part, r0:r0 + GRID_W, :] = o[part * GRID_W:(part + 1) * GRID_W]
            o_ref[0, part, r0 + GRID_W:r0 + FFT_PITCH, :] = pad


def _fft3_kernel(ri_ref, m3_ref, o_ref, scr_ref, *, rows):
    m3 = m3_ref[...]

    def column_pair(t, carry):
        w = 2 * t
        cols = [jnp.concatenate([ri_ref[0, part, pl.ds(w + dw, rows, stride=FFT_PITCH), :] for part in range(2)],
                                axis=0) for dw in range(2)]
        o = _dot(m3, jnp.concatenate(cols, axis=1).astype(BF16))
        for dw in range(2):
            scr_ref[pl.ds(w + dw, rows, stride=FFT_PITCH), :] = o[:, dw * LANE:(dw + 1) * LANE]
        return carry

    lax.fori_loop(0, GRID_W // 2, column_pair, 0, unroll=4)
    for k in range(rows):
        o_ref[0, k * GRID_W:(k + 1) * GRID_W, :] = scr_ref[k * FFT_PITCH:k * FFT_PITCH + GRID_W, :].astype(BF16)


def _fourier(u, tm):
    b, l, _ = u.shape
    rows = l // GRID_W
    w1, m2 = _dft_mats()
    ri = pl.pallas_call(
        functools.partial(_fft12_kernel, tm=tm),
        grid=(b, l // tm),
        in_specs=[pl.BlockSpec((1, tm, D_MODEL), lambda i, j: (i, j, U_FFT // D_MODEL)),
                  pl.BlockSpec((FFT_GROUP_DIM, 2 * FFT_GROUP_DIM), lambda i, j: (0, 0)),
                  pl.BlockSpec((2 * GRID_W, 2 * GRID_W), lambda i, j: (0, 0))],
        out_specs=pl.BlockSpec((1, 2, tm // GRID_W * FFT_PITCH, D_MODEL), lambda i, j: (i, 0, j, 0)),
        out_shape=jax.ShapeDtypeStruct((b, 2, rows * FFT_PITCH, D_MODEL), F32),
        compiler_params=pltpu.CompilerParams(dimension_semantics=("arbitrary", "arbitrary"),
                                             vmem_limit_bytes=VMEM_LIMIT),
        name="fft12",
    )(u, jnp.asarray(w1, BF16), jnp.asarray(m2, BF16))

    k = np.arange(rows)
    ang = 2.0 * np.pi * np.outer(k, k) / rows
    scale = 1.0 / np.sqrt(float(rows * GRID_W * FFT_GROUP_DIM))
    m3 = np.concatenate([np.cos(ang), np.sin(ang)], axis=1) * scale
    return pl.pallas_call(
        functools.partial(_fft3_kernel, rows=rows),
        grid=(b, D_MODEL // LANE),
        in_specs=[pl.BlockSpec((1, 2, rows * FFT_PITCH, LANE), lambda i, j: (i, 0, 0, j)),
                  pl.BlockSpec((rows, 2 * rows), lambda i, j: (0, 0))],
        out_specs=pl.BlockSpec((1, l, LANE), lambda i, j: (i, 0, j)),
        out_shape=jax.ShapeDtypeStruct((b, l, D_MODEL), BF16),
        scratch_shapes=[pltpu.VMEM((rows * FFT_PITCH, LANE), F32)],
        compiler_params=pltpu.CompilerParams(dimension_semantics=("arbitrary", "arbitrary"),
                                             vmem_limit_bytes=VMEM_LIMIT),
        name="fft3",
    )(ri, jnp.asarray(m3, BF16))


FF_SPLIT = 2


def _tail_kernel(y_ref, f_ref, ga_ref, gb_ref, x_ref, g1_ref, sh2_ref, sc2_ref, g2_ref,
                 wso_ref, wfo_ref, wo_ref, n2_ref, wff_ref, wfd_ref, fg_ref, o_ref):
    a = _dot(y_ref[...], wso_ref[...])
    bq = _dot(f_ref[...], wfo_ref[...])
    merged = jax.nn.sigmoid(ga_ref[...].astype(F32)) * a + jax.nn.sigmoid(gb_ref[...].astype(F32)) * bq
    xn = x_ref[...] + g1_ref[0] * _dot(merged.astype(BF16), wo_ref[...])
    h = (_rmsnorm(xn, n2_ref[...]) * (1.0 + sc2_ref[0]) + sh2_ref[0]).astype(BF16)
    ffw = D_FF // FF_SPLIT
    acc = None
    for s in range(FF_SPLIT):
        gate = _dot(h, wff_ref[:, s * ffw:(s + 1) * ffw])
        up = _dot(h, wff_ref[:, D_FF + s * ffw:D_FF + (s + 1) * ffw])
        act = (gate * jax.nn.sigmoid(gate) * up).astype(BF16)
        part = _dot(act, wfd_ref[s * ffw:(s + 1) * ffw, :])
        acc = part if acc is None else acc + part
    o_ref[...] = _rmsnorm(xn + g2_ref[0] * acc, fg_ref[...])


def _tail(y2d, f2d, u2d, x2d, g1, sh2, sc2, g2, wso, wfo, wo, n2, wff, wfd, fg, tm, rows_per_mod):
    m = x2d.shape[0]
    bpm = rows_per_mod // tm

    def const(shape):
        return pl.BlockSpec(shape, lambda i: (0, 0), pipeline_mode=pl.Buffered(1))

    def mod():
        return pl.BlockSpec((1, 1, D_MODEL), lambda i: (i // bpm, 0, 0))

    return pl.pallas_call(
        _tail_kernel,
        grid=(m // tm,),
        in_specs=[pl.BlockSpec((tm, D_INNER), lambda i: (i, 0)),
                  pl.BlockSpec((tm, D_MODEL), lambda i: (i, 0)),
                  pl.BlockSpec((tm, D_MODEL), lambda i: (i, U_GATE // D_MODEL)),
                  pl.BlockSpec((tm, D_MODEL), lambda i: (i, U_GATE // D_MODEL + 1)),
                  pl.BlockSpec((tm, D_MODEL), lambda i: (i, 0)),
                  mod(), mod(), mod(), mod(),
                  const((D_INNER, D_MODEL)), const((D_MODEL, D_MODEL)), const((D_MODEL, D_MODEL)),
                  const((1, D_MODEL)),
                  const((D_MODEL, 2 * D_FF)), const((D_FF, D_MODEL)),
                  const((1, D_MODEL))],
        out_specs=pl.BlockSpec((tm, D_MODEL), lambda i: (i, 0)),
        out_shape=jax.ShapeDtypeStruct((m, D_MODEL), F32),
        compiler_params=pltpu.CompilerParams(dimension_semantics=("arbitrary",),
                                             vmem_limit_bytes=VMEM_LIMIT),
        name="tail",
    )(y2d, f2d, u2d, u2d, x2d, g1, sh2, sc2, g2, wso, wfo, wo, n2, wff, wfd, fg)


def _dt_lanes(a):
    lead = a.shape[:-1]
    a = a.reshape(lead + (2, N_GROUPS, HEADS_PER_GROUP))
    a = jnp.swapaxes(a, -3, -2).reshape(lead + (2 * N_HEADS,))
    return jnp.concatenate([a, jnp.zeros(lead + (LANE - 2 * N_HEADS,), a.dtype)], axis=-1)


def _block(x, c, ctx, c_ctx, w_ada, b_ada, norm1_g, w_in, conv_w, conv_b, dt_bias, a_log, d_skip,
           ssd_norm_g, w_ssd_out, w_fft_out, w_o, norm2_g, w_ffn_in, w_ffn_out, final_g,
           proj_tm, prep_tl, ssd_tl, fft_tm, tail_tm):
    b, l, d = x.shape
    ctx_len = ctx.shape[1]
    assert d == D_MODEL and l % (GRID_W * 2) == 0 and ctx_len % CHUNK == 0 and b + 1 <= SUBLANE

    w_in_t = w_in[0].T
    w_main_t = w_in_t.astype(BF16)
    w_dt = _dt_lanes(w_in_t[REF_DT:REF_Z].T).T
    w_dt_hi = w_dt.astype(BF16)
    w_dt_lo = (w_dt - w_dt_hi.astype(F32)).astype(BF16)
    w_dt2 = jnp.concatenate([w_dt_hi, w_dt_lo], axis=0)
    conv_w_p = jnp.concatenate([conv_w[0], jnp.zeros((SUBLANE - D_CONV, CONV_DIM), F32)], axis=0)
    conv_b_p = conv_b[0].reshape(1, CONV_DIM)
    bias_v = _dt_lanes(dt_bias[0].reshape(-1)).reshape(1, LANE)
    alog_v = _dt_lanes(a_log[0].reshape(-1)).reshape(1, LANE)
    dskip_v = jnp.repeat(d_skip[0], HEAD_DIM).reshape(1, D_INNER)
    gain_v = ssd_norm_g[0].reshape(1, D_INNER)
    wso = w_ssd_out[0].astype(BF16)
    wfo = w_fft_out[0].astype(BF16)
    wo = w_o[0].astype(BF16)
    wff = w_ffn_in[0].astype(BF16)
    wfd = w_ffn_out[0].astype(BF16)

    cvecs = jnp.concatenate([c, c_ctx[None, :], jnp.zeros((SUBLANE - b - 1, D_MODEL), F32)], axis=0)
    mods = _mods(cvecs, b + 1, w_ada[0], b_ada[0])
    lat = mods[:b].reshape(b, 6, 1, D_MODEL)
    sh1, sc1, g1, sh2, sc2, g2 = (lat[:, k] for k in range(6))
    cm = mods[b].reshape(6, 1, 1, D_MODEL)

    x2d = x.reshape(b * l, D_MODEL)
    n1 = norm1_g[0].reshape(1, D_MODEL)
    u2d, dt_raw = _inproj(x2d, sh1, sc1, n1, w_main_t, w_dt2, U_WIDTH, proj_tm, l)
    u = u2d.reshape(b, l, U_WIDTH)
    cu2d, cdt_raw = _inproj(ctx.reshape(b * ctx_len, D_MODEL), cm[0], cm[1], n1, w_main_t, w_dt2,
                            CONV_DIM, b * ctx_len, b * ctx_len)
    cu = cu2d.reshape(b, ctx_len, CONV_DIM)

    rows = _dtprep(dt_raw.reshape(b, l, LANE), bias_v, alog_v, prep_tl)
    crows = _dtprep(cdt_raw.reshape(b, ctx_len, LANE), bias_v, alog_v, ctx_len)
    y_ssd = _ssd(u, rows, cu, crows, conv_w_p, conv_b_p, dskip_v, gain_v, ssd_tl)

    f_mix = _fourier(u, fft_tm)

    out = _tail(y_ssd.reshape(b * l, D_INNER), f_mix.reshape(b * l, D_MODEL), u2d, x2d,
                g1, sh2, sc2, g2, wso, wfo, wo, norm2_g[0].reshape(1, D_MODEL), wff, wfd,
                final_g.reshape(1, D_MODEL), tail_tm, l)
    return out.reshape(b, l, D_MODEL)


def kernel(x, c, ctx, c_ctx, w_ada, b_ada, norm1_g, w_in, conv_w, conv_b, dt_bias, a_log, d_skip,
           ssd_norm_g, w_ssd_out, w_fft_out, w_o, norm2_g, w_ffn_in, w_ffn_out, final_g):
    return _block(x, c, ctx, c_ctx, w_ada, b_ada, norm1_g, w_in, conv_w, conv_b, dt_bias, a_log, d_skip,
                  ssd_norm_g, w_ssd_out, w_fft_out, w_o, norm2_g, w_ffn_in, w_ffn_out, final_g,
                  proj_tm=2048, prep_tl=1024, ssd_tl=4096, fft_tm=512, tail_tm=512)
```

```python
import functools
import math

import jax
import jax.numpy as jnp
import numpy as np
from jax import lax
from jax.experimental import pallas as pl
from jax.experimental.pallas import tpu as pltpu

F32 = jnp.float32
BF16 = jnp.bfloat16

D_MODEL = 1024
EPS = 1e-6
GRID_W = 64

N_GROUPS = 8
HEADS_PER_GROUP = 4
HEAD_DIM = 64
N_HEADS = N_GROUPS * HEADS_PER_GROUP
D_STATE = 128
D_INNER = N_HEADS * HEAD_DIM
GROUP_W = HEADS_PER_GROUP * HEAD_DIM
GS = N_GROUPS * D_STATE
CONV_DIM = D_INNER + 2 * GS
D_CONV = 5
CHUNK = 128

FFT_GROUPS = 8
FFT_GROUP_DIM = 128
D_FF = 2816

REF_DT = CONV_DIM
REF_Z = REF_DT + 2 * N_HEADS

U_B = D_INNER
U_C = U_B + GS
U_Z = CONV_DIM
U_FFT = U_Z + D_INNER
U_GATE = U_FFT + D_MODEL
U_WIDTH = U_GATE + 2 * D_MODEL

LANE = 128
SUBLANE = 8
HALO = 16
VMEM_LIMIT = 56 * 1024 * 1024

N_DH = 2 * HEADS_PER_GROUP
R_CUM, R_NCX, R_EY, R_WX, R_DIAG = 0, 8, 16, 24, 32
ROWS_PER_GROUP = 40
N_SLAB = (GROUP_W + 2 * D_STATE) // LANE
DT_FLOOR = 1e-37
LOG2E = math.log2(math.e)
N_CUM_TERMS = 3
K_ARGS = 2 * N_CUM_TERMS * N_DH
FFT_PITCH = GRID_W + 4


def _dot(a, b):
    return jnp.dot(a, b, preferred_element_type=F32)


def _dot_tn(a, b):
    return lax.dot_general(a, b, (((0,), (0,)), ((), ())), preferred_element_type=F32)


def _dot_nt(a, b):
    return lax.dot_general(a, b, (((1,), (1,)), ((), ())), preferred_element_type=F32)


def _rmsnorm(x, g):
    return x * lax.rsqrt(jnp.mean(x * x, axis=-1, keepdims=True) + EPS) * g


def _mods_kernel(ct_ref, w_ref, b_ref, o_ref, *, n_vecs):
    s = ct_ref[...]
    s = s * jax.nn.sigmoid(s)
    w = w_ref[...]
    rows = [jnp.sum(w * s[:, r:r + 1], axis=0, keepdims=True) for r in range(n_vecs)]
    rows.append(jnp.zeros((SUBLANE - n_vecs, w.shape[1]), F32))
    o_ref[...] = jnp.concatenate(rows, axis=0) + b_ref[...]


def _mods(cvecs, n_vecs, w_ada, b_ada):
    n = w_ada.shape[1]
    tn = 1536
    return pl.pallas_call(
        functools.partial(_mods_kernel, n_vecs=n_vecs),
        grid=(n // tn,),
        in_specs=[pl.BlockSpec((D_MODEL, SUBLANE), lambda j: (0, 0)),
                  pl.BlockSpec((D_MODEL, tn), lambda j: (0, j)),
                  pl.BlockSpec((1, tn), lambda j: (0, j))],
        out_specs=pl.BlockSpec((SUBLANE, tn), lambda j: (0, j)),
        out_shape=jax.ShapeDtypeStruct((SUBLANE, n), F32),
        compiler_params=pltpu.CompilerParams(dimension_semantics=("arbitrary",),
                                             vmem_limit_bytes=VMEM_LIMIT),
        name="mods",
    )(cvecs.T, w_ada, b_ada.reshape(1, n))


def _inproj_kernel(x_ref, sh_ref, sc_ref, g_ref, w_ref, wdt_ref, u_ref, dt_ref, h_ref):
    @pl.when(pl.program_id(1) == 0)
    def _():
        h = _rmsnorm(x_ref[...], g_ref[...]) * (1.0 + sc_ref[0]) + sh_ref[0]
        hb = h.astype(BF16)
        h_ref[...] = hb
        hl = (h - hb.astype(F32)).astype(BF16)
        a = _dot_nt(hb, wdt_ref[...])
        dt_ref[...] = a[:, :LANE] + a[:, LANE:] + _dot_nt(hl, wdt_ref[:LANE, :])

    u_ref[...] = _dot_nt(h_ref[...], w_ref[...]).astype(BF16)


def _inproj(x2d, shift, scale, gain, w_t, w_dt_t, n_cols, tm, rows_per_mod):
    m = x2d.shape[0]
    tn = 1024
    bpm = rows_per_mod // tm

    def w_rows(i, j):
        return pl.multiple_of(j * tn + jnp.where(j * tn >= REF_DT, REF_Z - REF_DT, 0), 2 * HALO), 0

    return pl.pallas_call(
        _inproj_kernel,
        grid=(m // tm, n_cols // tn),
        in_specs=[pl.BlockSpec((tm, D_MODEL), lambda i, j: (i, 0)),
                  pl.BlockSpec((1, 1, D_MODEL), lambda i, j: (i // bpm, 0, 0)),
                  pl.BlockSpec((1, 1, D_MODEL), lambda i, j: (i // bpm, 0, 0)),
                  pl.BlockSpec((1, D_MODEL), lambda i, j: (0, 0)),
                  pl.BlockSpec((pl.Element(tn), pl.Element(D_MODEL)), w_rows),
                  pl.BlockSpec((2 * LANE, D_MODEL), lambda i, j: (0, 0))],
        out_specs=[pl.BlockSpec((tm, tn), lambda i, j: (i, j)),
                   pl.BlockSpec((tm, LANE), lambda i, j: (i, 0))],
        out_shape=[jax.ShapeDtypeStruct((m, n_cols), BF16),
                   jax.ShapeDtypeStruct((m, LANE), F32)],
        scratch_shapes=[pltpu.VMEM((tm, D_MODEL), BF16)],
        compiler_params=pltpu.CompilerParams(dimension_semantics=("arbitrary", "arbitrary"),
                                             vmem_limit_bytes=VMEM_LIMIT),
        name="inproj",
    )(x2d, shift, scale, gain, w_t, w_dt_t)


def _dtprep_kernel(raw_ref, bias_ref, alog_ref, row_ref, *, n_chunks):
    a = -jnp.exp(alog_ref[...])
    bias = bias_ref[...]
    ii = lax.broadcasted_iota(jnp.int32, (CHUNK, CHUNK), 0)
    jj = lax.broadcasted_iota(jnp.int32, (CHUNK, CHUNK), 1)
    tril = (jj <= ii).astype(F32)
    triu = (jj >= ii).astype(F32)
    lane = lax.broadcasted_iota(jnp.int32, (1, LANE), 1)
    is_fwd = (lane % N_DH) < HEADS_PER_GROUP
    for c in range(n_chunks):
        sl = slice(c * CHUNK, (c + 1) * CHUNK)
        dt = jax.nn.softplus(raw_ref[0, sl, :] + bias)
        dta = dt * a
        cf = jnp.dot(tril, dta, precision=lax.Precision.HIGHEST, preferred_element_type=F32)
        cb = jnp.dot(triu, dta, precision=lax.Precision.HIGHEST, preferred_element_type=F32)
        cum = jnp.where(is_fwd, cf, cb)
        tot = jnp.where(is_fwd, cf[CHUNK - 1:CHUNK, :], cb[0:1, :])
        wx = dt * jnp.exp(tot - cum)
        ey = jnp.exp(cum)
        ncx = jnp.log(jnp.maximum(dt, DT_FLOOR)) - cum
        diag = jnp.log(dt + pltpu.roll(dt, LANE - HEADS_PER_GROUP, 1))
        for r_off, val in ((R_CUM, cum * LOG2E), (R_NCX, ncx * LOG2E), (R_EY, ey), (R_WX, wx),
                           (R_DIAG, diag * LOG2E)):
            vt = val.T
            for g in range(N_GROUPS):
                r0 = g * ROWS_PER_GROUP + r_off
                row_ref[0, r0:r0 + N_DH, sl] = vt[g * N_DH:(g + 1) * N_DH, :]


def _dtprep(raw, bias, alog, tl):
    b, l, _ = raw.shape
    return pl.pallas_call(
        functools.partial(_dtprep_kernel, n_chunks=tl // CHUNK),
        grid=(b, l // tl),
        in_specs=[pl.BlockSpec((1, tl, LANE), lambda i, j: (i, j, 0)),
                  pl.BlockSpec((1, LANE), lambda i, j: (0, 0)),
                  pl.BlockSpec((1, LANE), lambda i, j: (0, 0))],
        out_specs=pl.BlockSpec((1, N_GROUPS * ROWS_PER_GROUP, tl), lambda i, j: (i, 0, j)),
        out_shape=jax.ShapeDtypeStruct((b, N_GROUPS * ROWS_PER_GROUP, l), F32),
        compiler_params=pltpu.CompilerParams(dimension_semantics=("arbitrary", "arbitrary"),
                                             vmem_limit_bytes=VMEM_LIMIT),
        name="dtprep",
    )(raw, bias, alog)


def _split_terms(x, n):
    terms = []
    for _ in range(n - 1):
        t = x.astype(BF16).astype(F32)
        terms.append(t)
        x = x - t
    terms.append(x)
    return terms


def _scale_rows(rows):
    return jnp.concatenate(_split_terms(rows[R_EY:R_EY + N_DH], 2) + _split_terms(rows[R_WX:R_WX + N_DH], 2),
                           axis=0).astype(BF16)


def _conv_silu(ext_refs, s, p0, cw, cbias):
    sl = slice(s * LANE, (s + 1) * LANE)
    acc = cbias[:, sl].astype(BF16)
    for k in range(D_CONV):
        pair = p0 + (HALO + k - D_CONV // 2 - k % 2) // 2
        tap = pltpu.bitcast(ext_refs[k % 2][s, pl.ds(pair, CHUNK // 2), :], BF16)
        acc = acc + cw[k:k + 1, sl].astype(BF16) * tap
    return acc * jax.nn.sigmoid(acc)


def _ssd_kernel(x_ref, b_ref, c_ref, xp_ref, bp_ref, cp_ref, xn_ref, bn_ref, cn_ref, z_ref, row_ref,
                cx_ref, cb_ref, crow_ref, cwx_ref, cwb_ref, cwc_ref, cbx_ref, cbb_ref, cbc_ref,
                dskip_ref, gain_ref, onesel_ref, sel_ref, out_ref,
                cache_ref, yacc_ref, sf_ref, sb_ref, ext0_ref, ext1_ref, args0_ref, args1_ref, ew0_ref, ew1_ref,
                *, tl, n_blk, ctx_len):
    p = pl.program_id(2)
    i = pl.program_id(3)
    n_ch = tl // CHUNK
    args_bufs = (args0_ref, args1_ref)
    ew_bufs = (ew0_ref, ew1_ref)
    ext_refs = (ext0_ref, ext1_ref)
    cw = jnp.concatenate([cwx_ref[...], cwb_ref[...], cwc_ref[...]], axis=1)
    cbias = jnp.concatenate([cbx_ref[...], cbb_ref[...], cbc_ref[...]], axis=1)

    def slabs_of(x, bm, cm):
        return [x[:, :LANE], x[:, LANE:], bm, cm]

    def slab_refs(xr, br, cr):
        return [(xr, 0), (xr, LANE), (br, 0), (cr, 0)]

    def fill_ext(src, first, last, n_rows):
        half = HALO // 2
        n_piece = n_rows // CHUNK

        def copy_piece(k, carry):
            r = pl.multiple_of(k * CHUNK, CHUNK)
            q = pl.multiple_of(k * (CHUNK // 2), CHUNK // 2)
            for s, (ref, lane0) in enumerate(src):
                ext0_ref[s, pl.ds(half + q, CHUNK // 2), :] = pltpu.bitcast(
                    ref[0, pl.ds(r, CHUNK), lane0:lane0 + LANE], jnp.uint32)
            return carry

        def shift_piece(k, carry):
            q = pl.multiple_of(k * (CHUNK // 2), CHUNK // 2)
            for s in range(len(src)):
                lo = ext0_ref[s, pl.ds(q, CHUNK // 2), :]
                hi = ext0_ref[s, pl.ds(q + 1, CHUNK // 2), :]
                ext1_ref[s, pl.ds(q, CHUNK // 2), :] = (lo >> 16) | (hi << 16)
            return carry

        for s in range(len(src)):
            ext0_ref[s, 0:half, :] = pltpu.bitcast(first[s], jnp.uint32)
            ext0_ref[s, half + n_rows // 2:2 * half + n_rows // 2, :] = pltpu.bitcast(last[s], jnp.uint32)
        lax.fori_loop(0, n_piece, copy_piece, 0, unroll=2)
        lax.fori_loop(0, n_piece, shift_piece, 0, unroll=2)
        for s in range(len(src)):
            q = n_rows // 2
            ext1_ref[s, q:q + half, :] = (ext0_ref[s, q:q + half, :] >> 16) | (ext0_ref[s, q + 1:q + half + 1, :] << 16)

    def row_of(c):
        return c * CHUNK if isinstance(c, int) else pl.multiple_of(c * CHUNK, CHUNK)

    def expand(rowt, d):
        return _dot_tn(_scale_rows(rowt), sel_ref[d])

    def run_pipelined(stage_a, stage_b, order):
        stage_a(order(0), 0)

        def two_chunks(t, carry):
            k = 2 * t
            stage_b(order(k), 0)
            stage_a(order(k + 1), 1)
            stage_b(order(k + 1), 1)
            stage_a(order(k + 2), 0)
            return carry

        lax.fori_loop(0, n_ch // 2 - 1, two_chunks, 0)
        stage_b(order(n_ch - 2), 0)
        stage_a(order(n_ch - 1), 1)
        stage_b(order(n_ch - 1), 1)

    @pl.when(jnp.logical_and(p == 0, i == 0))
    def _init_from_context():
        zpad = [jnp.zeros((HALO, LANE), BF16)] * (N_SLAB - 1)
        fill_ext(slab_refs(cx_ref, cb_ref, None)[:N_SLAB - 1], zpad, zpad, ctx_len)
        n_cc = ctx_len // CHUNK
        sf = jnp.zeros((D_STATE, GROUP_W), F32)
        st_b, ey_b = [], []
        for cc in range(n_cc):
            slabs = [_conv_silu(ext_refs, s, cc * CHUNK // 2, cw, cbias) for s in range(N_SLAB - 1)]
            xs = jnp.concatenate(slabs[:2], axis=1).astype(F32)
            bm = slabs[2]
            rowt = crow_ref[0, :, cc * CHUNK:(cc + 1) * CHUNK]
            ewf = expand(rowt, 0)
            ewb = expand(rowt, 1)
            sf = sf * ewf[CHUNK - 1:CHUNK, :GROUP_W] + _dot_tn(bm, (xs * ewf[:, GROUP_W:]).astype(BF16))
            st_b.append(_dot_tn(bm, (xs * ewb[:, GROUP_W:]).astype(BF16)))
            ey_b.append(ewb[0:1, :GROUP_W])
        sb = jnp.zeros((D_STATE, GROUP_W), F32)
        for cc in reversed(range(n_cc)):
            sb = sb * ey_b[cc] + st_b[cc]
        sf_ref[...] = sf
        sb_ref[...] = sb

    @pl.when(p == 0)
    def _forward():
        blk = i
        base = blk * tl
        has_prev = blk > 0
        has_next = blk < n_blk - 1
        fill_ext(slab_refs(x_ref, b_ref, c_ref),
                 [jnp.where(has_prev, v, jnp.zeros_like(v)) for v in slabs_of(xp_ref[0], bp_ref[0], cp_ref[0])],
                 [jnp.where(has_next, v, jnp.zeros_like(v)) for v in slabs_of(xn_ref[0], bn_ref[0], cn_ref[0])],
                 tl)
        ii = lax.broadcasted_iota(jnp.int32, (CHUNK, CHUNK), 0)
        jj = lax.broadcasted_iota(jnp.int32, (CHUNK, CHUNK), 1)
        lower = jj < ii
        upper = jj > ii

        def prepare(c, buf):
            r = row_of(c)
            p0 = c * (CHUNK // 2) if isinstance(c, int) else pl.multiple_of(c * (CHUNK // 2), CHUNK // 2)
            slabs = [_conv_silu(ext_refs, s, p0, cw, cbias) for s in range(N_SLAB)]
            for s in range(N_SLAB):
                cache_ref[pl.ds(base + r, CHUNK), s * LANE:(s + 1) * LANE] = slabs[s]
            yacc_ref[pl.ds(base + r, CHUNK), :] = jnp.concatenate(slabs[:2], axis=1).astype(F32) * dskip_ref[...]
            rowt = row_ref[0, :, pl.ds(r, CHUNK)]
            onesel = onesel_ref[...]
            ones = jnp.ones((N_DH, CHUNK), F32)
            at = jnp.concatenate(_split_terms(rowt[R_CUM:R_CUM + N_DH], N_CUM_TERMS) + [ones] * N_CUM_TERMS,
                                 axis=0).astype(BF16)
            wide = [jnp.concatenate([t] * N_DH, axis=1) * onesel
                    for t in _split_terms(rowt[R_NCX:R_NCX + N_DH], N_CUM_TERMS)]
            bmat = jnp.concatenate([onesel] * N_CUM_TERMS + wide, axis=0).astype(BF16)
            args_bufs[buf][...] = _dot_tn(at, bmat)
            ew_bufs[buf][...] = expand(rowt, 0)

        def consume(c, buf):
            r = row_of(c)
            xbc_b = cache_ref[pl.ds(base + r, CHUNK), :]
            xs_b = xbc_b[:, :GROUP_W]
            bm = xbc_b[:, GROUP_W:GROUP_W + D_STATE]
            cm = xbc_b[:, GROUP_W + D_STATE:]
            cbm = _dot_nt(cm, bm)
            yd = []
            for j in range(HEADS_PER_GROUP):
                argf = args_bufs[buf][:, j * CHUNK:(j + 1) * CHUNK]
                argb = args_bufs[buf][:, (HEADS_PER_GROUP + j) * CHUNK:(HEADS_PER_GROUP + j + 1) * CHUNK]
                diag = row_ref[0, R_DIAG + j:R_DIAG + j + 1, pl.ds(r, CHUNK)]
                arg = jnp.where(lower, argf, jnp.where(upper, argb, diag))
                wmat = (cbm * jnp.exp2(arg)).astype(BF16)
                yd.append(_dot(wmat, xs_b[:, j * HEAD_DIM:(j + 1) * HEAD_DIM]))
            ef = ew_bufs[buf][:, :GROUP_W]
            wxf = ew_bufs[buf][:, GROUP_W:]
            sf = sf_ref[...]
            yacc_ref[pl.ds(base + r, CHUNK), :] += jnp.concatenate(yd, axis=1) + _dot(cm, sf.astype(BF16)) * ef
            sf_ref[...] = sf * ef[CHUNK - 1:CHUNK, :] + _dot_tn(bm, (xs_b.astype(F32) * wxf).astype(BF16))

        run_pipelined(prepare, consume, lambda k: k)

    @pl.when(p == 1)
    def _backward():
        blk = n_blk - 1 - i
        base = blk * tl

        def prepare(c, buf):
            ew_bufs[buf][...] = expand(row_ref[0, :, pl.ds(row_of(c), CHUNK)], 1)

        def consume(c, buf):
            r = row_of(c)
            xbc_b = cache_ref[pl.ds(base + r, CHUNK), :]
            xs = xbc_b[:, :GROUP_W].astype(F32)
            bm = xbc_b[:, GROUP_W:GROUP_W + D_STATE]
            cm = xbc_b[:, GROUP_W + D_STATE:]
            eb = ew_bufs[buf][:, :GROUP_W]
            wxb = ew_bufs[buf][:, GROUP_W:]
            sb = sb_ref[...]
            y = yacc_ref[pl.ds(base + r, CHUNK), :] + _dot(cm, sb.astype(BF16)) * eb
            sb_ref[...] = sb * eb[0:1, :] + _dot_tn(bm, (xs * wxb).astype(BF16))
            z = z_ref[0, pl.ds(r, CHUNK), :].astype(F32)
            y = y * (z * jax.nn.sigmoid(z))
            out_ref[0, pl.ds(r, CHUNK), :] = _rmsnorm(y, gain_ref[...]).astype(BF16)

        run_pipelined(prepare, consume, lambda k: n_ch - 1 - k)


def _ssd_selectors():
    onesel = (np.arange(N_DH * CHUNK)[None, :] // CHUNK == np.arange(N_DH)[:, None]).astype(np.float32)
    sel = np.zeros((2, 4 * N_DH, 2 * GROUP_W), np.float32)
    for d in range(2):
        for j in range(HEADS_PER_GROUP):
            dh = d * HEADS_PER_GROUP + j
            for term in range(2):
                sel[d, term * N_DH + dh, j * HEAD_DIM:(j + 1) * HEAD_DIM] = 1.0
                sel[d, (2 + term) * N_DH + dh, GROUP_W + j * HEAD_DIM:GROUP_W + (j + 1) * HEAD_DIM] = 1.0
    return jnp.asarray(onesel, F32), jnp.asarray(sel, BF16)


def _ssd(u, rows, cu, crows, conv_w, conv_b, dskip, gain, tl):
    b, l, _ = u.shape
    ctx_len = cu.shape[1]
    assert ctx_len <= tl
    n_blk = l // tl
    hb = tl // HALO
    last_halo = l // HALO - 1
    assert (tl // CHUNK) % 2 == 0
    onesel, sel = _ssd_selectors()
    xg, bg, cg = 0, U_B // D_STATE, U_C // D_STATE

    def in_blk(p, i):
        return i * (1 - p) + (n_blk - 1) * p

    def seq_blk(p, i):
        return i * (1 - p) + (n_blk - 1 - i) * p

    def out_blk(p, i):
        return (n_blk - 1) * (1 - p) + (n_blk - 1 - i) * p

    def prev_halo(p, i):
        return jnp.maximum(in_blk(p, i) * hb - 1, 0)

    def next_halo(p, i):
        return jnp.minimum((in_blk(p, i) + 1) * hb, last_halo)

    def xbc_specs(rows_blk, row_index):
        return [pl.BlockSpec((1, rows_blk, GROUP_W), lambda bb, g, p, i: (bb, row_index(p, i), xg + g)),
                pl.BlockSpec((1, rows_blk, D_STATE), lambda bb, g, p, i: (bb, row_index(p, i), bg + g)),
                pl.BlockSpec((1, rows_blk, D_STATE), lambda bb, g, p, i: (bb, row_index(p, i), cg + g))]

    def param_specs(n_rows):
        return [pl.BlockSpec((n_rows, GROUP_W), lambda bb, g, p, i: (0, xg + g)),
                pl.BlockSpec((n_rows, D_STATE), lambda bb, g, p, i: (0, bg + g)),
                pl.BlockSpec((n_rows, D_STATE), lambda bb, g, p, i: (0, cg + g))]

    grid = (b, N_GROUPS, 2, n_blk)
    in_specs = (
        xbc_specs(tl, in_blk) + xbc_specs(HALO, prev_halo) + xbc_specs(HALO, next_halo)
        + [pl.BlockSpec((1, tl, GROUP_W), lambda bb, g, p, i: (bb, out_blk(p, i), U_Z // GROUP_W + g)),
           pl.BlockSpec((1, ROWS_PER_GROUP, tl), lambda bb, g, p, i: (bb, g, seq_blk(p, i)))]
        + xbc_specs(ctx_len, lambda p, i: 0)[:2]
        + [pl.BlockSpec((1, ROWS_PER_GROUP, ctx_len), lambda bb, g, p, i: (bb, g, 0))]
        + param_specs(SUBLANE) + param_specs(1)
        + [pl.BlockSpec((1, GROUP_W), lambda bb, g, p, i: (0, g)),
           pl.BlockSpec((1, GROUP_W), lambda bb, g, p, i: (0, g)),
           pl.BlockSpec(onesel.shape, lambda bb, g, p, i: (0, 0)),
           pl.BlockSpec(sel.shape, lambda bb, g, p, i: (0, 0, 0))])
    return pl.pallas_call(
        functools.partial(_ssd_kernel, tl=tl, n_blk=n_blk, ctx_len=ctx_len),
        grid=grid,
        in_specs=in_specs,
        out_specs=pl.BlockSpec((1, tl, GROUP_W), lambda bb, g, p, i: (bb, out_blk(p, i), g)),
        out_shape=jax.ShapeDtypeStruct((b, l, D_INNER), BF16),
        scratch_shapes=[pltpu.VMEM((l, N_SLAB * LANE), BF16),
                        pltpu.VMEM((l, GROUP_W), F32),
                        pltpu.VMEM((D_STATE, GROUP_W), F32),
                        pltpu.VMEM((D_STATE, GROUP_W), F32),
                        pltpu.VMEM((N_SLAB, (tl + 2 * HALO) // 2, LANE), jnp.uint32),
                        pltpu.VMEM((N_SLAB, (tl + 2 * HALO) // 2, LANE), jnp.uint32),
                        pltpu.VMEM((CHUNK, N_DH * CHUNK), F32),
                        pltpu.VMEM((CHUNK, N_DH * CHUNK), F32),
                        pltpu.VMEM((CHUNK, 2 * GROUP_W), F32),
                        pltpu.VMEM((CHUNK, 2 * GROUP_W), F32)],
        compiler_params=pltpu.CompilerParams(
            dimension_semantics=("arbitrary", "arbitrary", "arbitrary", "arbitrary"),
            vmem_limit_bytes=VMEM_LIMIT),
        name="ssd",
    )(u, u, u, u, u, u, u, u, u, u, rows, cu, cu, crows,
      conv_w, conv_w, conv_w, conv_b, conv_b, conv_b, dskip, gain, onesel, sel)


def _dft_mats():
    def cs(n):
        k = np.arange(n)
        ang = 2.0 * np.pi * np.outer(k, k) / n
        return np.cos(ang), np.sin(ang)

    c3, s3 = cs(FFT_GROUP_DIM)
    w1 = np.concatenate([c3, -s3], axis=1)
    c2, s2 = cs(GRID_W)
    m2 = np.block([[c2, s2], [-s2, c2]])
    return w1, m2


def _fft12_kernel(f_ref, w1_ref, m2_ref, o_ref, *, tm):
    f = f_ref[0]
    w1 = w1_ref[...]
    ps, qs = [], []
    for g in range(FFT_GROUPS):
        pq = _dot(f[:, g * FFT_GROUP_DIM:(g + 1) * FFT_GROUP_DIM], w1)
        ps.append(pq[:, :FFT_GROUP_DIM])
        qs.append(pq[:, FFT_GROUP_DIM:])
    pr = jnp.concatenate(ps, axis=1).astype(BF16)
    qr = jnp.concatenate(qs, axis=1).astype(BF16)
    m2 = m2_ref[...]
    pad = jnp.zeros((FFT_PITCH - GRID_W, D_MODEL), jnp.uint32)
    for r in range(tm // GRID_W):
        sl = slice(r * GRID_W, (r + 1) * GRID_W)
        o = _dot(m2, jnp.concatenate([pr[sl], qr[sl]], axis=0))
        r0 = r * FFT_PITCH
        o_ref[0, r0:r0 + GRID_W, :] = _pack_complex(o[:GRID_W], o[GRID_W:])
        o_ref[0, r0 + GRID_W:r0 + FFT_PITCH, :] = pad


def _pack_complex(re, im):
    re_bits = pltpu.bitcast(re.astype(BF16).astype(F32), jnp.uint32)
    im_bits = pltpu.bitcast(im.astype(BF16).astype(F32), jnp.uint32)
    return (re_bits >> 16) | (im_bits & jnp.uint32(0xFFFF0000))


def _unpack_complex(word):
    re = pltpu.bitcast(word << 16, F32)
    im = pltpu.bitcast(word & jnp.uint32(0xFFFF0000), F32)
    return re, im


def _fft3_kernel(ri_ref, m3_ref, o_ref, scr_ref, *, rows):
    m3 = m3_ref[...]

    def column_pair(t, carry):
        w = 2 * t
        cols = [jnp.concatenate(_unpack_complex(ri_ref[0, pl.ds(w + dw, rows, stride=FFT_PITCH), :]), axis=0)
                for dw in range(2)]
        o = _dot(m3, jnp.concatenate(cols, axis=1).astype(BF16))
        for dw in range(2):
            scr_ref[pl.ds(w + dw, rows, stride=FFT_PITCH), :] = o[:, dw * LANE:(dw + 1) * LANE]
        return carry

    lax.fori_loop(0, GRID_W // 2, column_pair, 0, unroll=4)
    for k in range(rows):
        o_ref[0, k * GRID_W:(k + 1) * GRID_W, :] = scr_ref[k * FFT_PITCH:k * FFT_PITCH + GRID_W, :].astype(BF16)


def _fourier(u, tm):
    b, l, _ = u.shape
    rows = l // GRID_W
    w1, m2 = _dft_mats()
    ri = pl.pallas_call(
        functools.partial(_fft12_kernel, tm=tm),
        grid=(b, l // tm),
        in_specs=[pl.BlockSpec((1, tm, D_MODEL), lambda i, j: (i, j, U_FFT // D_MODEL)),
                  pl.BlockSpec((FFT_GROUP_DIM, 2 * FFT_GROUP_DIM), lambda i, j: (0, 0)),
                  pl.BlockSpec((2 * GRID_W, 2 * GRID_W), lambda i, j: (0, 0))],
        out_specs=pl.BlockSpec((1, tm // GRID_W * FFT_PITCH, D_MODEL), lambda i, j: (i, j, 0)),
        out_shape=jax.ShapeDtypeStruct((b, rows * FFT_PITCH, D_MODEL), jnp.uint32),
        compiler_params=pltpu.CompilerParams(dimension_semantics=("arbitrary", "arbitrary"),
                                             vmem_limit_bytes=VMEM_LIMIT),
        name="fft12",
    )(u, jnp.asarray(w1, BF16), jnp.asarray(m2, BF16))

    k = np.arange(rows)
    ang = 2.0 * np.pi * np.outer(k, k) / rows
    scale = 1.0 / np.sqrt(float(rows * GRID_W * FFT_GROUP_DIM))
    m3 = np.concatenate([np.cos(ang), np.sin(ang)], axis=1) * scale
    return pl.pallas_call(
        functools.partial(_fft3_kernel, rows=rows),
        grid=(b, D_MODEL // LANE),
        in_specs=[pl.BlockSpec((1, rows * FFT_PITCH, LANE), lambda i, j: (i, 0, j)),
                  pl.BlockSpec((rows, 2 * rows), lambda i, j: (0, 0))],
        out_specs=pl.BlockSpec((1, l, LANE), lambda i, j: (i, 0, j)),
        out_shape=jax.ShapeDtypeStruct((b, l, D_MODEL), BF16),
        scratch_shapes=[pltpu.VMEM((rows * FFT_PITCH, LANE), F32)],
        compiler_params=pltpu.CompilerParams(dimension_semantics=("arbitrary", "arbitrary"),
                                             vmem_limit_bytes=VMEM_LIMIT),
        name="fft3",
    )(ri, jnp.asarray(m3, BF16))


FF_SPLIT = 2


def _tail_kernel(y_ref, f_ref, ga_ref, gb_ref, x_ref, g1_ref, sh2_ref, sc2_ref, g2_ref,
                 wso_ref, wfo_ref, wo_ref, n2_ref, wff_ref, wfd_ref, fg_ref, o_ref):
    a = _dot(y_ref[...], wso_ref[...])
    bq = _dot(f_ref[...], wfo_ref[...])
    merged = jax.nn.sigmoid(ga_ref[...].astype(F32)) * a + jax.nn.sigmoid(gb_ref[...].astype(F32)) * bq
    xn = x_ref[...] + g1_ref[0] * _dot(merged.astype(BF16), wo_ref[...])
    h = (_rmsnorm(xn, n2_ref[...]) * (1.0 + sc2_ref[0]) + sh2_ref[0]).astype(BF16)
    ffw = D_FF // FF_SPLIT
    acc = None
    for s in range(FF_SPLIT):
        gate = _dot(h, wff_ref[:, s * ffw:(s + 1) * ffw])
        up = _dot(h, wff_ref[:, D_FF + s * ffw:D_FF + (s + 1) * ffw])
        act = (gate * jax.nn.sigmoid(gate) * up).astype(BF16)
        part = _dot(act, wfd_ref[s * ffw:(s + 1) * ffw, :])
        acc = part if acc is None else acc + part
    o_ref[...] = _rmsnorm(xn + g2_ref[0] * acc, fg_ref[...])


def _tail(y2d, f2d, u2d, x2d, g1, sh2, sc2, g2, wso, wfo, wo, n2, wff, wfd, fg, tm, rows_per_mod):
    m = x2d.shape[0]
    bpm = rows_per_mod // tm

    def const(shape):
        return pl.BlockSpec(shape, lambda i: (0, 0), pipeline_mode=pl.Buffered(1))

    def mod():
        return pl.BlockSpec((1, 1, D_MODEL), lambda i: (i // bpm, 0, 0))

    return pl.pallas_call(
        _tail_kernel,
        grid=(m // tm,),
        in_specs=[pl.BlockSpec((tm, D_INNER), lambda i: (i, 0)),
                  pl.BlockSpec((tm, D_MODEL), lambda i: (i, 0)),
                  pl.BlockSpec((tm, D_MODEL), lambda i: (i, U_GATE // D_MODEL)),
                  pl.BlockSpec((tm, D_MODEL), lambda i: (i, U_GATE // D_MODEL + 1)),
                  pl.BlockSpec((tm, D_MODEL), lambda i: (i, 0)),
                  mod(), mod(), mod(), mod(),
                  const((D_INNER, D_MODEL)), const((D_MODEL, D_MODEL)), const((D_MODEL, D_MODEL)),
                  const((1, D_MODEL)),
                  const((D_MODEL, 2 * D_FF)), const((D_FF, D_MODEL)),
                  const((1, D_MODEL))],
        out_specs=pl.BlockSpec((tm, D_MODEL), lambda i: (i, 0)),
        out_shape=jax.ShapeDtypeStruct((m, D_MODEL), F32),
        compiler_params=pltpu.CompilerParams(dimension_semantics=("arbitrary",),
                                             vmem_limit_bytes=VMEM_LIMIT),
        name="tail",
    )(y2d, f2d, u2d, u2d, x2d, g1, sh2, sc2, g2, wso, wfo, wo, n2, wff, wfd, fg)


def _dt_lanes(a):
    lead = a.shape[:-1]
    a = a.reshape(lead + (2, N_GROUPS, HEADS_PER_GROUP))
    a = jnp.swapaxes(a, -3, -2).reshape(lead + (2 * N_HEADS,))
    return jnp.concatenate([a, jnp.zeros(lead + (LANE - 2 * N_HEADS,), a.dtype)], axis=-1)


def _block(x, c, ctx, c_ctx, w_ada, b_ada, norm1_g, w_in, conv_w, conv_b, dt_bias, a_log, d_skip,
           ssd_norm_g, w_ssd_out, w_fft_out, w_o, norm2_g, w_ffn_in, w_ffn_out, final_g,
           proj_tm, prep_tl, ssd_tl, fft_tm, tail_tm):
    b, l, d = x.shape
    ctx_len = ctx.shape[1]
    assert d == D_MODEL and l % (GRID_W * 2) == 0 and ctx_len % CHUNK == 0 and b + 1 <= SUBLANE

    w_in_t = w_in[0].T
    w_main_t = w_in_t.astype(BF16)
    w_dt = _dt_lanes(w_in_t[REF_DT:REF_Z].T).T
    w_dt_hi = w_dt.astype(BF16)
    w_dt_lo = (w_dt - w_dt_hi.astype(F32)).astype(BF16)
    w_dt2 = jnp.concatenate([w_dt_hi, w_dt_lo], axis=0)
    conv_w_p = jnp.concatenate([conv_w[0], jnp.zeros((SUBLANE - D_CONV, CONV_DIM), F32)], axis=0)
    conv_b_p = conv_b[0].reshape(1, CONV_DIM)
    bias_v = _dt_lanes(dt_bias[0].reshape(-1)).reshape(1, LANE)
    alog_v = _dt_lanes(a_log[0].reshape(-1)).reshape(1, LANE)
    dskip_v = jnp.repeat(d_skip[0], HEAD_DIM).reshape(1, D_INNER)
    gain_v = ssd_norm_g[0].reshape(1, D_INNER)
    wso = w_ssd_out[0].astype(BF16)
    wfo = w_fft_out[0].astype(BF16)
    wo = w_o[0].astype(BF16)
    wff = w_ffn_in[0].astype(BF16)
    wfd = w_ffn_out[0].astype(BF16)

    cvecs = jnp.concatenate([c, c_ctx[None, :], jnp.zeros((SUBLANE - b - 1, D_MODEL), F32)], axis=0)
    mods = _mods(cvecs, b + 1, w_ada[0], b_ada[0])
    lat = mods[:b].reshape(b, 6, 1, D_MODEL)
    sh1, sc1, g1, sh2, sc2, g2 = (lat[:, k] for k in range(6))
    cm = mods[b].reshape(6, 1, 1, D_MODEL)

    x2d = x.reshape(b * l, D_MODEL)
    n1 = norm1_g[0].reshape(1, D_MODEL)
    u2d, dt_raw = _inproj(x2d, sh1, sc1, n1, w_main_t, w_dt2, U_WIDTH, proj_tm, l)
    u = u2d.reshape(b, l, U_WIDTH)
    cu2d, cdt_raw = _inproj(ctx.reshape(b * ctx_len, D_MODEL), cm[0], cm[1], n1, w_main_t, w_dt2,
                            CONV_DIM, b * ctx_len, b * ctx_len)
    cu = cu2d.reshape(b, ctx_len, CONV_DIM)

    rows = _dtprep(dt_raw.reshape(b, l, LANE), bias_v, alog_v, prep_tl)
    crows = _dtprep(cdt_raw.reshape(b, ctx_len, LANE), bias_v, alog_v, ctx_len)
    y_ssd = _ssd(u, rows, cu, crows, conv_w_p, conv_b_p, dskip_v, gain_v, ssd_tl)

    f_mix = _fourier(u, fft_tm)

    out = _tail(y_ssd.reshape(b * l, D_INNER), f_mix.reshape(b * l, D_MODEL), u2d, x2d,
                g1, sh2, sc2, g2, wso, wfo, wo, norm2_g[0].reshape(1, D_MODEL), wff, wfd,
                final_g.reshape(1, D_MODEL), tail_tm, l)
    return out.reshape(b, l, D_MODEL)


def kernel(x, c, ctx, c_ctx, w_ada, b_ada, norm1_g, w_in, conv_w, conv_b, dt_bias, a_log, d_skip,
           ssd_norm_g, w_ssd_out, w_fft_out, w_o, norm2_g, w_ffn_in, w_ffn_out, final_g):
    return _block(x, c, ctx, c_ctx, w_ada, b_ada, norm1_g, w_in, conv_w, conv_b, dt_bias, a_log, d_skip,
                  ssd_norm_g, w_ssd_out, w_fft_out, w_o, norm2_g, w_ffn_in, w_ffn_out, final_g,
                  proj_tm=2048, prep_tl=1024, ssd_tl=4096, fft_tm=512, tail_tm=512)
```

```python
import functools
import math

import jax
import jax.numpy as jnp
import numpy as np
from jax import lax
from jax.experimental import pallas as pl
from jax.experimental.pallas import tpu as pltpu

F32 = jnp.float32
BF16 = jnp.bfloat16

D_MODEL = 1024
EPS = 1e-6
GRID_W = 64

N_GROUPS = 8
HEADS_PER_GROUP = 4
HEAD_DIM = 64
N_HEADS = N_GROUPS * HEADS_PER_GROUP
D_STATE = 128
D_INNER = N_HEADS * HEAD_DIM
GROUP_W = HEADS_PER_GROUP * HEAD_DIM
GS = N_GROUPS * D_STATE
CONV_DIM = D_INNER + 2 * GS
D_CONV = 5
CHUNK = 128

FFT_GROUPS = 8
FFT_GROUP_DIM = 128
D_FF = 2816

REF_DT = CONV_DIM
REF_Z = REF_DT + 2 * N_HEADS

U_B = D_INNER
U_C = U_B + GS
U_Z = CONV_DIM
U_FFT = U_Z + D_INNER
U_GATE = U_FFT + D_MODEL
U_WIDTH = U_GATE + 2 * D_MODEL

LANE = 128
SUBLANE = 8
HALO = 16
VMEM_LIMIT = 56 * 1024 * 1024

N_DH = 2 * HEADS_PER_GROUP
R_CUM, R_NCX, R_EY, R_WX, R_DIAG = 0, 8, 16, 24, 32
ROWS_PER_GROUP = 40
N_SLAB = (GROUP_W + 2 * D_STATE) // LANE
DT_FLOOR = 1e-37
LOG2E = math.log2(math.e)
N_CUM_TERMS = 3
FWD_PER_TRIP = 4
BWD_PER_TRIP = 4
K_ARGS = 2 * N_CUM_TERMS * N_DH
FFT_PITCH = GRID_W + 4


def _dot(a, b):
    return jnp.dot(a, b, preferred_element_type=F32)


def _dot_tn(a, b):
    return lax.dot_general(a, b, (((0,), (0,)), ((), ())), preferred_element_type=F32)


def _dot_nt(a, b):
    return lax.dot_general(a, b, (((1,), (1,)), ((), ())), preferred_element_type=F32)


def _rmsnorm(x, g):
    return x * lax.rsqrt(jnp.mean(x * x, axis=-1, keepdims=True) + EPS) * g


def _mods_kernel(ct_ref, w_ref, b_ref, o_ref, *, n_vecs):
    s = ct_ref[...]
    s = s * jax.nn.sigmoid(s)
    w = w_ref[...]
    rows = [jnp.sum(w * s[:, r:r + 1], axis=0, keepdims=True) for r in range(n_vecs)]
    rows.append(jnp.zeros((SUBLANE - n_vecs, w.shape[1]), F32))
    o_ref[...] = jnp.concatenate(rows, axis=0) + b_ref[...]


def _mods(cvecs, n_vecs, w_ada, b_ada):
    n = w_ada.shape[1]
    tn = 1536
    return pl.pallas_call(
        functools.partial(_mods_kernel, n_vecs=n_vecs),
        grid=(n // tn,),
        in_specs=[pl.BlockSpec((D_MODEL, SUBLANE), lambda j: (0, 0)),
                  pl.BlockSpec((D_MODEL, tn), lambda j: (0, j)),
                  pl.BlockSpec((1, tn), lambda j: (0, j))],
        out_specs=pl.BlockSpec((SUBLANE, tn), lambda j: (0, j)),
        out_shape=jax.ShapeDtypeStruct((SUBLANE, n), F32),
        compiler_params=pltpu.CompilerParams(dimension_semantics=("arbitrary",),
                                             vmem_limit_bytes=VMEM_LIMIT),
        name="mods",
    )(cvecs.T, w_ada, b_ada.reshape(1, n))


def _inproj_kernel(x_ref, sh_ref, sc_ref, g_ref, w_ref, wdt_ref, u_ref, dt_ref, h_ref):
    @pl.when(pl.program_id(1) == 0)
    def _():
        h = _rmsnorm(x_ref[...], g_ref[...]) * (1.0 + sc_ref[0]) + sh_ref[0]
        hb = h.astype(BF16)
        h_ref[...] = hb
        hl = (h - hb.astype(F32)).astype(BF16)
        a = _dot_nt(hb, wdt_ref[...])
        dt_ref[...] = a[:, :LANE] + a[:, LANE:] + _dot_nt(hl, wdt_ref[:LANE, :])

    u_ref[...] = _dot_nt(h_ref[...], w_ref[...]).astype(BF16)


def _inproj(x2d, shift, scale, gain, w_t, w_dt_t, n_cols, tm, rows_per_mod):
    m = x2d.shape[0]
    tn = 1024
    bpm = rows_per_mod // tm

    def w_rows(i, j):
        return pl.multiple_of(j * tn + jnp.where(j * tn >= REF_DT, REF_Z - REF_DT, 0), 2 * HALO), 0

    return pl.pallas_call(
        _inproj_kernel,
        grid=(m // tm, n_cols // tn),
        in_specs=[pl.BlockSpec((tm, D_MODEL), lambda i, j: (i, 0)),
                  pl.BlockSpec((1, 1, D_MODEL), lambda i, j: (i // bpm, 0, 0)),
                  pl.BlockSpec((1, 1, D_MODEL), lambda i, j: (i // bpm, 0, 0)),
                  pl.BlockSpec((1, D_MODEL), lambda i, j: (0, 0)),
                  pl.BlockSpec((pl.Element(tn), pl.Element(D_MODEL)), w_rows),
                  pl.BlockSpec((2 * LANE, D_MODEL), lambda i, j: (0, 0))],
        out_specs=[pl.BlockSpec((tm, tn), lambda i, j: (i, j)),
                   pl.BlockSpec((tm, LANE), lambda i, j: (i, 0))],
        out_shape=[jax.ShapeDtypeStruct((m, n_cols), BF16),
                   jax.ShapeDtypeStruct((m, LANE), F32)],
        scratch_shapes=[pltpu.VMEM((tm, D_MODEL), BF16)],
        compiler_params=pltpu.CompilerParams(dimension_semantics=("arbitrary", "arbitrary"),
                                             vmem_limit_bytes=VMEM_LIMIT),
        name="inproj",
    )(x2d, shift, scale, gain, w_t, w_dt_t)


def _dtprep_kernel(raw_ref, bias_ref, alog_ref, row_ref, *, n_chunks):
    a = -jnp.exp(alog_ref[...])
    bias = bias_ref[...]
    ii = lax.broadcasted_iota(jnp.int32, (CHUNK, CHUNK), 0)
    jj = lax.broadcasted_iota(jnp.int32, (CHUNK, CHUNK), 1)
    tril = (jj <= ii).astype(F32)
    triu = (jj >= ii).astype(F32)
    lane = lax.broadcasted_iota(jnp.int32, (1, LANE), 1)
    is_fwd = (lane % N_DH) < HEADS_PER_GROUP
    for c in range(n_chunks):
        sl = slice(c * CHUNK, (c + 1) * CHUNK)
        dt = jax.nn.softplus(raw_ref[0, sl, :] + bias)
        dta = dt * a
        cf = jnp.dot(tril, dta, precision=lax.Precision.HIGHEST, preferred_element_type=F32)
        cb = jnp.dot(triu, dta, precision=lax.Precision.HIGHEST, preferred_element_type=F32)
        cum = jnp.where(is_fwd, cf, cb)
        tot = jnp.where(is_fwd, cf[CHUNK - 1:CHUNK, :], cb[0:1, :])
        wx = dt * jnp.exp(tot - cum)
        ey = jnp.exp(cum)
        ncx = jnp.log(jnp.maximum(dt, DT_FLOOR)) - cum
        diag = jnp.log(dt + pltpu.roll(dt, LANE - HEADS_PER_GROUP, 1))
        for r_off, val in ((R_CUM, cum * LOG2E), (R_NCX, ncx * LOG2E), (R_EY, ey), (R_WX, wx),
                           (R_DIAG, diag * LOG2E)):
            vt = val.T
            for g in range(N_GROUPS):
                r0 = g * ROWS_PER_GROUP + r_off
                row_ref[0, r0:r0 + N_DH, sl] = vt[g * N_DH:(g + 1) * N_DH, :]


def _dtprep(raw, bias, alog, tl):
    b, l, _ = raw.shape
    return pl.pallas_call(
        functools.partial(_dtprep_kernel, n_chunks=tl // CHUNK),
        grid=(b, l // tl),
        in_specs=[pl.BlockSpec((1, tl, LANE), lambda i, j: (i, j, 0)),
                  pl.BlockSpec((1, LANE), lambda i, j: (0, 0)),
                  pl.BlockSpec((1, LANE), lambda i, j: (0, 0))],
        out_specs=pl.BlockSpec((1, N_GROUPS * ROWS_PER_GROUP, tl), lambda i, j: (i, 0, j)),
        out_shape=jax.ShapeDtypeStruct((b, N_GROUPS * ROWS_PER_GROUP, l), F32),
        compiler_params=pltpu.CompilerParams(dimension_semantics=("arbitrary", "arbitrary"),
                                             vmem_limit_bytes=VMEM_LIMIT),
        name="dtprep",
    )(raw, bias, alog)


def _split_terms(x, n):
    terms = []
    for _ in range(n - 1):
        t = x.astype(BF16).astype(F32)
        terms.append(t)
        x = x - t
    terms.append(x)
    return terms


def _scale_rows(rows):
    return jnp.concatenate(_split_terms(rows[R_EY:R_EY + N_DH], 2) + _split_terms(rows[R_WX:R_WX + N_DH], 2),
                           axis=0).astype(BF16)


def _conv_silu(ext_refs, s, p0, cw, cbias):
    sl = slice(s * LANE, (s + 1) * LANE)
    acc = cbias[:, sl].astype(BF16)
    for k in range(D_CONV):
        pair = p0 + (HALO + k - D_CONV // 2 - k % 2) // 2
        tap = pltpu.bitcast(ext_refs[k % 2][s, pl.ds(pair, CHUNK // 2), :], BF16)
        acc = acc + cw[k:k + 1, sl].astype(BF16) * tap
    return acc * jax.nn.sigmoid(acc)


def _ssd_kernel(x_ref, b_ref, c_ref, xp_ref, bp_ref, cp_ref, xn_ref, bn_ref, cn_ref, z_ref, row_ref,
                cx_ref, cb_ref, crow_ref, cwx_ref, cwb_ref, cwc_ref, cbx_ref, cbb_ref, cbc_ref,
                dskip_ref, gain_ref, onesel_ref, sel_ref, out_ref,
                cache_ref, yacc_ref, sf_ref, sb_ref, ext0_ref, ext1_ref, args0_ref, args1_ref, ew0_ref, ew1_ref,
                st0_ref, st1_ref, *, tl, n_blk, ctx_len):
    p = pl.program_id(2)
    i = pl.program_id(3)
    n_ch = tl // CHUNK
    args_bufs = (args0_ref, args1_ref)
    ew_bufs = (ew0_ref, ew1_ref)
    st_bufs = (st0_ref, st1_ref)
    ext_refs = (ext0_ref, ext1_ref)
    cw = jnp.concatenate([cwx_ref[...], cwb_ref[...], cwc_ref[...]], axis=1)
    cbias = jnp.concatenate([cbx_ref[...], cbb_ref[...], cbc_ref[...]], axis=1)

    def slabs_of(x, bm, cm):
        return [x[:, :LANE], x[:, LANE:], bm, cm]

    def slab_refs(xr, br, cr):
        return [(xr, 0), (xr, LANE), (br, 0), (cr, 0)]

    def fill_ext(src, first, last, n_rows):
        half = HALO // 2
        n_piece = n_rows // CHUNK

        def copy_piece(k, carry):
            r = pl.multiple_of(k * CHUNK, CHUNK)
            q = pl.multiple_of(k * (CHUNK // 2), CHUNK // 2)
            for s, (ref, lane0) in enumerate(src):
                ext0_ref[s, pl.ds(half + q, CHUNK // 2), :] = pltpu.bitcast(
                    ref[0, pl.ds(r, CHUNK), lane0:lane0 + LANE], jnp.uint32)
            return carry

        def shift_piece(k, carry):
            q = pl.multiple_of(k * (CHUNK // 2), CHUNK // 2)
            for s in range(len(src)):
                lo = ext0_ref[s, pl.ds(q, CHUNK // 2), :]
                hi = ext0_ref[s, pl.ds(q + 1, CHUNK // 2), :]
                ext1_ref[s, pl.ds(q, CHUNK // 2), :] = (lo >> 16) | (hi << 16)
            return carry

        for s in range(len(src)):
            ext0_ref[s, 0:half, :] = pltpu.bitcast(first[s], jnp.uint32)
            ext0_ref[s, half + n_rows // 2:2 * half + n_rows // 2, :] = pltpu.bitcast(last[s], jnp.uint32)
        lax.fori_loop(0, n_piece, copy_piece, 0, unroll=2)
        lax.fori_loop(0, n_piece, shift_piece, 0, unroll=2)
        for s in range(len(src)):
            q = n_rows // 2
            ext1_ref[s, q:q + half, :] = (ext0_ref[s, q:q + half, :] >> 16) | (ext0_ref[s, q + 1:q + half + 1, :] << 16)

    def row_of(c):
        return c * CHUNK if isinstance(c, int) else pl.multiple_of(c * CHUNK, CHUNK)

    def expand(rowt, d):
        return _dot_tn(_scale_rows(rowt), sel_ref[d])

    def run_pipelined(before, stage_b, after, order, per_trip):
        def step(k, buf, with_next):
            if with_next:
                before(order(k + 1), 1 - buf)
            stage_b(order(k), buf)
            if with_next:
                after(order(k + 1), 1 - buf)

        before(order(0), 0)
        after(order(0), 0)

        def trip(t, carry):
            for d in range(per_trip):
                step(per_trip * t + d, d % 2, True)
            return carry

        lax.fori_loop(0, n_ch // per_trip - 1, trip, 0)
        for k in range(n_ch - per_trip, n_ch):
            step(k, k % 2, k + 1 < n_ch)

    @pl.when(jnp.logical_and(p == 0, i == 0))
    def _init_from_context():
        zpad = [jnp.zeros((HALO, LANE), BF16)] * (N_SLAB - 1)
        fill_ext(slab_refs(cx_ref, cb_ref, None)[:N_SLAB - 1], zpad, zpad, ctx_len)
        n_cc = ctx_len // CHUNK
        sf = jnp.zeros((D_STATE, GROUP_W), F32)
        st_b, ey_b = [], []
        for cc in range(n_cc):
            slabs = [_conv_silu(ext_refs, s, cc * CHUNK // 2, cw, cbias) for s in range(N_SLAB - 1)]
            xs = jnp.concatenate(slabs[:2], axis=1).astype(F32)
            bm = slabs[2]
            rowt = crow_ref[0, :, cc * CHUNK:(cc + 1) * CHUNK]
            ewf = expand(rowt, 0)
            ewb = expand(rowt, 1)
            sf = sf * ewf[CHUNK - 1:CHUNK, :GROUP_W] + _dot_tn(bm, (xs * ewf[:, GROUP_W:]).astype(BF16))
            st_b.append(_dot_tn(bm, (xs * ewb[:, GROUP_W:]).astype(BF16)))
            ey_b.append(ewb[0:1, :GROUP_W])
        sb = jnp.zeros((D_STATE, GROUP_W), F32)
        for cc in reversed(range(n_cc)):
            sb = sb * ey_b[cc] + st_b[cc]
        sf_ref[...] = sf
        sb_ref[...] = sb

    @pl.when(p == 0)
    def _forward():
        blk = i
        base = blk * tl
        has_prev = blk > 0
        has_next = blk < n_blk - 1
        fill_ext(slab_refs(x_ref, b_ref, c_ref),
                 [jnp.where(has_prev, v, jnp.zeros_like(v)) for v in slabs_of(xp_ref[0], bp_ref[0], cp_ref[0])],
                 [jnp.where(has_next, v, jnp.zeros_like(v)) for v in slabs_of(xn_ref[0], bn_ref[0], cn_ref[0])],
                 tl)
        ii = lax.broadcasted_iota(jnp.int32, (CHUNK, CHUNK), 0)
        jj = lax.broadcasted_iota(jnp.int32, (CHUNK, CHUNK), 1)
        lower = jj < ii
        upper = jj > ii

        def prepare_conv(c, buf):
            r = row_of(c)
            p0 = c * (CHUNK // 2) if isinstance(c, int) else pl.multiple_of(c * (CHUNK // 2), CHUNK // 2)
            slabs = [_conv_silu(ext_refs, s, p0, cw, cbias) for s in range(N_SLAB)]
            for s in range(N_SLAB):
                cache_ref[pl.ds(base + r, CHUNK), s * LANE:(s + 1) * LANE] = slabs[s]
            yacc_ref[pl.ds(base + r, CHUNK), :] = jnp.concatenate(slabs[:2], axis=1).astype(F32) * dskip_ref[...]

        def prepare_scales(c, buf):
            rowt = row_ref[0, :, pl.ds(row_of(c), CHUNK)]
            onesel = onesel_ref[...]
            ones = jnp.ones((N_DH, CHUNK), F32)
            at = jnp.concatenate(_split_terms(rowt[R_CUM:R_CUM + N_DH], N_CUM_TERMS) + [ones] * N_CUM_TERMS,
                                 axis=0).astype(BF16)
            wide = [jnp.concatenate([t] * N_DH, axis=1) * onesel
                    for t in _split_terms(rowt[R_NCX:R_NCX + N_DH], N_CUM_TERMS)]
            bmat = jnp.concatenate([onesel] * N_CUM_TERMS + wide, axis=0).astype(BF16)
            args_bufs[buf][...] = _dot_tn(at, bmat)
            ew_bufs[buf][...] = expand(rowt, 0)

        def consume(c, buf):
            r = row_of(c)
            xbc_b = cache_ref[pl.ds(base + r, CHUNK), :]
            xs_b = xbc_b[:, :GROUP_W]
            bm = xbc_b[:, GROUP_W:GROUP_W + D_STATE]
            cm = xbc_b[:, GROUP_W + D_STATE:]
            cbm = _dot_nt(cm, bm)
            yd = []
            for j in range(HEADS_PER_GROUP):
                argf = args_bufs[buf][:, j * CHUNK:(j + 1) * CHUNK]
                argb = args_bufs[buf][:, (HEADS_PER_GROUP + j) * CHUNK:(HEADS_PER_GROUP + j + 1) * CHUNK]
                diag = row_ref[0, R_DIAG + j:R_DIAG + j + 1, pl.ds(r, CHUNK)]
                arg = jnp.where(lower, argf, jnp.where(upper, argb, diag))
                wmat = (cbm * jnp.exp2(arg)).astype(BF16)
                yd.append(_dot(wmat, xs_b[:, j * HEAD_DIM:(j + 1) * HEAD_DIM]))
            ef = ew_bufs[buf][:, :GROUP_W]
            wxf = ew_bufs[buf][:, GROUP_W:]
            sf = sf_ref[...]
            yacc_ref[pl.ds(base + r, CHUNK), :] += jnp.concatenate(yd, axis=1) + _dot(cm, sf.astype(BF16)) * ef
            sf_ref[...] = sf * ef[CHUNK - 1:CHUNK, :] + _dot_tn(bm, (xs_b.astype(F32) * wxf).astype(BF16))

        def prepare(c, buf):
            prepare_conv(c, buf)
            prepare_scales(c, buf)

        run_pipelined(lambda c, buf: None, consume, prepare, lambda k: k, FWD_PER_TRIP)

    @pl.when(p == 1)
    def _backward():
        blk = n_blk - 1 - i
        base = blk * tl

        def prepare(c, buf):
            r = row_of(c)
            ew = expand(row_ref[0, :, pl.ds(r, CHUNK)], 1)
            ew_bufs[buf][...] = ew
            xs = cache_ref[pl.ds(base + r, CHUNK), 0:GROUP_W].astype(F32)
            bm = cache_ref[pl.ds(base + r, CHUNK), GROUP_W:GROUP_W + D_STATE]
            st_bufs[buf][...] = _dot_tn(bm, (xs * ew[:, GROUP_W:]).astype(BF16))

        def consume(c, buf):
            r = row_of(c)
            cm = cache_ref[pl.ds(base + r, CHUNK), GROUP_W + D_STATE:]
            eb = ew_bufs[buf][:, :GROUP_W]
            sb = sb_ref[...]
            y = yacc_ref[pl.ds(base + r, CHUNK), :] + _dot(cm, sb.astype(BF16)) * eb
            sb_ref[...] = sb * eb[0:1, :] + st_bufs[buf][...]
            z = z_ref[0, pl.ds(r, CHUNK), :].astype(F32)
            y = y * (z * jax.nn.sigmoid(z))
            out_ref[0, pl.ds(r, CHUNK), :] = _rmsnorm(y, gain_ref[...]).astype(BF16)

        run_pipelined(prepare, consume, lambda c, buf: None, lambda k: n_ch - 1 - k, BWD_PER_TRIP)


def _ssd_selectors():
    onesel = (np.arange(N_DH * CHUNK)[None, :] // CHUNK == np.arange(N_DH)[:, None]).astype(np.float32)
    sel = np.zeros((2, 4 * N_DH, 2 * GROUP_W), np.float32)
    for d in range(2):
        for j in range(HEADS_PER_GROUP):
            dh = d * HEADS_PER_GROUP + j
            for term in range(2):
                sel[d, term * N_DH + dh, j * HEAD_DIM:(j + 1) * HEAD_DIM] = 1.0
                sel[d, (2 + term) * N_DH + dh, GROUP_W + j * HEAD_DIM:GROUP_W + (j + 1) * HEAD_DIM] = 1.0
    return jnp.asarray(onesel, F32), jnp.asarray(sel, BF16)


def _ssd(u, rows, cu, crows, conv_w, conv_b, dskip, gain, tl):
    b, l, _ = u.shape
    ctx_len = cu.shape[1]
    assert ctx_len <= tl
    n_blk = l // tl
    hb = tl // HALO
    last_halo = l // HALO - 1
    assert (tl // CHUNK) % FWD_PER_TRIP == 0 and (tl // CHUNK) % BWD_PER_TRIP == 0
    onesel, sel = _ssd_selectors()
    xg, bg, cg = 0, U_B // D_STATE, U_C // D_STATE

    def in_blk(p, i):
        return i * (1 - p) + (n_blk - 1) * p

    def seq_blk(p, i):
        return i * (1 - p) + (n_blk - 1 - i) * p

    def out_blk(p, i):
        return (n_blk - 1) * (1 - p) + (n_blk - 1 - i) * p

    def prev_halo(p, i):
        return jnp.maximum(in_blk(p, i) * hb - 1, 0)

    def next_halo(p, i):
        return jnp.minimum((in_blk(p, i) + 1) * hb, last_halo)

    def xbc_specs(rows_blk, row_index):
        return [pl.BlockSpec((1, rows_blk, GROUP_W), lambda bb, g, p, i: (bb, row_index(p, i), xg + g)),
                pl.BlockSpec((1, rows_blk, D_STATE), lambda bb, g, p, i: (bb, row_index(p, i), bg + g)),
                pl.BlockSpec((1, rows_blk, D_STATE), lambda bb, g, p, i: (bb, row_index(p, i), cg + g))]

    def param_specs(n_rows):
        return [pl.BlockSpec((n_rows, GROUP_W), lambda bb, g, p, i: (0, xg + g)),
                pl.BlockSpec((n_rows, D_STATE), lambda bb, g, p, i: (0, bg + g)),
                pl.BlockSpec((n_rows, D_STATE), lambda bb, g, p, i: (0, cg + g))]

    grid = (b, N_GROUPS, 2, n_blk)
    in_specs = (
        xbc_specs(tl, in_blk) + xbc_specs(HALO, prev_halo) + xbc_specs(HALO, next_halo)
        + [pl.BlockSpec((1, tl, GROUP_W), lambda bb, g, p, i: (bb, out_blk(p, i), U_Z // GROUP_W + g)),
           pl.BlockSpec((1, ROWS_PER_GROUP, tl), lambda bb, g, p, i: (bb, g, seq_blk(p, i)))]
        + xbc_specs(ctx_len, lambda p, i: 0)[:2]
        + [pl.BlockSpec((1, ROWS_PER_GROUP, ctx_len), lambda bb, g, p, i: (bb, g, 0))]
        + param_specs(SUBLANE) + param_specs(1)
        + [pl.BlockSpec((1, GROUP_W), lambda bb, g, p, i: (0, g)),
           pl.BlockSpec((1, GROUP_W), lambda bb, g, p, i: (0, g)),
           pl.BlockSpec(onesel.shape, lambda bb, g, p, i: (0, 0)),
           pl.BlockSpec(sel.shape, lambda bb, g, p, i: (0, 0, 0))])
    return pl.pallas_call(
        functools.partial(_ssd_kernel, tl=tl, n_blk=n_blk, ctx_len=ctx_len),
        grid=grid,
        in_specs=in_specs,
        out_specs=pl.BlockSpec((1, tl, GROUP_W), lambda bb, g, p, i: (bb, out_blk(p, i), g)),
        out_shape=jax.ShapeDtypeStruct((b, l, D_INNER), BF16),
        scratch_shapes=[pltpu.VMEM((l, N_SLAB * LANE), BF16),
                        pltpu.VMEM((l, GROUP_W), F32),
                        pltpu.VMEM((D_STATE, GROUP_W), F32),
                        pltpu.VMEM((D_STATE, GROUP_W), F32),
                        pltpu.VMEM((N_SLAB, (tl + 2 * HALO) // 2, LANE), jnp.uint32),
                        pltpu.VMEM((N_SLAB, (tl + 2 * HALO) // 2, LANE), jnp.uint32),
                        pltpu.VMEM((CHUNK, N_DH * CHUNK), F32),
                        pltpu.VMEM((CHUNK, N_DH * CHUNK), F32),
                        pltpu.VMEM((CHUNK, 2 * GROUP_W), F32),
                        pltpu.VMEM((CHUNK, 2 * GROUP_W), F32),
                        pltpu.VMEM((D_STATE, GROUP_W), F32),
                        pltpu.VMEM((D_STATE, GROUP_W), F32)],
        compiler_params=pltpu.CompilerParams(
            dimension_semantics=("arbitrary", "arbitrary", "arbitrary", "arbitrary"),
            vmem_limit_bytes=VMEM_LIMIT),
        name="ssd",
    )(u, u, u, u, u, u, u, u, u, u, rows, cu, cu, crows,
      conv_w, conv_w, conv_w, conv_b, conv_b, conv_b, dskip, gain, onesel, sel)


def _dft_mats():
    def cs(n):
        k = np.arange(n)
        ang = 2.0 * np.pi * np.outer(k, k) / n
        return np.cos(ang), np.sin(ang)

    c3, s3 = cs(FFT_GROUP_DIM)
    w1 = np.concatenate([c3, -s3], axis=1)
    c2, s2 = cs(GRID_W)
    m2 = np.block([[c2, s2], [-s2, c2]])
    return w1, m2


def _fft12_kernel(f_ref, w1_ref, m2_ref, o_ref, *, tm):
    f = f_ref[0]
    w1 = w1_ref[...]
    ps, qs = [], []
    for g in range(FFT_GROUPS):
        pq = _dot(f[:, g * FFT_GROUP_DIM:(g + 1) * FFT_GROUP_DIM], w1)
        ps.append(pq[:, :FFT_GROUP_DIM])
        qs.append(pq[:, FFT_GROUP_DIM:])
    pr = jnp.concatenate(ps, axis=1).astype(BF16)
    qr = jnp.concatenate(qs, axis=1).astype(BF16)
    m2 = m2_ref[...]
    pad = jnp.zeros((FFT_PITCH - GRID_W, D_MODEL), jnp.uint32)
    for r in range(tm // GRID_W):
        sl = slice(r * GRID_W, (r + 1) * GRID_W)
        o = _dot(m2, jnp.concatenate([pr[sl], qr[sl]], axis=0))
        r0 = r * FFT_PITCH
        o_ref[0, r0:r0 + GRID_W, :] = _pack_complex(o[:GRID_W], o[GRID_W:])
        o_ref[0, r0 + GRID_W:r0 + FFT_PITCH, :] = pad


def _pack_complex(re, im):
    re_bits = pltpu.bitcast(re.astype(BF16).astype(F32), jnp.uint32)
    im_bits = pltpu.bitcast(im.astype(BF16).astype(F32), jnp.uint32)
    return (re_bits >> 16) | (im_bits & jnp.uint32(0xFFFF0000))


def _unpack_complex(word):
    re = pltpu.bitcast(word << 16, F32)
    im = pltpu.bitcast(word & jnp.uint32(0xFFFF0000), F32)
    return re, im


def _fft3_kernel(ri_ref, m3_ref, o_ref, scr_ref, *, rows):
    m3 = m3_ref[...]

    def column_pair(t, carry):
        w = 2 * t
        cols = [jnp.concatenate(_unpack_complex(ri_ref[0, pl.ds(w + dw, rows, stride=FFT_PITCH), :]), axis=0)
                for dw in range(2)]
        o = _dot(m3, jnp.concatenate(cols, axis=1).astype(BF16))
        for dw in range(2):
            scr_ref[pl.ds(w + dw, rows, stride=FFT_PITCH), :] = o[:, dw * LANE:(dw + 1) * LANE]
        return carry

    lax.fori_loop(0, GRID_W // 2, column_pair, 0, unroll=4)
    for k in range(rows):
        o_ref[0, k * GRID_W:(k + 1) * GRID_W, :] = scr_ref[k * FFT_PITCH:k * FFT_PITCH + GRID_W, :].astype(BF16)


def _fourier(u, tm):
    b, l, _ = u.shape
    rows = l // GRID_W
    w1, m2 = _dft_mats()
    ri = pl.pallas_call(
        functools.partial(_fft12_kernel, tm=tm),
        grid=(b, l // tm),
        in_specs=[pl.BlockSpec((1, tm, D_MODEL), lambda i, j: (i, j, U_FFT // D_MODEL)),
                  pl.BlockSpec((FFT_GROUP_DIM, 2 * FFT_GROUP_DIM), lambda i, j: (0, 0)),
                  pl.BlockSpec((2 * GRID_W, 2 * GRID_W), lambda i, j: (0, 0))],
        out_specs=pl.BlockSpec((1, tm // GRID_W * FFT_PITCH, D_MODEL), lambda i, j: (i, j, 0)),
        out_shape=jax.ShapeDtypeStruct((b, rows * FFT_PITCH, D_MODEL), jnp.uint32),
        compiler_params=pltpu.CompilerParams(dimension_semantics=("arbitrary", "arbitrary"),
                                             vmem_limit_bytes=VMEM_LIMIT),
        name="fft12",
    )(u, jnp.asarray(w1, BF16), jnp.asarray(m2, BF16))

    k = np.arange(rows)
    ang = 2.0 * np.pi * np.outer(k, k) / rows
    scale = 1.0 / np.sqrt(float(rows * GRID_W * FFT_GROUP_DIM))
    m3 = np.concatenate([np.cos(ang), np.sin(ang)], axis=1) * scale
    return pl.pallas_call(
        functools.partial(_fft3_kernel, rows=rows),
        grid=(b, D_MODEL // LANE),
        in_specs=[pl.BlockSpec((1, rows * FFT_PITCH, LANE), lambda i, j: (i, 0, j)),
                  pl.BlockSpec((rows, 2 * rows), lambda i, j: (0, 0))],
        out_specs=pl.BlockSpec((1, l, LANE), lambda i, j: (i, 0, j)),
        out_shape=jax.ShapeDtypeStruct((b, l, D_MODEL), BF16),
        scratch_shapes=[pltpu.VMEM((rows * FFT_PITCH, LANE), F32)],
        compiler_params=pltpu.CompilerParams(dimension_semantics=("arbitrary", "arbitrary"),
                                             vmem_limit_bytes=VMEM_LIMIT),
        name="fft3",
    )(ri, jnp.asarray(m3, BF16))


FF_SPLIT = 2


def _tail_kernel(y_ref, f_ref, ga_ref, gb_ref, x_ref, g1_ref, sh2_ref, sc2_ref, g2_ref,
                 wso_ref, wfo_ref, wo_ref, n2_ref, wff_ref, wfd_ref, fg_ref, o_ref):
    a = _dot(y_ref[...], wso_ref[...])
    bq = _dot(f_ref[...], wfo_ref[...])
    merged = jax.nn.sigmoid(ga_ref[...].astype(F32)) * a + jax.nn.sigmoid(gb_ref[...].astype(F32)) * bq
    xn = x_ref[...] + g1_ref[0] * _dot(merged.astype(BF16), wo_ref[...])
    h = (_rmsnorm(xn, n2_ref[...]) * (1.0 + sc2_ref[0]) + sh2_ref[0]).astype(BF16)
    ffw = D_FF // FF_SPLIT
    acc = None
    for s in range(FF_SPLIT):
        gate = _dot(h, wff_ref[:, s * ffw:(s + 1) * ffw])
        up = _dot(h, wff_ref[:, D_FF + s * ffw:D_FF + (s + 1) * ffw])
        act = (gate * jax.nn.sigmoid(gate) * up).astype(BF16)
        part = _dot(act, wfd_ref[s * ffw:(s + 1) * ffw, :])
        acc = part if acc is None else acc + part
    o_ref[...] = _rmsnorm(xn + g2_ref[0] * acc, fg_ref[...])


def _tail(y2d, f2d, u2d, x2d, g1, sh2, sc2, g2, wso, wfo, wo, n2, wff, wfd, fg, tm, rows_per_mod):
    m = x2d.shape[0]
    bpm = rows_per_mod // tm

    def const(shape):
        return pl.BlockSpec(shape, lambda i: (0, 0), pipeline_mode=pl.Buffered(1))

    def mod():
        return pl.BlockSpec((1, 1, D_MODEL), lambda i: (i // bpm, 0, 0))

    return pl.pallas_call(
        _tail_kernel,
        grid=(m // tm,),
        in_specs=[pl.BlockSpec((tm, D_INNER), lambda i: (i, 0)),
                  pl.BlockSpec((tm, D_MODEL), lambda i: (i, 0)),
                  pl.BlockSpec((tm, D_MODEL), lambda i: (i, U_GATE // D_MODEL)),
                  pl.BlockSpec((tm, D_MODEL), lambda i: (i, U_GATE // D_MODEL + 1)),
                  pl.BlockSpec((tm, D_MODEL), lambda i: (i, 0)),
                  mod(), mod(), mod(), mod(),
                  const((D_INNER, D_MODEL)), const((D_MODEL, D_MODEL)), const((D_MODEL, D_MODEL)),
                  const((1, D_MODEL)),
                  const((D_MODEL, 2 * D_FF)), const((D_FF, D_MODEL)),
                  const((1, D_MODEL))],
        out_specs=pl.BlockSpec((tm, D_MODEL), lambda i: (i, 0)),
        out_shape=jax.ShapeDtypeStruct((m, D_MODEL), F32),
        compiler_params=pltpu.CompilerParams(dimension_semantics=("arbitrary",),
                                             vmem_limit_bytes=VMEM_LIMIT),
        name="tail",
    )(y2d, f2d, u2d, u2d, x2d, g1, sh2, sc2, g2, wso, wfo, wo, n2, wff, wfd, fg)


def _dt_lanes(a):
    lead = a.shape[:-1]
    a = a.reshape(lead + (2, N_GROUPS, HEADS_PER_GROUP))
    a = jnp.swapaxes(a, -3, -2).reshape(lead + (2 * N_HEADS,))
    return jnp.concatenate([a, jnp.zeros(lead + (LANE - 2 * N_HEADS,), a.dtype)], axis=-1)


def _block(x, c, ctx, c_ctx, w_ada, b_ada, norm1_g, w_in, conv_w, conv_b, dt_bias, a_log, d_skip,
           ssd_norm_g, w_ssd_out, w_fft_out, w_o, norm2_g, w_ffn_in, w_ffn_out, final_g,
           proj_tm, prep_tl, ssd_tl, fft_tm, tail_tm):
    b, l, d = x.shape
    ctx_len = ctx.shape[1]
    assert d == D_MODEL and l % (GRID_W * 2) == 0 and ctx_len % CHUNK == 0 and b + 1 <= SUBLANE

    w_in_t = w_in[0].T
    w_main_t = w_in_t.astype(BF16)
    w_dt = _dt_lanes(w_in_t[REF_DT:REF_Z].T).T
    w_dt_hi = w_dt.astype(BF16)
    w_dt_lo = (w_dt - w_dt_hi.astype(F32)).astype(BF16)
    w_dt2 = jnp.concatenate([w_dt_hi, w_dt_lo], axis=0)
    conv_w_p = jnp.concatenate([conv_w[0], jnp.zeros((SUBLANE - D_CONV, CONV_DIM), F32)], axis=0)
    conv_b_p = conv_b[0].reshape(1, CONV_DIM)
    bias_v = _dt_lanes(dt_bias[0].reshape(-1)).reshape(1, LANE)
    alog_v = _dt_lanes(a_log[0].reshape(-1)).reshape(1, LANE)
    dskip_v = jnp.repeat(d_skip[0], HEAD_DIM).reshape(1, D_INNER)
    gain_v = ssd_norm_g[0].reshape(1, D_INNER)
    wso = w_ssd_out[0].astype(BF16)
    wfo = w_fft_out[0].astype(BF16)
    wo = w_o[0].astype(BF16)
    wff = w_ffn_in[0].astype(BF16)
    wfd = w_ffn_out[0].astype(BF16)

    cvecs = jnp.concatenate([c, c_ctx[None, :], jnp.zeros((SUBLANE - b - 1, D_MODEL), F32)], axis=0)
    mods = _mods(cvecs, b + 1, w_ada[0], b_ada[0])
    lat = mods[:b].reshape(b, 6, 1, D_MODEL)
    sh1, sc1, g1, sh2, sc2, g2 = (lat[:, k] for k in range(6))
    cm = mods[b].reshape(6, 1, 1, D_MODEL)

    x2d = x.reshape(b * l, D_MODEL)
    n1 = norm1_g[0].reshape(1, D_MODEL)
    u2d, dt_raw = _inproj(x2d, sh1, sc1, n1, w_main_t, w_dt2, U_WIDTH, proj_tm, l)
    u = u2d.reshape(b, l, U_WIDTH)
    cu2d, cdt_raw = _inproj(ctx.reshape(b * ctx_len, D_MODEL), cm[0], cm[1], n1, w_main_t, w_dt2,
                            CONV_DIM, b * ctx_len, b * ctx_len)
    cu = cu2d.reshape(b, ctx_len, CONV_DIM)

    rows = _dtprep(dt_raw.reshape(b, l, LANE), bias_v, alog_v, prep_tl)
    crows = _dtprep(cdt_raw.reshape(b, ctx_len, LANE), bias_v, alog_v, ctx_len)
    y_ssd = _ssd(u, rows, cu, crows, conv_w_p, conv_b_p, dskip_v, gain_v, ssd_tl)

    f_mix = _fourier(u, fft_tm)

    out = _tail(y_ssd.reshape(b * l, D_INNER), f_mix.reshape(b * l, D_MODEL), u2d, x2d,
                g1, sh2, sc2, g2, wso, wfo, wo, norm2_g[0].reshape(1, D_MODEL), wff, wfd,
                final_g.reshape(1, D_MODEL), tail_tm, l)
    return out.reshape(b, l, D_MODEL)


def kernel(x, c, ctx, c_ctx, w_ada, b_ada, norm1_g, w_in, conv_w, conv_b, dt_bias, a_log, d_skip,
           ssd_norm_g, w_ssd_out, w_fft_out, w_o, norm2_g, w_ffn_in, w_ffn_out, final_g):
    return _block(x, c, ctx, c_ctx, w_ada, b_ada, norm1_g, w_in, conv_w, conv_b, dt_bias, a_log, d_skip,
                  ssd_norm_g, w_ssd_out, w_fft_out, w_o, norm2_g, w_ffn_in, w_ffn_out, final_g,
                  proj_tm=2048, prep_tl=1024, ssd_tl=4096, fft_tm=512, tail_tm=512)
```

```python
import functools
import math

import jax
import jax.numpy as jnp
import numpy as np
from jax import lax
from jax.experimental import pallas as pl
from jax.experimental.pallas import tpu as pltpu

F32 = jnp.float32
BF16 = jnp.bfloat16

D_MODEL = 1024
EPS = 1e-6
GRID_W = 64

N_GROUPS = 8
HEADS_PER_GROUP = 4
HEAD_DIM = 64
N_HEADS = N_GROUPS * HEADS_PER_GROUP
D_STATE = 128
D_INNER = N_HEADS * HEAD_DIM
GROUP_W = HEADS_PER_GROUP * HEAD_DIM
GS = N_GROUPS * D_STATE
CONV_DIM = D_INNER + 2 * GS
D_CONV = 5
CHUNK = 128

FFT_GROUPS = 8
FFT_GROUP_DIM = 128
D_FF = 2816

REF_DT = CONV_DIM
REF_Z = REF_DT + 2 * N_HEADS

U_B = D_INNER
U_C = U_B + GS
U_Z = CONV_DIM
U_FFT = U_Z + D_INNER
U_GATE = U_FFT + D_MODEL
U_WIDTH = U_GATE + 2 * D_MODEL

LANE = 128
SUBLANE = 8
HALO = 16
VMEM_LIMIT = 56 * 1024 * 1024

N_DH = 2 * HEADS_PER_GROUP
R_CUM, R_NCX, R_EY, R_WX, R_DIAG = 0, 8, 16, 24, 32
ROWS_PER_GROUP = 40
N_SLAB = (GROUP_W + 2 * D_STATE) // LANE
DT_FLOOR = 1e-37
LOG2E = math.log2(math.e)
N_CUM_TERMS = 3
FWD_PER_TRIP = 4
BWD_PER_TRIP = 8
FWD_WEAVE = "C0 C1 P0 P1"
BWD_WEAVE = "P0 C0"
K_ARGS = 2 * N_CUM_TERMS * N_DH
FFT_PITCH = GRID_W + 4


def _dot(a, b):
    return jnp.dot(a, b, preferred_element_type=F32)


def _dot_tn(a, b):
    return lax.dot_general(a, b, (((0,), (0,)), ((), ())), preferred_element_type=F32)


def _dot_nt(a, b):
    return lax.dot_general(a, b, (((1,), (1,)), ((), ())), preferred_element_type=F32)


def _rmsnorm(x, g):
    return x * lax.rsqrt(jnp.mean(x * x, axis=-1, keepdims=True) + EPS) * g


def _mods_kernel(ct_ref, w_ref, b_ref, o_ref, *, n_vecs):
    s = ct_ref[...]
    s = s * jax.nn.sigmoid(s)
    w = w_ref[...]
    rows = [jnp.sum(w * s[:, r:r + 1], axis=0, keepdims=True) for r in range(n_vecs)]
    rows.append(jnp.zeros((SUBLANE - n_vecs, w.shape[1]), F32))
    o_ref[...] = jnp.concatenate(rows, axis=0) + b_ref[...]


def _mods(cvecs, n_vecs, w_ada, b_ada):
    n = w_ada.shape[1]
    tn = 1536
    return pl.pallas_call(
        functools.partial(_mods_kernel, n_vecs=n_vecs),
        grid=(n // tn,),
        in_specs=[pl.BlockSpec((D_MODEL, SUBLANE), lambda j: (0, 0)),
                  pl.BlockSpec((D_MODEL, tn), lambda j: (0, j)),
                  pl.BlockSpec((1, tn), lambda j: (0, j))],
        out_specs=pl.BlockSpec((SUBLANE, tn), lambda j: (0, j)),
        out_shape=jax.ShapeDtypeStruct((SUBLANE, n), F32),
        compiler_params=pltpu.CompilerParams(dimension_semantics=("arbitrary",),
                                             vmem_limit_bytes=VMEM_LIMIT),
        name="mods",
    )(cvecs.T, w_ada, b_ada.reshape(1, n))


def _inproj_kernel(x_ref, sh_ref, sc_ref, g_ref, w_ref, wdt_ref, u_ref, dt_ref, h_ref):
    @pl.when(pl.program_id(1) == 0)
    def _():
        h = _rmsnorm(x_ref[...], g_ref[...]) * (1.0 + sc_ref[0]) + sh_ref[0]
        hb = h.astype(BF16)
        h_ref[...] = hb
        hl = (h - hb.astype(F32)).astype(BF16)
        a = _dot_nt(hb, wdt_ref[...])
        dt_ref[...] = a[:, :LANE] + a[:, LANE:] + _dot_nt(hl, wdt_ref[:LANE, :])

    u_ref[...] = _dot_nt(h_ref[...], w_ref[...]).astype(BF16)


def _inproj(x2d, shift, scale, gain, w_t, w_dt_t, n_cols, tm, rows_per_mod):
    m = x2d.shape[0]
    tn = 1024
    bpm = rows_per_mod // tm

    def w_rows(i, j):
        return pl.multiple_of(j * tn + jnp.where(j * tn >= REF_DT, REF_Z - REF_DT, 0), 2 * HALO), 0

    return pl.pallas_call(
        _inproj_kernel,
        grid=(m // tm, n_cols // tn),
        in_specs=[pl.BlockSpec((tm, D_MODEL), lambda i, j: (i, 0)),
                  pl.BlockSpec((1, 1, D_MODEL), lambda i, j: (i // bpm, 0, 0)),
                  pl.BlockSpec((1, 1, D_MODEL), lambda i, j: (i // bpm, 0, 0)),
                  pl.BlockSpec((1, D_MODEL), lambda i, j: (0, 0)),
                  pl.BlockSpec((pl.Element(tn), pl.Element(D_MODEL)), w_rows),
                  pl.BlockSpec((2 * LANE, D_MODEL), lambda i, j: (0, 0))],
        out_specs=[pl.BlockSpec((tm, tn), lambda i, j: (i, j)),
                   pl.BlockSpec((tm, LANE), lambda i, j: (i, 0))],
        out_shape=[jax.ShapeDtypeStruct((m, n_cols), BF16),
                   jax.ShapeDtypeStruct((m, LANE), F32)],
        scratch_shapes=[pltpu.VMEM((tm, D_MODEL), BF16)],
        compiler_params=pltpu.CompilerParams(dimension_semantics=("arbitrary", "arbitrary"),
                                             vmem_limit_bytes=VMEM_LIMIT),
        name="inproj",
    )(x2d, shift, scale, gain, w_t, w_dt_t)


def _dtprep_kernel(raw_ref, bias_ref, alog_ref, row_ref, *, n_chunks):
    a = -jnp.exp(alog_ref[...])
    bias = bias_ref[...]
    ii = lax.broadcasted_iota(jnp.int32, (CHUNK, CHUNK), 0)
    jj = lax.broadcasted_iota(jnp.int32, (CHUNK, CHUNK), 1)
    tril = (jj <= ii).astype(BF16)
    triu = (jj >= ii).astype(BF16)
    lane = lax.broadcasted_iota(jnp.int32, (1, LANE), 1)
    is_fwd = (lane % N_DH) < HEADS_PER_GROUP

    def tri_sum(tri, terms):
        s = _dot(tri, terms)
        return s[:, :LANE] + s[:, LANE:2 * LANE] + s[:, 2 * LANE:]

    for c in range(n_chunks):
        sl = slice(c * CHUNK, (c + 1) * CHUNK)
        dt = jax.nn.softplus(raw_ref[0, sl, :] + bias)
        dta = dt * a
        terms = jnp.concatenate(_split_terms(dta, N_CUM_TERMS), axis=1).astype(BF16)
        cf = tri_sum(tril, terms)
        cb = tri_sum(triu, terms)
        cum = jnp.where(is_fwd, cf, cb)
        tot = jnp.where(is_fwd, cf[CHUNK - 1:CHUNK, :], cb[0:1, :])
        wx = dt * jnp.exp(tot - cum)
        ey = jnp.exp(cum)
        ncx = jnp.log(jnp.maximum(dt, DT_FLOOR)) - cum
        diag = jnp.log(dt + pltpu.roll(dt, LANE - HEADS_PER_GROUP, 1))
        for r_off, val in ((R_CUM, cum * LOG2E), (R_NCX, ncx * LOG2E), (R_EY, ey), (R_WX, wx),
                           (R_DIAG, diag * LOG2E)):
            vt = val.T
            for g in range(N_GROUPS):
                r0 = g * ROWS_PER_GROUP + r_off
                row_ref[0, r0:r0 + N_DH, sl] = vt[g * N_DH:(g + 1) * N_DH, :]


def _dtprep(raw, bias, alog, tl):
    b, l, _ = raw.shape
    return pl.pallas_call(
        functools.partial(_dtprep_kernel, n_chunks=tl // CHUNK),
        grid=(b, l // tl),
        in_specs=[pl.BlockSpec((1, tl, LANE), lambda i, j: (i, j, 0)),
                  pl.BlockSpec((1, LANE), lambda i, j: (0, 0)),
                  pl.BlockSpec((1, LANE), lambda i, j: (0, 0))],
        out_specs=pl.BlockSpec((1, N_GROUPS * ROWS_PER_GROUP, tl), lambda i, j: (i, 0, j)),
        out_shape=jax.ShapeDtypeStruct((b, N_GROUPS * ROWS_PER_GROUP, l), F32),
        compiler_params=pltpu.CompilerParams(dimension_semantics=("arbitrary", "arbitrary"),
                                             vmem_limit_bytes=VMEM_LIMIT),
        name="dtprep",
    )(raw, bias, alog)


def _split_terms(x, n):
    terms = []
    for _ in range(n - 1):
        t = x.astype(BF16).astype(F32)
        terms.append(t)
        x = x - t
    terms.append(x)
    return terms


def _scale_rows(rows):
    return jnp.concatenate(_split_terms(rows[R_EY:R_EY + N_DH], 2) + _split_terms(rows[R_WX:R_WX + N_DH], 2),
                           axis=0).astype(BF16)


def _conv_silu(ext_refs, s, p0, cw, cbias):
    sl = slice(s * LANE, (s + 1) * LANE)
    acc = cbias[:, sl].astype(BF16)
    for k in range(D_CONV):
        pair = p0 + (HALO + k - D_CONV // 2 - k % 2) // 2
        tap = pltpu.bitcast(ext_refs[k % 2][s, pl.ds(pair, CHUNK // 2), :], BF16)
        acc = acc + cw[k:k + 1, sl].astype(BF16) * tap
    return acc * jax.nn.sigmoid(acc)


def _ssd_kernel(x_ref, b_ref, c_ref, xp_ref, bp_ref, cp_ref, xn_ref, bn_ref, cn_ref, z_ref, row_ref,
                cx_ref, cb_ref, crow_ref, cwx_ref, cwb_ref, cwc_ref, cbx_ref, cbb_ref, cbc_ref,
                dskip_ref, gain_ref, onesel_ref, sel_ref, out_ref,
                cache_ref, yacc_ref, sf_ref, sb_ref, ext0_ref, ext1_ref, args0_ref, args1_ref, ew0_ref, ew1_ref,
                st0_ref, st1_ref, *, tl, n_blk, ctx_len):
    p = pl.program_id(2)
    i = pl.program_id(3)
    n_ch = tl // CHUNK
    args_bufs = (args0_ref, args1_ref)
    ew_bufs = (ew0_ref, ew1_ref)
    st_bufs = (st0_ref, st1_ref)
    ext_refs = (ext0_ref, ext1_ref)
    cw = jnp.concatenate([cwx_ref[...], cwb_ref[...], cwc_ref[...]], axis=1)
    cbias = jnp.concatenate([cbx_ref[...], cbb_ref[...], cbc_ref[...]], axis=1)

    def slabs_of(x, bm, cm):
        return [x[:, :LANE], x[:, LANE:], bm, cm]

    def slab_refs(xr, br, cr):
        return [(xr, 0), (xr, LANE), (br, 0), (cr, 0)]

    def fill_ext(src, first, last, n_rows):
        half = HALO // 2
        n_piece = n_rows // CHUNK

        def copy_piece(k, carry):
            r = pl.multiple_of(k * CHUNK, CHUNK)
            q = pl.multiple_of(k * (CHUNK // 2), CHUNK // 2)
            for s, (ref, lane0) in enumerate(src):
                ext0_ref[s, pl.ds(half + q, CHUNK // 2), :] = pltpu.bitcast(
                    ref[0, pl.ds(r, CHUNK), lane0:lane0 + LANE], jnp.uint32)
            return carry

        def shift_piece(k, carry):
            q = pl.multiple_of(k * (CHUNK // 2), CHUNK // 2)
            for s in range(len(src)):
                lo = ext0_ref[s, pl.ds(q, CHUNK // 2), :]
                hi = ext0_ref[s, pl.ds(q + 1, CHUNK // 2), :]
                ext1_ref[s, pl.ds(q, CHUNK // 2), :] = (lo >> 16) | (hi << 16)
            return carry

        for s in range(len(src)):
            ext0_ref[s, 0:half, :] = pltpu.bitcast(first[s], jnp.uint32)
            ext0_ref[s, half + n_rows // 2:2 * half + n_rows // 2, :] = pltpu.bitcast(last[s], jnp.uint32)
        lax.fori_loop(0, n_piece, copy_piece, 0, unroll=2)
        lax.fori_loop(0, n_piece, shift_piece, 0, unroll=2)
        for s in range(len(src)):
            q = n_rows // 2
            ext1_ref[s, q:q + half, :] = (ext0_ref[s, q:q + half, :] >> 16) | (ext0_ref[s, q + 1:q + half + 1, :] << 16)

    def row_of(c):
        return c * CHUNK if isinstance(c, int) else pl.multiple_of(c * CHUNK, CHUNK)

    def expand(rowt, d):
        return _dot_tn(_scale_rows(rowt), sel_ref[d])

    def run_pipelined(prepare_parts, consume_parts, weave, order, per_trip):
        def step(k, buf, with_next):
            handed = None
            for item in weave.split():
                n = int(item[1:])
                if item[0] == "P":
                    if with_next:
                        prepare_parts[n](order(k + 1), 1 - buf)
                else:
                    handed = consume_parts[n](order(k), buf, handed)

        for part in prepare_parts:
            part(order(0), 0)

        def trip(t, carry):
            for d in range(per_trip):
                step(per_trip * t + d, d % 2, True)
            return carry

        lax.fori_loop(0, n_ch // per_trip - 1, trip, 0)
        for k in range(n_ch - per_trip, n_ch):
            step(k, k % 2, k + 1 < n_ch)

    @pl.when(jnp.logical_and(p == 0, i == 0))
    def _init_from_context():
        zpad = [jnp.zeros((HALO, LANE), BF16)] * (N_SLAB - 1)
        fill_ext(slab_refs(cx_ref, cb_ref, None)[:N_SLAB - 1], zpad, zpad, ctx_len)
        n_cc = ctx_len // CHUNK
        sf = jnp.zeros((D_STATE, GROUP_W), F32)
        st_b, ey_b = [], []
        for cc in range(n_cc):
            slabs = [_conv_silu(ext_refs, s, cc * CHUNK // 2, cw, cbias) for s in range(N_SLAB - 1)]
            xs = jnp.concatenate(slabs[:2], axis=1).astype(F32)
            bm = slabs[2]
            rowt = crow_ref[0, :, cc * CHUNK:(cc + 1) * CHUNK]
            ewf = expand(rowt, 0)
            ewb = expand(rowt, 1)
            sf = sf * ewf[CHUNK - 1:CHUNK, :GROUP_W] + _dot_tn(bm, (xs * ewf[:, GROUP_W:]).astype(BF16))
            st_b.append(_dot_tn(bm, (xs * ewb[:, GROUP_W:]).astype(BF16)))
            ey_b.append(ewb[0:1, :GROUP_W])
        sb = jnp.zeros((D_STATE, GROUP_W), F32)
        for cc in reversed(range(n_cc)):
            sb = sb * ey_b[cc] + st_b[cc]
        sf_ref[...] = sf
        sb_ref[...] = sb

    @pl.when(p == 0)
    def _forward():
        blk = i
        base = blk * tl
        has_prev = blk > 0
        has_next = blk < n_blk - 1
        fill_ext(slab_refs(x_ref, b_ref, c_ref),
                 [jnp.where(has_prev, v, jnp.zeros_like(v)) for v in slabs_of(xp_ref[0], bp_ref[0], cp_ref[0])],
                 [jnp.where(has_next, v, jnp.zeros_like(v)) for v in slabs_of(xn_ref[0], bn_ref[0], cn_ref[0])],
                 tl)
        ii = lax.broadcasted_iota(jnp.int32, (CHUNK, CHUNK), 0)
        jj = lax.broadcasted_iota(jnp.int32, (CHUNK, CHUNK), 1)
        lower = jj < ii
        upper = jj > ii

        def prepare_conv(c, buf):
            r = row_of(c)
            p0 = c * (CHUNK // 2) if isinstance(c, int) else pl.multiple_of(c * (CHUNK // 2), CHUNK // 2)
            slabs = [_conv_silu(ext_refs, s, p0, cw, cbias) for s in range(N_SLAB)]
            for s in range(N_SLAB):
                cache_ref[pl.ds(base + r, CHUNK), s * LANE:(s + 1) * LANE] = slabs[s]
            yacc_ref[pl.ds(base + r, CHUNK), :] = jnp.concatenate(slabs[:2], axis=1).astype(F32) * dskip_ref[...]

        def prepare_scales(c, buf):
            rowt = row_ref[0, :, pl.ds(row_of(c), CHUNK)]
            onesel = onesel_ref[...]
            ones = jnp.ones((N_DH, CHUNK), F32)
            at = jnp.concatenate(_split_terms(rowt[R_CUM:R_CUM + N_DH], N_CUM_TERMS) + [ones] * N_CUM_TERMS,
                                 axis=0).astype(BF16)
            wide = [jnp.concatenate([t] * N_DH, axis=1) * onesel
                    for t in _split_terms(rowt[R_NCX:R_NCX + N_DH], N_CUM_TERMS)]
            bmat = jnp.concatenate([onesel] * N_CUM_TERMS + wide, axis=0).astype(BF16)
            args_bufs[buf][...] = _dot_tn(at, bmat)
            ew_bufs[buf][...] = expand(rowt, 0)

        def consume_state(c, buf, _):
            r = row_of(c)
            xbc_b = cache_ref[pl.ds(base + r, CHUNK), :]
            xs_b = xbc_b[:, :GROUP_W]
            bm = xbc_b[:, GROUP_W:GROUP_W + D_STATE]
            cm = xbc_b[:, GROUP_W + D_STATE:]
            cbm = _dot_nt(cm, bm)
            ef = ew_bufs[buf][:, :GROUP_W]
            wxf = ew_bufs[buf][:, GROUP_W:]
            sf = sf_ref[...]
            y_off = _dot(cm, sf.astype(BF16)) * ef
            sf_ref[...] = sf * ef[CHUNK - 1:CHUNK, :] + _dot_tn(bm, (xs_b.astype(F32) * wxf).astype(BF16))
            return r, xs_b, cbm, y_off

        def consume_decay(c, buf, handed):
            r, xs_b, cbm, y_off = handed
            yd = []
            for j in range(HEADS_PER_GROUP):
                argf = args_bufs[buf][:, j * CHUNK:(j + 1) * CHUNK]
                argb = args_bufs[buf][:, (HEADS_PER_GROUP + j) * CHUNK:(HEADS_PER_GROUP + j + 1) * CHUNK]
                diag = row_ref[0, R_DIAG + j:R_DIAG + j + 1, pl.ds(r, CHUNK)]
                arg = jnp.where(lower, argf, jnp.where(upper, argb, diag))
                wmat = (cbm * jnp.exp2(arg)).astype(BF16)
                yd.append(_dot(wmat, xs_b[:, j * HEAD_DIM:(j + 1) * HEAD_DIM]))
            yacc_ref[pl.ds(base + r, CHUNK), :] += jnp.concatenate(yd, axis=1) + y_off

        run_pipelined([prepare_conv, prepare_scales], [consume_state, consume_decay], FWD_WEAVE,
                      lambda k: k, FWD_PER_TRIP)

    @pl.when(p == 1)
    def _backward():
        blk = n_blk - 1 - i
        base = blk * tl

        def prepare(c, buf):
            r = row_of(c)
            ew = expand(row_ref[0, :, pl.ds(r, CHUNK)], 1)
            ew_bufs[buf][...] = ew
            xs = cache_ref[pl.ds(base + r, CHUNK), 0:GROUP_W].astype(F32)
            bm = cache_ref[pl.ds(base + r, CHUNK), GROUP_W:GROUP_W + D_STATE]
            st_bufs[buf][...] = _dot_tn(bm, (xs * ew[:, GROUP_W:]).astype(BF16))

        def consume(c, buf, _):
            r = row_of(c)
            cm = cache_ref[pl.ds(base + r, CHUNK), GROUP_W + D_STATE:]
            eb = ew_bufs[buf][:, :GROUP_W]
            sb = sb_ref[...]
            y = yacc_ref[pl.ds(base + r, CHUNK), :] + _dot(cm, sb.astype(BF16)) * eb
            sb_ref[...] = sb * eb[0:1, :] + st_bufs[buf][...]
            z = z_ref[0, pl.ds(r, CHUNK), :].astype(F32)
            y = y * (z * jax.nn.sigmoid(z))
            out_ref[0, pl.ds(r, CHUNK), :] = _rmsnorm(y, gain_ref[...]).astype(BF16)

        run_pipelined([prepare], [consume], BWD_WEAVE, lambda k: n_ch - 1 - k, BWD_PER_TRIP)


def _ssd_selectors():
    onesel = (np.arange(N_DH * CHUNK)[None, :] // CHUNK == np.arange(N_DH)[:, None]).astype(np.float32)
    sel = np.zeros((2, 4 * N_DH, 2 * GROUP_W), np.float32)
    for d in range(2):
        for j in range(HEADS_PER_GROUP):
            dh = d * HEADS_PER_GROUP + j
            for term in range(2):
                sel[d, term * N_DH + dh, j * HEAD_DIM:(j + 1) * HEAD_DIM] = 1.0
                sel[d, (2 + term) * N_DH + dh, GROUP_W + j * HEAD_DIM:GROUP_W + (j + 1) * HEAD_DIM] = 1.0
    return jnp.asarray(onesel, F32), jnp.asarray(sel, BF16)


def _ssd(u, rows, cu, crows, conv_w, conv_b, dskip, gain, tl):
    b, l, _ = u.shape
    ctx_len = cu.shape[1]
    assert ctx_len <= tl
    n_blk = l // tl
    hb = tl // HALO
    last_halo = l // HALO - 1
    assert (tl // CHUNK) % FWD_PER_TRIP == 0 and (tl // CHUNK) % BWD_PER_TRIP == 0
    onesel, sel = _ssd_selectors()
    xg, bg, cg = 0, U_B // D_STATE, U_C // D_STATE

    def in_blk(p, i):
        return i * (1 - p) + (n_blk - 1) * p

    def seq_blk(p, i):
        return i * (1 - p) + (n_blk - 1 - i) * p

    def out_blk(p, i):
        return (n_blk - 1) * (1 - p) + (n_blk - 1 - i) * p

    def prev_halo(p, i):
        return jnp.maximum(in_blk(p, i) * hb - 1, 0)

    def next_halo(p, i):
        return jnp.minimum((in_blk(p, i) + 1) * hb, last_halo)

    def xbc_specs(rows_blk, row_index):
        return [pl.BlockSpec((1, rows_blk, GROUP_W), lambda bb, g, p, i: (bb, row_index(p, i), xg + g)),
                pl.BlockSpec((1, rows_blk, D_STATE), lambda bb, g, p, i: (bb, row_index(p, i), bg + g)),
                pl.BlockSpec((1, rows_blk, D_STATE), lambda bb, g, p, i: (bb, row_index(p, i), cg + g))]

    def param_specs(n_rows):
        return [pl.BlockSpec((n_rows, GROUP_W), lambda bb, g, p, i: (0, xg + g)),
                pl.BlockSpec((n_rows, D_STATE), lambda bb, g, p, i: (0, bg + g)),
                pl.BlockSpec((n_rows, D_STATE), lambda bb, g, p, i: (0, cg + g))]

    grid = (b, N_GROUPS, 2, n_blk)
    in_specs = (
        xbc_specs(tl, in_blk) + xbc_specs(HALO, prev_halo) + xbc_specs(HALO, next_halo)
        + [pl.BlockSpec((1, tl, GROUP_W), lambda bb, g, p, i: (bb, out_blk(p, i), U_Z // GROUP_W + g)),
           pl.BlockSpec((1, ROWS_PER_GROUP, tl), lambda bb, g, p, i: (bb, g, seq_blk(p, i)))]
        + xbc_specs(ctx_len, lambda p, i: 0)[:2]
        + [pl.BlockSpec((1, ROWS_PER_GROUP, ctx_len), lambda bb, g, p, i: (bb, g, 0))]
        + param_specs(SUBLANE) + param_specs(1)
        + [pl.BlockSpec((1, GROUP_W), lambda bb, g, p, i: (0, g)),
           pl.BlockSpec((1, GROUP_W), lambda bb, g, p, i: (0, g)),
           pl.BlockSpec(onesel.shape, lambda bb, g, p, i: (0, 0)),
           pl.BlockSpec(sel.shape, lambda bb, g, p, i: (0, 0, 0))])
    return pl.pallas_call(
        functools.partial(_ssd_kernel, tl=tl, n_blk=n_blk, ctx_len=ctx_len),
        grid=grid,
        in_specs=in_specs,
        out_specs=pl.BlockSpec((1, tl, GROUP_W), lambda bb, g, p, i: (bb, out_blk(p, i), g)),
        out_shape=jax.ShapeDtypeStruct((b, l, D_INNER), BF16),
        scratch_shapes=[pltpu.VMEM((l, N_SLAB * LANE), BF16),
                        pltpu.VMEM((l, GROUP_W), F32),
                        pltpu.VMEM((D_STATE, GROUP_W), F32),
                        pltpu.VMEM((D_STATE, GROUP_W), F32),
                        pltpu.VMEM((N_SLAB, (tl + 2 * HALO) // 2, LANE), jnp.uint32),
                        pltpu.VMEM((N_SLAB, (tl + 2 * HALO) // 2, LANE), jnp.uint32),
                        pltpu.VMEM((CHUNK, N_DH * CHUNK), F32),
                        pltpu.VMEM((CHUNK, N_DH * CHUNK), F32),
                        pltpu.VMEM((CHUNK, 2 * GROUP_W), F32),
                        pltpu.VMEM((CHUNK, 2 * GROUP_W), F32),
                        pltpu.VMEM((D_STATE, GROUP_W), F32),
                        pltpu.VMEM((D_STATE, GROUP_W), F32)],
        compiler_params=pltpu.CompilerParams(
            dimension_semantics=("arbitrary", "arbitrary", "arbitrary", "arbitrary"),
            vmem_limit_bytes=VMEM_LIMIT),
        name="ssd",
    )(u, u, u, u, u, u, u, u, u, u, rows, cu, cu, crows,
      conv_w, conv_w, conv_w, conv_b, conv_b, conv_b, dskip, gain, onesel, sel)


def _dft_mats():
    def cs(n):
        k = np.arange(n)
        ang = 2.0 * np.pi * np.outer(k, k) / n
        return np.cos(ang), np.sin(ang)

    c3, s3 = cs(FFT_GROUP_DIM)
    w1 = np.concatenate([c3, -s3], axis=1)
    c2, s2 = cs(GRID_W)
    m2 = np.block([[c2, s2], [-s2, c2]])
    return w1, m2


def _fft12_kernel(f_ref, w1_ref, m2_ref, o_ref, *, tm):
    f = f_ref[0]
    w1 = w1_ref[...]
    ps, qs = [], []
    for g in range(FFT_GROUPS):
        pq = _dot(f[:, g * FFT_GROUP_DIM:(g + 1) * FFT_GROUP_DIM], w1)
        ps.append(pq[:, :FFT_GROUP_DIM])
        qs.append(pq[:, FFT_GROUP_DIM:])
    pr = jnp.concatenate(ps, axis=1).astype(BF16)
    qr = jnp.concatenate(qs, axis=1).astype(BF16)
    m2 = m2_ref[...]
    pad = jnp.zeros((FFT_PITCH - GRID_W, D_MODEL), jnp.uint32)
    for r in range(tm // GRID_W):
        sl = slice(r * GRID_W, (r + 1) * GRID_W)
        o = _dot(m2, jnp.concatenate([pr[sl], qr[sl]], axis=0))
        r0 = r * FFT_PITCH
        o_ref[0, r0:r0 + GRID_W, :] = _pack_complex(o[:GRID_W], o[GRID_W:])
        o_ref[0, r0 + GRID_W:r0 + FFT_PITCH, :] = pad


def _pack_complex(re, im):
    re_bits = pltpu.bitcast(re.astype(BF16).astype(F32), jnp.uint32)
    im_bits = pltpu.bitcast(im.astype(BF16).astype(F32), jnp.uint32)
    return (re_bits >> 16) | (im_bits & jnp.uint32(0xFFFF0000))


def _unpack_complex(word):
    re = pltpu.bitcast(word << 16, F32)
    im = pltpu.bitcast(word & jnp.uint32(0xFFFF0000), F32)
    return re, im


def _fft3_kernel(ri_ref, m3_ref, o_ref, scr_ref, *, rows):
    m3 = m3_ref[...]

    def column_pair(t, carry):
        w = 2 * t
        cols = [jnp.concatenate(_unpack_complex(ri_ref[0, pl.ds(w + dw, rows, stride=FFT_PITCH), :]), axis=0)
                for dw in range(2)]
        o = _dot(m3, jnp.concatenate(cols, axis=1).astype(BF16))
        for dw in range(2):
            scr_ref[pl.ds(w + dw, rows, stride=FFT_PITCH), :] = o[:, dw * LANE:(dw + 1) * LANE]
        return carry

    lax.fori_loop(0, GRID_W // 2, column_pair, 0, unroll=4)
    for k in range(rows):
        o_ref[0, k * GRID_W:(k + 1) * GRID_W, :] = scr_ref[k * FFT_PITCH:k * FFT_PITCH + GRID_W, :].astype(BF16)


def _fourier(u, tm):
    b, l, _ = u.shape
    rows = l // GRID_W
    w1, m2 = _dft_mats()
    ri = pl.pallas_call(
        functools.partial(_fft12_kernel, tm=tm),
        grid=(b, l // tm),
        in_specs=[pl.BlockSpec((1, tm, D_MODEL), lambda i, j: (i, j, U_FFT // D_MODEL)),
                  pl.BlockSpec((FFT_GROUP_DIM, 2 * FFT_GROUP_DIM), lambda i, j: (0, 0)),
                  pl.BlockSpec((2 * GRID_W, 2 * GRID_W), lambda i, j: (0, 0))],
        out_specs=pl.BlockSpec((1, tm // GRID_W * FFT_PITCH, D_MODEL), lambda i, j: (i, j, 0)),
        out_shape=jax.ShapeDtypeStruct((b, rows * FFT_PITCH, D_MODEL), jnp.uint32),
        compiler_params=pltpu.CompilerParams(dimension_semantics=("arbitrary", "arbitrary"),
                                             vmem_limit_bytes=VMEM_LIMIT),
        name="fft12",
    )(u, jnp.asarray(w1, BF16), jnp.asarray(m2, BF16))

    k = np.arange(rows)
    ang = 2.0 * np.pi * np.outer(k, k) / rows
    scale = 1.0 / np.sqrt(float(rows * GRID_W * FFT_GROUP_DIM))
    m3 = np.concatenate([np.cos(ang), np.sin(ang)], axis=1) * scale
    return pl.pallas_call(
        functools.partial(_fft3_kernel, rows=rows),
        grid=(b, D_MODEL // LANE),
        in_specs=[pl.BlockSpec((1, rows * FFT_PITCH, LANE), lambda i, j: (i, 0, j)),
                  pl.BlockSpec((rows, 2 * rows), lambda i, j: (0, 0))],
        out_specs=pl.BlockSpec((1, l, LANE), lambda i, j: (i, 0, j)),
        out_shape=jax.ShapeDtypeStruct((b, l, D_MODEL), BF16),
        scratch_shapes=[pltpu.VMEM((rows * FFT_PITCH, LANE), F32)],
        compiler_params=pltpu.CompilerParams(dimension_semantics=("arbitrary", "arbitrary"),
                                             vmem_limit_bytes=VMEM_LIMIT),
        name="fft3",
    )(ri, jnp.asarray(m3, BF16))


FF_SPLIT = 2


def _tail_kernel(y_ref, f_ref, ga_ref, gb_ref, x_ref, g1_ref, sh2_ref, sc2_ref, g2_ref,
                 wso_ref, wfo_ref, wo_ref, n2_ref, wff_ref, wfd_ref, fg_ref, o_ref):
    a = _dot(y_ref[...], wso_ref[...])
    bq = _dot(f_ref[...], wfo_ref[...])
    merged = jax.nn.sigmoid(ga_ref[...].astype(F32)) * a + jax.nn.sigmoid(gb_ref[...].astype(F32)) * bq
    xn = x_ref[...] + g1_ref[0] * _dot(merged.astype(BF16), wo_ref[...])
    h = (_rmsnorm(xn, n2_ref[...]) * (1.0 + sc2_ref[0]) + sh2_ref[0]).astype(BF16)
    ffw = D_FF // FF_SPLIT
    acc = None
    for s in range(FF_SPLIT):
        gate = _dot(h, wff_ref[:, s * ffw:(s + 1) * ffw])
        up = _dot(h, wff_ref[:, D_FF + s * ffw:D_FF + (s + 1) * ffw])
        act = (gate * jax.nn.sigmoid(gate) * up).astype(BF16)
        part = _dot(act, wfd_ref[s * ffw:(s + 1) * ffw, :])
        acc = part if acc is None else acc + part
    o_ref[...] = _rmsnorm(xn + g2_ref[0] * acc, fg_ref[...])


def _tail(y2d, f2d, u2d, x2d, g1, sh2, sc2, g2, wso, wfo, wo, n2, wff, wfd, fg, tm, rows_per_mod):
    m = x2d.shape[0]
    bpm = rows_per_mod // tm

    def const(shape):
        return pl.BlockSpec(shape, lambda i: (0, 0), pipeline_mode=pl.Buffered(1))

    def mod():
        return pl.BlockSpec((1, 1, D_MODEL), lambda i: (i // bpm, 0, 0))

    return pl.pallas_call(
        _tail_kernel,
        grid=(m // tm,),
        in_specs=[pl.BlockSpec((tm, D_INNER), lambda i: (i, 0)),
                  pl.BlockSpec((tm, D_MODEL), lambda i: (i, 0)),
                  pl.BlockSpec((tm, D_MODEL), lambda i: (i, U_GATE // D_MODEL)),
                  pl.BlockSpec((tm, D_MODEL), lambda i: (i, U_GATE // D_MODEL + 1)),
                  pl.BlockSpec((tm, D_MODEL), lambda i: (i, 0)),
                  mod(), mod(), mod(), mod(),
                  const((D_INNER, D_MODEL)), const((D_MODEL, D_MODEL)), const((D_MODEL, D_MODEL)),
                  const((1, D_MODEL)),
                  const((D_MODEL, 2 * D_FF)), const((D_FF, D_MODEL)),
                  const((1, D_MODEL))],
        out_specs=pl.BlockSpec((tm, D_MODEL), lambda i: (i, 0)),
        out_shape=jax.ShapeDtypeStruct((m, D_MODEL), F32),
        compiler_params=pltpu.CompilerParams(dimension_semantics=("arbitrary",),
                                             vmem_limit_bytes=VMEM_LIMIT),
        name="tail",
    )(y2d, f2d, u2d, u2d, x2d, g1, sh2, sc2, g2, wso, wfo, wo, n2, wff, wfd, fg)


def _dt_lanes(a):
    lead = a.shape[:-1]
    a = a.reshape(lead + (2, N_GROUPS, HEADS_PER_GROUP))
    a = jnp.swapaxes(a, -3, -2).reshape(lead + (2 * N_HEADS,))
    return jnp.concatenate([a, jnp.zeros(lead + (LANE - 2 * N_HEADS,), a.dtype)], axis=-1)


def _block(x, c, ctx, c_ctx, w_ada, b_ada, norm1_g, w_in, conv_w, conv_b, dt_bias, a_log, d_skip,
           ssd_norm_g, w_ssd_out, w_fft_out, w_o, norm2_g, w_ffn_in, w_ffn_out, final_g,
           proj_tm, prep_tl, ssd_tl, fft_tm, tail_tm):
    b, l, d = x.shape
    ctx_len = ctx.shape[1]
    assert d == D_MODEL and l % (GRID_W * 2) == 0 and ctx_len % CHUNK == 0 and b + 1 <= SUBLANE

    w_in_t = w_in[0].T
    w_main_t = w_in_t.astype(BF16)
    w_dt = _dt_lanes(w_in_t[REF_DT:REF_Z].T).T
    w_dt_hi = w_dt.astype(BF16)
    w_dt_lo = (w_dt - w_dt_hi.astype(F32)).astype(BF16)
    w_dt2 = jnp.concatenate([w_dt_hi, w_dt_lo], axis=0)
    conv_w_p = jnp.concatenate([conv_w[0], jnp.zeros((SUBLANE - D_CONV, CONV_DIM), F32)], axis=0)
    conv_b_p = conv_b[0].reshape(1, CONV_DIM)
    bias_v = _dt_lanes(dt_bias[0].reshape(-1)).reshape(1, LANE)
    alog_v = _dt_lanes(a_log[0].reshape(-1)).reshape(1, LANE)
    dskip_v = jnp.repeat(d_skip[0], HEAD_DIM).reshape(1, D_INNER)
    gain_v = ssd_norm_g[0].reshape(1, D_INNER)
    wso = w_ssd_out[0].astype(BF16)
    wfo = w_fft_out[0].astype(BF16)
    wo = w_o[0].astype(BF16)
    wff = w_ffn_in[0].astype(BF16)
    wfd = w_ffn_out[0].astype(BF16)

    cvecs = jnp.concatenate([c, c_ctx[None, :], jnp.zeros((SUBLANE - b - 1, D_MODEL), F32)], axis=0)
    mods = _mods(cvecs, b + 1, w_ada[0], b_ada[0])
    lat = mods[:b].reshape(b, 6, 1, D_MODEL)
    sh1, sc1, g1, sh2, sc2, g2 = (lat[:, k] for k in range(6))
    cm = mods[b].reshape(6, 1, 1, D_MODEL)

    x2d = x.reshape(b * l, D_MODEL)
    n1 = norm1_g[0].reshape(1, D_MODEL)
    u2d, dt_raw = _inproj(x2d, sh1, sc1, n1, w_main_t, w_dt2, U_WIDTH, proj_tm, l)
    u = u2d.reshape(b, l, U_WIDTH)
    cu2d, cdt_raw = _inproj(ctx.reshape(b * ctx_len, D_MODEL), cm[0], cm[1], n1, w_main_t, w_dt2,
                            CONV_DIM, b * ctx_len, b * ctx_len)
    cu = cu2d.reshape(b, ctx_len, CONV_DIM)

    rows = _dtprep(dt_raw.reshape(b, l, LANE), bias_v, alog_v, prep_tl)
    crows = _dtprep(cdt_raw.reshape(b, ctx_len, LANE), bias_v, alog_v, ctx_len)
    y_ssd = _ssd(u, rows, cu, crows, conv_w_p, conv_b_p, dskip_v, gain_v, ssd_tl)

    f_mix = _fourier(u, fft_tm)

    out = _tail(y_ssd.reshape(b * l, D_INNER), f_mix.reshape(b * l, D_MODEL), u2d, x2d,
                g1, sh2, sc2, g2, wso, wfo, wo, norm2_g[0].reshape(1, D_MODEL), wff, wfd,
                final_g.reshape(1, D_MODEL), tail_tm, l)
    return out.reshape(b, l, D_MODEL)


def kernel(x, c, ctx, c_ctx, w_ada, b_ada, norm1_g, w_in, conv_w, conv_b, dt_bias, a_log, d_skip,
           ssd_norm_g, w_ssd_out, w_fft_out, w_o, norm2_g, w_ffn_in, w_ffn_out, final_g):
    return _block(x, c, ctx, c_ctx, w_ada, b_ada, norm1_g, w_in, conv_w, conv_b, dt_bias, a_log, d_skip,
                  ssd_norm_g, w_ssd_out, w_fft_out, w_o, norm2_g, w_ffn_in, w_ffn_out, final_g,
                  proj_tm=2048, prep_tl=1024, ssd_tl=4096, fft_tm=512, tail_tm=512)
```

```python
import functools
import math

import jax
import jax.numpy as jnp
import numpy as np
from jax import lax
from jax.experimental import pallas as pl
from jax.experimental.pallas import tpu as pltpu

F32 = jnp.float32
BF16 = jnp.bfloat16

D_MODEL = 1024
EPS = 1e-6
GRID_W = 64

N_GROUPS = 8
HEADS_PER_GROUP = 4
HEAD_DIM = 64
N_HEADS = N_GROUPS * HEADS_PER_GROUP
D_STATE = 128
D_INNER = N_HEADS * HEAD_DIM
GROUP_W = HEADS_PER_GROUP * HEAD_DIM
GS = N_GROUPS * D_STATE
CONV_DIM = D_INNER + 2 * GS
D_CONV = 5
CHUNK = 128

FFT_GROUPS = 8
FFT_GROUP_DIM = 128
D_FF = 2816

REF_DT = CONV_DIM
REF_Z = REF_DT + 2 * N_HEADS

U_B = D_INNER
U_C = U_B + GS
U_Z = CONV_DIM
U_FFT = U_Z + D_INNER
U_GATE = U_FFT + D_MODEL
U_WIDTH = U_GATE + 2 * D_MODEL

LANE = 128
SUBLANE = 8
HALO = 16
VMEM_LIMIT = 56 * 1024 * 1024

N_DH = 2 * HEADS_PER_GROUP
R_CUM, R_NCX, R_EY, R_WX, R_DIAG = 0, 8, 16, 24, 32
ROWS_PER_GROUP = 40
N_SLAB = (GROUP_W + 2 * D_STATE) // LANE
DT_FLOOR = 1e-37
LOG2E = math.log2(math.e)
N_CUM_TERMS = 3
FWD_PER_TRIP = 4
BWD_PER_TRIP = 8
N_BUFS = 4
FWD_AHEAD = 1
BWD_AHEAD = 3
FWD_WEAVE = "C0 C1 P0 P1"
BWD_WEAVE = "P0 C0"
K_ARGS = 2 * N_CUM_TERMS * N_DH
FFT_PITCH = GRID_W + 4


def _dot(a, b):
    return jnp.dot(a, b, preferred_element_type=F32)


def _dot_tn(a, b):
    return lax.dot_general(a, b, (((0,), (0,)), ((), ())), preferred_element_type=F32)


def _dot_nt(a, b):
    return lax.dot_general(a, b, (((1,), (1,)), ((), ())), preferred_element_type=F32)


def _rmsnorm(x, g):
    return x * lax.rsqrt(jnp.mean(x * x, axis=-1, keepdims=True) + EPS) * g


def _mods_kernel(ct_ref, w_ref, b_ref, o_ref, *, n_vecs):
    s = ct_ref[...]
    s = s * jax.nn.sigmoid(s)
    w = w_ref[...]
    rows = [jnp.sum(w * s[:, r:r + 1], axis=0, keepdims=True) for r in range(n_vecs)]
    rows.append(jnp.zeros((SUBLANE - n_vecs, w.shape[1]), F32))
    o_ref[...] = jnp.concatenate(rows, axis=0) + b_ref[...]


def _mods(cvecs, n_vecs, w_ada, b_ada):
    n = w_ada.shape[1]
    tn = 1536
    return pl.pallas_call(
        functools.partial(_mods_kernel, n_vecs=n_vecs),
        grid=(n // tn,),
        in_specs=[pl.BlockSpec((D_MODEL, SUBLANE), lambda j: (0, 0)),
                  pl.BlockSpec((D_MODEL, tn), lambda j: (0, j)),
                  pl.BlockSpec((1, tn), lambda j: (0, j))],
        out_specs=pl.BlockSpec((SUBLANE, tn), lambda j: (0, j)),
        out_shape=jax.ShapeDtypeStruct((SUBLANE, n), F32),
        compiler_params=pltpu.CompilerParams(dimension_semantics=("arbitrary",),
                                             vmem_limit_bytes=VMEM_LIMIT),
        name="mods",
    )(cvecs.T, w_ada, b_ada.reshape(1, n))


def _inproj_kernel(x_ref, sh_ref, sc_ref, g_ref, w_ref, wdt_ref, u_ref, dt_ref, h_ref):
    @pl.when(pl.program_id(1) == 0)
    def _():
        h = _rmsnorm(x_ref[...], g_ref[...]) * (1.0 + sc_ref[0]) + sh_ref[0]
        hb = h.astype(BF16)
        h_ref[...] = hb
        hl = (h - hb.astype(F32)).astype(BF16)
        a = _dot_nt(hb, wdt_ref[...])
        dt_ref[...] = a[:, :LANE] + a[:, LANE:] + _dot_nt(hl, wdt_ref[:LANE, :])

    u_ref[...] = _dot_nt(h_ref[...], w_ref[...]).astype(BF16)


def _inproj(x2d, shift, scale, gain, w_t, w_dt_t, n_cols, tm, rows_per_mod):
    m = x2d.shape[0]
    tn = 1024
    bpm = rows_per_mod // tm

    def w_rows(i, j):
        return pl.multiple_of(j * tn + jnp.where(j * tn >= REF_DT, REF_Z - REF_DT, 0), 2 * HALO), 0

    return pl.pallas_call(
        _inproj_kernel,
        grid=(m // tm, n_cols // tn),
        in_specs=[pl.BlockSpec((tm, D_MODEL), lambda i, j: (i, 0)),
                  pl.BlockSpec((1, 1, D_MODEL), lambda i, j: (i // bpm, 0, 0)),
                  pl.BlockSpec((1, 1, D_MODEL), lambda i, j: (i // bpm, 0, 0)),
                  pl.BlockSpec((1, D_MODEL), lambda i, j: (0, 0)),
                  pl.BlockSpec((pl.Element(tn), pl.Element(D_MODEL)), w_rows),
                  pl.BlockSpec((2 * LANE, D_MODEL), lambda i, j: (0, 0))],
        out_specs=[pl.BlockSpec((tm, tn), lambda i, j: (i, j)),
                   pl.BlockSpec((tm, LANE), lambda i, j: (i, 0))],
        out_shape=[jax.ShapeDtypeStruct((m, n_cols), BF16),
                   jax.ShapeDtypeStruct((m, LANE), F32)],
        scratch_shapes=[pltpu.VMEM((tm, D_MODEL), BF16)],
        compiler_params=pltpu.CompilerParams(dimension_semantics=("arbitrary", "arbitrary"),
                                             vmem_limit_bytes=VMEM_LIMIT),
        name="inproj",
    )(x2d, shift, scale, gain, w_t, w_dt_t)


def _dtprep_kernel(raw_ref, bias_ref, alog_ref, row_ref, *, n_chunks):
    a = -jnp.exp(alog_ref[...])
    bias = bias_ref[...]
    ii = lax.broadcasted_iota(jnp.int32, (CHUNK, CHUNK), 0)
    jj = lax.broadcasted_iota(jnp.int32, (CHUNK, CHUNK), 1)
    tril = (jj <= ii).astype(BF16)
    triu = (jj >= ii).astype(BF16)
    lane = lax.broadcasted_iota(jnp.int32, (1, LANE), 1)
    is_fwd = (lane % N_DH) < HEADS_PER_GROUP

    def tri_sum(tri, terms):
        s = _dot(tri, terms)
        return s[:, :LANE] + s[:, LANE:2 * LANE] + s[:, 2 * LANE:]

    for c in range(n_chunks):
        sl = slice(c * CHUNK, (c + 1) * CHUNK)
        dt = jax.nn.softplus(raw_ref[0, sl, :] + bias)
        dta = dt * a
        terms = jnp.concatenate(_split_terms(dta, N_CUM_TERMS), axis=1).astype(BF16)
        cf = tri_sum(tril, terms)
        cb = tri_sum(triu, terms)
        cum = jnp.where(is_fwd, cf, cb)
        tot = jnp.where(is_fwd, cf[CHUNK - 1:CHUNK, :], cb[0:1, :])
        wx = dt * jnp.exp(tot - cum)
        ey = jnp.exp(cum)
        ncx = jnp.log(jnp.maximum(dt, DT_FLOOR)) - cum
        diag = jnp.log(dt + pltpu.roll(dt, LANE - HEADS_PER_GROUP, 1))
        for r_off, val in ((R_CUM, cum * LOG2E), (R_NCX, ncx * LOG2E), (R_EY, ey), (R_WX, wx),
                           (R_DIAG, diag * LOG2E)):
            vt = val.T
            for g in range(N_GROUPS):
                r0 = g * ROWS_PER_GROUP + r_off
                row_ref[0, r0:r0 + N_DH, sl] = vt[g * N_DH:(g + 1) * N_DH, :]


def _dtprep(raw, bias, alog, tl):
    b, l, _ = raw.shape
    return pl.pallas_call(
        functools.partial(_dtprep_kernel, n_chunks=tl // CHUNK),
        grid=(b, l // tl),
        in_specs=[pl.BlockSpec((1, tl, LANE), lambda i, j: (i, j, 0)),
                  pl.BlockSpec((1, LANE), lambda i, j: (0, 0)),
                  pl.BlockSpec((1, LANE), lambda i, j: (0, 0))],
        out_specs=pl.BlockSpec((1, N_GROUPS * ROWS_PER_GROUP, tl), lambda i, j: (i, 0, j)),
        out_shape=jax.ShapeDtypeStruct((b, N_GROUPS * ROWS_PER_GROUP, l), F32),
        compiler_params=pltpu.CompilerParams(dimension_semantics=("arbitrary", "arbitrary"),
                                             vmem_limit_bytes=VMEM_LIMIT),
        name="dtprep",
    )(raw, bias, alog)


def _split_terms(x, n):
    terms = []
    for _ in range(n - 1):
        t = x.astype(BF16).astype(F32)
        terms.append(t)
        x = x - t
    terms.append(x)
    return terms


def _scale_rows(rows):
    return jnp.concatenate(_split_terms(rows[R_EY:R_EY + N_DH], 2) + _split_terms(rows[R_WX:R_WX + N_DH], 2),
                           axis=0).astype(BF16)


def _conv_silu(ext_refs, s, p0, cw, cbias):
    sl = slice(s * LANE, (s + 1) * LANE)
    acc = cbias[:, sl].astype(BF16)
    for k in range(D_CONV):
        pair = p0 + (HALO + k - D_CONV // 2 - k % 2) // 2
        tap = pltpu.bitcast(ext_refs[k % 2][s, pl.ds(pair, CHUNK // 2), :], BF16)
        acc = acc + cw[k:k + 1, sl].astype(BF16) * tap
    return acc * jax.nn.sigmoid(acc)


def _ssd_kernel(x_ref, b_ref, c_ref, xp_ref, bp_ref, cp_ref, xn_ref, bn_ref, cn_ref, z_ref, row_ref,
                cx_ref, cb_ref, crow_ref, cwx_ref, cwb_ref, cwc_ref, cbx_ref, cbb_ref, cbc_ref,
                dskip_ref, gain_ref, onesel_ref, sel_ref, out_ref,
                cache_ref, yacc_ref, sf_ref, sb_ref, ext0_ref, ext1_ref, *buf_refs, tl, n_blk, ctx_len):
    p = pl.program_id(2)
    i = pl.program_id(3)
    n_ch = tl // CHUNK
    args_bufs = buf_refs[0:N_BUFS]
    ew_bufs = buf_refs[N_BUFS:2 * N_BUFS]
    st_bufs = buf_refs[2 * N_BUFS:3 * N_BUFS]
    ext_refs = (ext0_ref, ext1_ref)
    cw = jnp.concatenate([cwx_ref[...], cwb_ref[...], cwc_ref[...]], axis=1)
    cbias = jnp.concatenate([cbx_ref[...], cbb_ref[...], cbc_ref[...]], axis=1)

    def slabs_of(x, bm, cm):
        return [x[:, :LANE], x[:, LANE:], bm, cm]

    def slab_refs(xr, br, cr):
        return [(xr, 0), (xr, LANE), (br, 0), (cr, 0)]

    def fill_ext(src, first, last, n_rows):
        half = HALO // 2
        n_piece = n_rows // CHUNK

        def copy_piece(k, carry):
            r = pl.multiple_of(k * CHUNK, CHUNK)
            q = pl.multiple_of(k * (CHUNK // 2), CHUNK // 2)
            for s, (ref, lane0) in enumerate(src):
                ext0_ref[s, pl.ds(half + q, CHUNK // 2), :] = pltpu.bitcast(
                    ref[0, pl.ds(r, CHUNK), lane0:lane0 + LANE], jnp.uint32)
            return carry

        def shift_piece(k, carry):
            q = pl.multiple_of(k * (CHUNK // 2), CHUNK // 2)
            for s in range(len(src)):
                lo = ext0_ref[s, pl.ds(q, CHUNK // 2), :]
                hi = ext0_ref[s, pl.ds(q + 1, CHUNK // 2), :]
                ext1_ref[s, pl.ds(q, CHUNK // 2), :] = (lo >> 16) | (hi << 16)
            return carry

        for s in range(len(src)):
            ext0_ref[s, 0:half, :] = pltpu.bitcast(first[s], jnp.uint32)
            ext0_ref[s, half + n_rows // 2:2 * half + n_rows // 2, :] = pltpu.bitcast(last[s], jnp.uint32)
        lax.fori_loop(0, n_piece, copy_piece, 0, unroll=2)
        lax.fori_loop(0, n_piece, shift_piece, 0, unroll=2)
        for s in range(len(src)):
            q = n_rows // 2
            ext1_ref[s, q:q + half, :] = (ext0_ref[s, q:q + half, :] >> 16) | (ext0_ref[s, q + 1:q + half + 1, :] << 16)

    def row_of(c):
        return c * CHUNK if isinstance(c, int) else pl.multiple_of(c * CHUNK, CHUNK)

    def expand(rowt, d):
        return _dot_tn(_scale_rows(rowt), sel_ref[d])

    def run_pipelined(prepare_parts, consume_parts, weave, order, per_trip, ahead):
        def step(k, d, with_next):
            handed = None
            for item in weave.split():
                n = int(item[1:])
                if item[0] == "P":
                    if with_next:
                        prepare_parts[n](order(k + ahead), (d + ahead) % N_BUFS)
                else:
                    handed = consume_parts[n](order(k), d % N_BUFS, handed)

        for a in range(ahead):
            for part in prepare_parts:
                part(order(a), a)

        def trip(t, carry):
            for d in range(per_trip):
                step(per_trip * t + d, d, True)
            return carry

        lax.fori_loop(0, n_ch // per_trip - 1, trip, 0)
        for d in range(per_trip):
            k = n_ch - per_trip + d
            step(k, d, k + ahead < n_ch)

    @pl.when(jnp.logical_and(p == 0, i == 0))
    def _init_from_context():
        zpad = [jnp.zeros((HALO, LANE), BF16)] * (N_SLAB - 1)
        fill_ext(slab_refs(cx_ref, cb_ref, None)[:N_SLAB - 1], zpad, zpad, ctx_len)
        n_cc = ctx_len // CHUNK
        sf = jnp.zeros((D_STATE, GROUP_W), F32)
        st_b, ey_b = [], []
        for cc in range(n_cc):
            slabs = [_conv_silu(ext_refs, s, cc * CHUNK // 2, cw, cbias) for s in range(N_SLAB - 1)]
            xs = jnp.concatenate(slabs[:2], axis=1).astype(F32)
            bm = slabs[2]
            rowt = crow_ref[0, :, cc * CHUNK:(cc + 1) * CHUNK]
            ewf = expand(rowt, 0)
            ewb = expand(rowt, 1)
            sf = sf * ewf[CHUNK - 1:CHUNK, :GROUP_W] + _dot_tn(bm, (xs * ewf[:, GROUP_W:]).astype(BF16))
            st_b.append(_dot_tn(bm, (xs * ewb[:, GROUP_W:]).astype(BF16)))
            ey_b.append(ewb[0:1, :GROUP_W])
        sb = jnp.zeros((D_STATE, GROUP_W), F32)
        for cc in reversed(range(n_cc)):
            sb = sb * ey_b[cc] + st_b[cc]
        sf_ref[...] = sf
        sb_ref[...] = sb

    @pl.when(p == 0)
    def _forward():
        blk = i
        base = blk * tl
        has_prev = blk > 0
        has_next = blk < n_blk - 1
        fill_ext(slab_refs(x_ref, b_ref, c_ref),
                 [jnp.where(has_prev, v, jnp.zeros_like(v)) for v in slabs_of(xp_ref[0], bp_ref[0], cp_ref[0])],
                 [jnp.where(has_next, v, jnp.zeros_like(v)) for v in slabs_of(xn_ref[0], bn_ref[0], cn_ref[0])],
                 tl)
        ii = lax.broadcasted_iota(jnp.int32, (CHUNK, CHUNK), 0)
        jj = lax.broadcasted_iota(jnp.int32, (CHUNK, CHUNK), 1)
        lower = jj < ii
        upper = jj > ii

        def prepare_conv(c, buf):
            r = row_of(c)
            p0 = c * (CHUNK // 2) if isinstance(c, int) else pl.multiple_of(c * (CHUNK // 2), CHUNK // 2)
            slabs = [_conv_silu(ext_refs, s, p0, cw, cbias) for s in range(N_SLAB)]
            for s in range(N_SLAB):
                cache_ref[pl.ds(base + r, CHUNK), s * LANE:(s + 1) * LANE] = slabs[s]
            yacc_ref[pl.ds(base + r, CHUNK), :] = jnp.concatenate(slabs[:2], axis=1).astype(F32) * dskip_ref[...]

        def prepare_scales(c, buf):
            rowt = row_ref[0, :, pl.ds(row_of(c), CHUNK)]
            onesel = onesel_ref[...]
            ones = jnp.ones((N_DH, CHUNK), F32)
            at = jnp.concatenate(_split_terms(rowt[R_CUM:R_CUM + N_DH], N_CUM_TERMS) + [ones] * N_CUM_TERMS,
                                 axis=0).astype(BF16)
            wide = [jnp.concatenate([t] * N_DH, axis=1) * onesel
                    for t in _split_terms(rowt[R_NCX:R_NCX + N_DH], N_CUM_TERMS)]
            bmat = jnp.concatenate([onesel] * N_CUM_TERMS + wide, axis=0).astype(BF16)
            args_bufs[buf][...] = _dot_tn(at, bmat)
            ew_bufs[buf][...] = expand(rowt, 0)

        def consume_state(c, buf, _):
            r = row_of(c)
            xbc_b = cache_ref[pl.ds(base + r, CHUNK), :]
            xs_b = xbc_b[:, :GROUP_W]
            bm = xbc_b[:, GROUP_W:GROUP_W + D_STATE]
            cm = xbc_b[:, GROUP_W + D_STATE:]
            return r, xs_b, bm, cm, _dot_nt(cm, bm)

        def consume_decay(c, buf, handed):
            r, xs_b, bm, cm, cbm = handed
            yd = []
            for j in range(HEADS_PER_GROUP):
                argf = args_bufs[buf][:, j * CHUNK:(j + 1) * CHUNK]
                argb = args_bufs[buf][:, (HEADS_PER_GROUP + j) * CHUNK:(HEADS_PER_GROUP + j + 1) * CHUNK]
                diag = row_ref[0, R_DIAG + j:R_DIAG + j + 1, pl.ds(r, CHUNK)]
                arg = jnp.where(lower, argf, jnp.where(upper, argb, diag))
                wmat = (cbm * jnp.exp2(arg)).astype(BF16)
                yd.append(_dot(wmat, xs_b[:, j * HEAD_DIM:(j + 1) * HEAD_DIM]))
            ef = ew_bufs[buf][:, :GROUP_W]
            wxf = ew_bufs[buf][:, GROUP_W:]
            sf = sf_ref[...]
            yacc_ref[pl.ds(base + r, CHUNK), :] += jnp.concatenate(yd, axis=1) + _dot(cm, sf.astype(BF16)) * ef
            sf_ref[...] = sf * ef[CHUNK - 1:CHUNK, :] + _dot_tn(bm, (xs_b.astype(F32) * wxf).astype(BF16))

        run_pipelined([prepare_conv, prepare_scales], [consume_state, consume_decay], FWD_WEAVE,
                      lambda k: k, FWD_PER_TRIP, FWD_AHEAD)

    @pl.when(p == 1)
    def _backward():
        blk = n_blk - 1 - i
        base = blk * tl

        def prepare(c, buf):
            r = row_of(c)
            ew = expand(row_ref[0, :, pl.ds(r, CHUNK)], 1)
            ew_bufs[buf][...] = ew
            xs = cache_ref[pl.ds(base + r, CHUNK), 0:GROUP_W].astype(F32)
            bm = cache_ref[pl.ds(base + r, CHUNK), GROUP_W:GROUP_W + D_STATE]
            st_bufs[buf][...] = _dot_tn(bm, (xs * ew[:, GROUP_W:]).astype(BF16))

        def consume(c, buf, _):
            r = row_of(c)
            cm = cache_ref[pl.ds(base + r, CHUNK), GROUP_W + D_STATE:]
            eb = ew_bufs[buf][:, :GROUP_W]
            sb = sb_ref[...]
            y = yacc_ref[pl.ds(base + r, CHUNK), :] + _dot(cm, sb.astype(BF16)) * eb
            sb_ref[...] = sb * eb[0:1, :] + st_bufs[buf][...]
            z = z_ref[0, pl.ds(r, CHUNK), :].astype(F32)
            y = y * (z * jax.nn.sigmoid(z))
            out_ref[0, pl.ds(r, CHUNK), :] = _rmsnorm(y, gain_ref[...]).astype(BF16)

        run_pipelined([prepare], [consume], BWD_WEAVE, lambda k: n_ch - 1 - k, BWD_PER_TRIP, BWD_AHEAD)


def _ssd_selectors():
    onesel = (np.arange(N_DH * CHUNK)[None, :] // CHUNK == np.arange(N_DH)[:, None]).astype(np.float32)
    sel = np.zeros((2, 4 * N_DH, 2 * GROUP_W), np.float32)
    for d in range(2):
        for j in range(HEADS_PER_GROUP):
            dh = d * HEADS_PER_GROUP + j
            for term in range(2):
                sel[d, term * N_DH + dh, j * HEAD_DIM:(j + 1) * HEAD_DIM] = 1.0
                sel[d, (2 + term) * N_DH + dh, GROUP_W + j * HEAD_DIM:GROUP_W + (j + 1) * HEAD_DIM] = 1.0
    return jnp.asarray(onesel, F32), jnp.asarray(sel, BF16)


def _ssd(u, rows, cu, crows, conv_w, conv_b, dskip, gain, tl):
    b, l, _ = u.shape
    ctx_len = cu.shape[1]
    assert ctx_len <= tl
    n_blk = l // tl
    hb = tl // HALO
    last_halo = l // HALO - 1
    assert (tl // CHUNK) % FWD_PER_TRIP == 0 and (tl // CHUNK) % BWD_PER_TRIP == 0
    onesel, sel = _ssd_selectors()
    xg, bg, cg = 0, U_B // D_STATE, U_C // D_STATE

    def in_blk(p, i):
        return i * (1 - p) + (n_blk - 1) * p

    def seq_blk(p, i):
        return i * (1 - p) + (n_blk - 1 - i) * p

    def out_blk(p, i):
        return (n_blk - 1) * (1 - p) + (n_blk - 1 - i) * p

    def prev_halo(p, i):
        return jnp.maximum(in_blk(p, i) * hb - 1, 0)

    def next_halo(p, i):
        return jnp.minimum((in_blk(p, i) + 1) * hb, last_halo)

    def xbc_specs(rows_blk, row_index):
        return [pl.BlockSpec((1, rows_blk, GROUP_W), lambda bb, g, p, i: (bb, row_index(p, i), xg + g)),
                pl.BlockSpec((1, rows_blk, D_STATE), lambda bb, g, p, i: (bb, row_index(p, i), bg + g)),
                pl.BlockSpec((1, rows_blk, D_STATE), lambda bb, g, p, i: (bb, row_index(p, i), cg + g))]

    def param_specs(n_rows):
        return [pl.BlockSpec((n_rows, GROUP_W), lambda bb, g, p, i: (0, xg + g)),
                pl.BlockSpec((n_rows, D_STATE), lambda bb, g, p, i: (0, bg + g)),
                pl.BlockSpec((n_rows, D_STATE), lambda bb, g, p, i: (0, cg + g))]

    grid = (b, N_GROUPS, 2, n_blk)
    in_specs = (
        xbc_specs(tl, in_blk) + xbc_specs(HALO, prev_halo) + xbc_specs(HALO, next_halo)
        + [pl.BlockSpec((1, tl, GROUP_W), lambda bb, g, p, i: (bb, out_blk(p, i), U_Z // GROUP_W + g)),
           pl.BlockSpec((1, ROWS_PER_GROUP, tl), lambda bb, g, p, i: (bb, g, seq_blk(p, i)))]
        + xbc_specs(ctx_len, lambda p, i: 0)[:2]
        + [pl.BlockSpec((1, ROWS_PER_GROUP, ctx_len), lambda bb, g, p, i: (bb, g, 0))]
        + param_specs(SUBLANE) + param_specs(1)
        + [pl.BlockSpec((1, GROUP_W), lambda bb, g, p, i: (0, g)),
           pl.BlockSpec((1, GROUP_W), lambda bb, g, p, i: (0, g)),
           pl.BlockSpec(onesel.shape, lambda bb, g, p, i: (0, 0)),
           pl.BlockSpec(sel.shape, lambda bb, g, p, i: (0, 0, 0))])
    return pl.pallas_call(
        functools.partial(_ssd_kernel, tl=tl, n_blk=n_blk, ctx_len=ctx_len),
        grid=grid,
        in_specs=in_specs,
        out_specs=pl.BlockSpec((1, tl, GROUP_W), lambda bb, g, p, i: (bb, out_blk(p, i), g)),
        out_shape=jax.ShapeDtypeStruct((b, l, D_INNER), BF16),
        scratch_shapes=[pltpu.VMEM((l, N_SLAB * LANE), BF16),
                        pltpu.VMEM((l, GROUP_W), F32),
                        pltpu.VMEM((D_STATE, GROUP_W), F32),
                        pltpu.VMEM((D_STATE, GROUP_W), F32),
                        pltpu.VMEM((N_SLAB, (tl + 2 * HALO) // 2, LANE), jnp.uint32),
                        pltpu.VMEM((N_SLAB, (tl + 2 * HALO) // 2, LANE), jnp.uint32)]
        + [pltpu.VMEM((CHUNK, N_DH * CHUNK), F32)] * N_BUFS
        + [pltpu.VMEM((CHUNK, 2 * GROUP_W), F32)] * N_BUFS
        + [pltpu.VMEM((D_STATE, GROUP_W), F32)] * N_BUFS,
        compiler_params=pltpu.CompilerParams(
            dimension_semantics=("arbitrary", "arbitrary", "arbitrary", "arbitrary"),
            vmem_limit_bytes=VMEM_LIMIT),
        name="ssd",
    )(u, u, u, u, u, u, u, u, u, u, rows, cu, cu, crows,
      conv_w, conv_w, conv_w, conv_b, conv_b, conv_b, dskip, gain, onesel, sel)


def _dft_mats():
    def cs(n):
        k = np.arange(n)
        ang = 2.0 * np.pi * np.outer(k, k) / n
        return np.cos(ang), np.sin(ang)

    c3, s3 = cs(FFT_GROUP_DIM)
    w1 = np.concatenate([c3, -s3], axis=1)
    c2, s2 = cs(GRID_W)
    m2 = np.block([[c2, s2], [-s2, c2]])
    return w1, m2


def _fft12_kernel(f_ref, w1_ref, m2_ref, o_ref, *, tm):
    f = f_ref[0]
    w1 = w1_ref[...]
    ps, qs = [], []
    for g in range(FFT_GROUPS):
        pq = _dot(f[:, g * FFT_GROUP_DIM:(g + 1) * FFT_GROUP_DIM], w1)
        ps.append(pq[:, :FFT_GROUP_DIM])
        qs.append(pq[:, FFT_GROUP_DIM:])
    pr = jnp.concatenate(ps, axis=1).astype(BF16)
    qr = jnp.concatenate(qs, axis=1).astype(BF16)
    m2 = m2_ref[...]
    pad = jnp.zeros((FFT_PITCH - GRID_W, D_MODEL), jnp.uint32)
    for r in range(tm // GRID_W):
        sl = slice(r * GRID_W, (r + 1) * GRID_W)
        o = _dot(m2, jnp.concatenate([pr[sl], qr[sl]], axis=0))
        r0 = r * FFT_PITCH
        o_ref[0, r0:r0 + GRID_W, :] = _pack_complex(o[:GRID_W], o[GRID_W:])
        o_ref[0, r0 + GRID_W:r0 + FFT_PITCH, :] = pad


def _pack_complex(re, im):
    re_bits = pltpu.bitcast(re.astype(BF16).astype(F32), jnp.uint32)
    im_bits = pltpu.bitcast(im.astype(BF16).astype(F32), jnp.uint32)
    return (re_bits >> 16) | (im_bits & jnp.uint32(0xFFFF0000))


def _unpack_complex(word):
    re = pltpu.bitcast(word << 16, F32)
    im = pltpu.bitcast(word & jnp.uint32(0xFFFF0000), F32)
    return re, im


def _fft3_kernel(ri_ref, m3_ref, o_ref, scr_ref, *, rows):
    m3 = m3_ref[...]

    def column_pair(t, carry):
        w = 2 * t
        cols = [jnp.concatenate(_unpack_complex(ri_ref[0, pl.ds(w + dw, rows, stride=FFT_PITCH), :]), axis=0)
                for dw in range(2)]
        o = _dot(m3, jnp.concatenate(cols, axis=1).astype(BF16))
        for dw in range(2):
            scr_ref[pl.ds(w + dw, rows, stride=FFT_PITCH), :] = o[:, dw * LANE:(dw + 1) * LANE]
        return carry

    lax.fori_loop(0, GRID_W // 2, column_pair, 0, unroll=4)
    for k in range(rows):
        o_ref[0, k * GRID_W:(k + 1) * GRID_W, :] = scr_ref[k * FFT_PITCH:k * FFT_PITCH + GRID_W, :].astype(BF16)


def _fourier(u, tm):
    b, l, _ = u.shape
    rows = l // GRID_W
    w1, m2 = _dft_mats()
    ri = pl.pallas_call(
        functools.partial(_fft12_kernel, tm=tm),
        grid=(b, l // tm),
        in_specs=[pl.BlockSpec((1, tm, D_MODEL), lambda i, j: (i, j, U_FFT // D_MODEL)),
                  pl.BlockSpec((FFT_GROUP_DIM, 2 * FFT_GROUP_DIM), lambda i, j: (0, 0)),
                  pl.BlockSpec((2 * GRID_W, 2 * GRID_W), lambda i, j: (0, 0))],
        out_specs=pl.BlockSpec((1, tm // GRID_W * FFT_PITCH, D_MODEL), lambda i, j: (i, j, 0)),
        out_shape=jax.ShapeDtypeStruct((b, rows * FFT_PITCH, D_MODEL), jnp.uint32),
        compiler_params=pltpu.CompilerParams(dimension_semantics=("arbitrary", "arbitrary"),
                                             vmem_limit_bytes=VMEM_LIMIT),
        name="fft12",
    )(u, jnp.asarray(w1, BF16), jnp.asarray(m2, BF16))

    k = np.arange(rows)
    ang = 2.0 * np.pi * np.outer(k, k) / rows
    scale = 1.0 / np.sqrt(float(rows * GRID_W * FFT_GROUP_DIM))
    m3 = np.concatenate([np.cos(ang), np.sin(ang)], axis=1) * scale
    return pl.pallas_call(
        functools.partial(_fft3_kernel, rows=rows),
        grid=(b, D_MODEL // LANE),
        in_specs=[pl.BlockSpec((1, rows * FFT_PITCH, LANE), lambda i, j: (i, 0, j)),
                  pl.BlockSpec((rows, 2 * rows), lambda i, j: (0, 0))],
        out_specs=pl.BlockSpec((1, l, LANE), lambda i, j: (i, 0, j)),
        out_shape=jax.ShapeDtypeStruct((b, l, D_MODEL), BF16),
        scratch_shapes=[pltpu.VMEM((rows * FFT_PITCH, LANE), F32)],
        compiler_params=pltpu.CompilerParams(dimension_semantics=("arbitrary", "arbitrary"),
                                             vmem_limit_bytes=VMEM_LIMIT),
        name="fft3",
    )(ri, jnp.asarray(m3, BF16))


FF_SPLIT = 2


def _tail_kernel(y_ref, f_ref, ga_ref, gb_ref, x_ref, g1_ref, sh2_ref, sc2_ref, g2_ref,
                 wso_ref, wfo_ref, wo_ref, n2_ref, wff_ref, wfd_ref, fg_ref, o_ref):
    a = _dot(y_ref[...], wso_ref[...])
    bq = _dot(f_ref[...], wfo_ref[...])
    merged = jax.nn.sigmoid(ga_ref[...].astype(F32)) * a + jax.nn.sigmoid(gb_ref[...].astype(F32)) * bq
    xn = x_ref[...] + g1_ref[0] * _dot(merged.astype(BF16), wo_ref[...])
    h = (_rmsnorm(xn, n2_ref[...]) * (1.0 + sc2_ref[0]) + sh2_ref[0]).astype(BF16)
    ffw = D_FF // FF_SPLIT
    acc = None
    for s in range(FF_SPLIT):
        gate = _dot(h, wff_ref[:, s * ffw:(s + 1) * ffw])
        up = _dot(h, wff_ref[:, D_FF + s * ffw:D_FF + (s + 1) * ffw])
        act = (gate * jax.nn.sigmoid(gate) * up).astype(BF16)
        part = _dot(act, wfd_ref[s * ffw:(s + 1) * ffw, :])
        acc = part if acc is None else acc + part
    o_ref[...] = _rmsnorm(xn + g2_ref[0] * acc, fg_ref[...])


def _tail(y2d, f2d, u2d, x2d, g1, sh2, sc2, g2, wso, wfo, wo, n2, wff, wfd, fg, tm, rows_per_mod):
    m = x2d.shape[0]
    bpm = rows_per_mod // tm

    def const(shape):
        return pl.BlockSpec(shape, lambda i: (0, 0), pipeline_mode=pl.Buffered(1))

    def mod():
        return pl.BlockSpec((1, 1, D_MODEL), lambda i: (i // bpm, 0, 0))

    return pl.pallas_call(
        _tail_kernel,
        grid=(m // tm,),
        in_specs=[pl.BlockSpec((tm, D_INNER), lambda i: (i, 0)),
                  pl.BlockSpec((tm, D_MODEL), lambda i: (i, 0)),
                  pl.BlockSpec((tm, D_MODEL), lambda i: (i, U_GATE // D_MODEL)),
                  pl.BlockSpec((tm, D_MODEL), lambda i: (i, U_GATE // D_MODEL + 1)),
                  pl.BlockSpec((tm, D_MODEL), lambda i: (i, 0)),
                  mod(), mod(), mod(), mod(),
                  const((D_INNER, D_MODEL)), const((D_MODEL, D_MODEL)), const((D_MODEL, D_MODEL)),
                  const((1, D_MODEL)),
                  const((D_MODEL, 2 * D_FF)), const((D_FF, D_MODEL)),
                  const((1, D_MODEL))],
        out_specs=pl.BlockSpec((tm, D_MODEL), lambda i: (i, 0)),
        out_shape=jax.ShapeDtypeStruct((m, D_MODEL), F32),
        compiler_params=pltpu.CompilerParams(dimension_semantics=("arbitrary",),
                                             vmem_limit_bytes=VMEM_LIMIT),
        name="tail",
    )(y2d, f2d, u2d, u2d, x2d, g1, sh2, sc2, g2, wso, wfo, wo, n2, wff, wfd, fg)


def _dt_lanes(a):
    lead = a.shape[:-1]
    a = a.reshape(lead + (2, N_GROUPS, HEADS_PER_GROUP))
    a = jnp.swapaxes(a, -3, -2).reshape(lead + (2 * N_HEADS,))
    return jnp.concatenate([a, jnp.zeros(lead + (LANE - 2 * N_HEADS,), a.dtype)], axis=-1)


def _block(x, c, ctx, c_ctx, w_ada, b_ada, norm1_g, w_in, conv_w, conv_b, dt_bias, a_log, d_skip,
           ssd_norm_g, w_ssd_out, w_fft_out, w_o, norm2_g, w_ffn_in, w_ffn_out, final_g,
           proj_tm, prep_tl, ssd_tl, fft_tm, tail_tm):
    b, l, d = x.shape
    ctx_len = ctx.shape[1]
    assert d == D_MODEL and l % (GRID_W * 2) == 0 and ctx_len % CHUNK == 0 and b + 1 <= SUBLANE

    w_in_t = w_in[0].T
    w_main_t = w_in_t.astype(BF16)
    w_dt = _dt_lanes(w_in_t[REF_DT:REF_Z].T).T
    w_dt_hi = w_dt.astype(BF16)
    w_dt_lo = (w_dt - w_dt_hi.astype(F32)).astype(BF16)
    w_dt2 = jnp.concatenate([w_dt_hi, w_dt_lo], axis=0)
    conv_w_p = jnp.concatenate([conv_w[0], jnp.zeros((SUBLANE - D_CONV, CONV_DIM), F32)], axis=0)
    conv_b_p = conv_b[0].reshape(1, CONV_DIM)
    bias_v = _dt_lanes(dt_bias[0].reshape(-1)).reshape(1, LANE)
    alog_v = _dt_lanes(a_log[0].reshape(-1)).reshape(1, LANE)
    dskip_v = jnp.repeat(d_skip[0], HEAD_DIM).reshape(1, D_INNER)
    gain_v = ssd_norm_g[0].reshape(1, D_INNER)
    wso = w_ssd_out[0].astype(BF16)
    wfo = w_fft_out[0].astype(BF16)
    wo = w_o[0].astype(BF16)
    wff = w_ffn_in[0].astype(BF16)
    wfd = w_ffn_out[0].astype(BF16)

    cvecs = jnp.concatenate([c, c_ctx[None, :], jnp.zeros((SUBLANE - b - 1, D_MODEL), F32)], axis=0)
    mods = _mods(cvecs, b + 1, w_ada[0], b_ada[0])
    lat = mods[:b].reshape(b, 6, 1, D_MODEL)
    sh1, sc1, g1, sh2, sc2, g2 = (lat[:, k] for k in range(6))
    cm = mods[b].reshape(6, 1, 1, D_MODEL)

    x2d = x.reshape(b * l, D_MODEL)
    n1 = norm1_g[0].reshape(1, D_MODEL)
    u2d, dt_raw = _inproj(x2d, sh1, sc1, n1, w_main_t, w_dt2, U_WIDTH, proj_tm, l)
    u = u2d.reshape(b, l, U_WIDTH)
    cu2d, cdt_raw = _inproj(ctx.reshape(b * ctx_len, D_MODEL), cm[0], cm[1], n1, w_main_t, w_dt2,
                            CONV_DIM, b * ctx_len, b * ctx_len)
    cu = cu2d.reshape(b, ctx_len, CONV_DIM)

    rows = _dtprep(dt_raw.reshape(b, l, LANE), bias_v, alog_v, prep_tl)
    crows = _dtprep(cdt_raw.reshape(b, ctx_len, LANE), bias_v, alog_v, ctx_len)
    y_ssd = _ssd(u, rows, cu, crows, conv_w_p, conv_b_p, dskip_v, gain_v, ssd_tl)

    f_mix = _fourier(u, fft_tm)

    out = _tail(y_ssd.reshape(b * l, D_INNER), f_mix.reshape(b * l, D_MODEL), u2d, x2d,
                g1, sh2, sc2, g2, wso, wfo, wo, norm2_g[0].reshape(1, D_MODEL), wff, wfd,
                final_g.reshape(1, D_MODEL), tail_tm, l)
    return out.reshape(b, l, D_MODEL)


def kernel(x, c, ctx, c_ctx, w_ada, b_ada, norm1_g, w_in, conv_w, conv_b, dt_bias, a_log, d_skip,
           ssd_norm_g, w_ssd_out, w_fft_out, w_o, norm2_g, w_ffn_in, w_ffn_out, final_g):
    return _block(x, c, ctx, c_ctx, w_ada, b_ada, norm1_g, w_in, conv_w, conv_b, dt_bias, a_log, d_skip,
                  ssd_norm_g, w_ssd_out, w_fft_out, w_o, norm2_g, w_ffn_in, w_ffn_out, final_g,
                  proj_tm=2048, prep_tl=1024, ssd_tl=4096, fft_tm=512, tail_tm=512)
```

```python
import functools
import math

import jax
import jax.numpy as jnp
import numpy as np
from jax import lax
from jax.experimental import pallas as pl
from jax.experimental.pallas import tpu as pltpu

F32 = jnp.float32
BF16 = jnp.bfloat16

D_MODEL = 1024
EPS = 1e-6
GRID_W = 64

N_GROUPS = 8
HEADS_PER_GROUP = 4
HEAD_DIM = 64
N_HEADS = N_GROUPS * HEADS_PER_GROUP
D_STATE = 128
D_INNER = N_HEADS * HEAD_DIM
GROUP_W = HEADS_PER_GROUP * HEAD_DIM
GS = N_GROUPS * D_STATE
CONV_DIM = D_INNER + 2 * GS
D_CONV = 5
CHUNK = 128

FFT_GROUPS = 8
FFT_GROUP_DIM = 128
D_FF = 2816

REF_DT = CONV_DIM
REF_Z = REF_DT + 2 * N_HEADS

U_B = D_INNER
U_C = U_B + GS
U_Z = CONV_DIM
U_FFT = U_Z + D_INNER
U_GATE = U_FFT + D_MODEL
U_WIDTH = U_GATE + 2 * D_MODEL

LANE = 128
SUBLANE = 8
HALO = 16
VMEM_LIMIT = 56 * 1024 * 1024

N_DH = 2 * HEADS_PER_GROUP
R_CUM, R_NCX, R_EY, R_WX, R_DIAG = 0, 8, 16, 24, 32
ROWS_PER_GROUP = 40
N_SLAB = (GROUP_W + 2 * D_STATE) // LANE
DT_FLOOR = 1e-37
LOG2E = math.log2(math.e)
N_CUM_TERMS = 3
FWD_PER_TRIP = 4
BWD_PER_TRIP = 8
N_BUFS = 4
FWD_AHEAD = (1, 1)
BWD_AHEAD = (3,)
FWD_WEAVE = "C0 C1 P0 P1"
BWD_WEAVE = "P0 C0"
K_ARGS = 2 * N_CUM_TERMS * N_DH
FFT_PITCH = GRID_W + 4


def _dot(a, b):
    return jnp.dot(a, b, preferred_element_type=F32)


def _dot_tn(a, b):
    return lax.dot_general(a, b, (((0,), (0,)), ((), ())), preferred_element_type=F32)


def _dot_nt(a, b):
    return lax.dot_general(a, b, (((1,), (1,)), ((), ())), preferred_element_type=F32)


def _rmsnorm(x, g):
    return x * lax.rsqrt(jnp.mean(x * x, axis=-1, keepdims=True) + EPS) * g


def _mods_kernel(ct_ref, w_ref, b_ref, o_ref, *, n_vecs):
    s = ct_ref[...]
    s = s * jax.nn.sigmoid(s)
    w = w_ref[...]
    rows = [jnp.sum(w * s[:, r:r + 1], axis=0, keepdims=True) for r in range(n_vecs)]
    rows.append(jnp.zeros((SUBLANE - n_vecs, w.shape[1]), F32))
    o_ref[...] = jnp.concatenate(rows, axis=0) + b_ref[...]


def _mods(cvecs, n_vecs, w_ada, b_ada):
    n = w_ada.shape[1]
    tn = 1536
    return pl.pallas_call(
        functools.partial(_mods_kernel, n_vecs=n_vecs),
        grid=(n // tn,),
        in_specs=[pl.BlockSpec((D_MODEL, SUBLANE), lambda j: (0, 0)),
                  pl.BlockSpec((D_MODEL, tn), lambda j: (0, j)),
                  pl.BlockSpec((1, tn), lambda j: (0, j))],
        out_specs=pl.BlockSpec((SUBLANE, tn), lambda j: (0, j)),
        out_shape=jax.ShapeDtypeStruct((SUBLANE, n), F32),
        compiler_params=pltpu.CompilerParams(dimension_semantics=("arbitrary",),
                                             vmem_limit_bytes=VMEM_LIMIT),
        name="mods",
    )(cvecs.T, w_ada, b_ada.reshape(1, n))


def _inproj_kernel(x_ref, sh_ref, sc_ref, g_ref, w_ref, wdt_ref, u_ref, dt_ref, h_ref):
    @pl.when(pl.program_id(1) == 0)
    def _():
        h = _rmsnorm(x_ref[...], g_ref[...]) * (1.0 + sc_ref[0]) + sh_ref[0]
        hb = h.astype(BF16)
        h_ref[...] = hb
        hl = (h - hb.astype(F32)).astype(BF16)
        a = _dot_nt(hb, wdt_ref[...])
        dt_ref[...] = a[:, :LANE] + a[:, LANE:] + _dot_nt(hl, wdt_ref[:LANE, :])

    u_ref[...] = _dot_nt(h_ref[...], w_ref[...]).astype(BF16)


def _inproj(x2d, shift, scale, gain, w_t, w_dt_t, n_cols, tm, rows_per_mod):
    m = x2d.shape[0]
    tn = 1024
    bpm = rows_per_mod // tm

    def w_rows(i, j):
        return pl.multiple_of(j * tn + jnp.where(j * tn >= REF_DT, REF_Z - REF_DT, 0), 2 * HALO), 0

    return pl.pallas_call(
        _inproj_kernel,
        grid=(m // tm, n_cols // tn),
        in_specs=[pl.BlockSpec((tm, D_MODEL), lambda i, j: (i, 0)),
                  pl.BlockSpec((1, 1, D_MODEL), lambda i, j: (i // bpm, 0, 0)),
                  pl.BlockSpec((1, 1, D_MODEL), lambda i, j: (i // bpm, 0, 0)),
                  pl.BlockSpec((1, D_MODEL), lambda i, j: (0, 0)),
                  pl.BlockSpec((pl.Element(tn), pl.Element(D_MODEL)), w_rows),
                  pl.BlockSpec((2 * LANE, D_MODEL), lambda i, j: (0, 0))],
        out_specs=[pl.BlockSpec((tm, tn), lambda i, j: (i, j)),
                   pl.BlockSpec((tm, LANE), lambda i, j: (i, 0))],
        out_shape=[jax.ShapeDtypeStruct((m, n_cols), BF16),
                   jax.ShapeDtypeStruct((m, LANE), F32)],
        scratch_shapes=[pltpu.VMEM((tm, D_MODEL), BF16)],
        compiler_params=pltpu.CompilerParams(dimension_semantics=("arbitrary", "arbitrary"),
                                             vmem_limit_bytes=VMEM_LIMIT),
        name="inproj",
    )(x2d, shift, scale, gain, w_t, w_dt_t)


def _dtprep_kernel(raw_ref, bias_ref, alog_ref, row_ref, *, n_chunks):
    a = -jnp.exp(alog_ref[...])
    bias = bias_ref[...]
    ii = lax.broadcasted_iota(jnp.int32, (CHUNK, CHUNK), 0)
    jj = lax.broadcasted_iota(jnp.int32, (CHUNK, CHUNK), 1)
    tril = (jj <= ii).astype(BF16)
    triu = (jj >= ii).astype(BF16)
    lane = lax.broadcasted_iota(jnp.int32, (1, LANE), 1)
    is_fwd = (lane % N_DH) < HEADS_PER_GROUP

    def tri_sum(tri, terms):
        s = _dot(tri, terms)
        return s[:, :LANE] + s[:, LANE:2 * LANE] + s[:, 2 * LANE:]

    for c in range(n_chunks):
        sl = slice(c * CHUNK, (c + 1) * CHUNK)
        dt = jax.nn.softplus(raw_ref[0, sl, :] + bias)
        dta = dt * a
        terms = jnp.concatenate(_split_terms(dta, N_CUM_TERMS), axis=1).astype(BF16)
        cf = tri_sum(tril, terms)
        cb = tri_sum(triu, terms)
        cum = jnp.where(is_fwd, cf, cb)
        tot = jnp.where(is_fwd, cf[CHUNK - 1:CHUNK, :], cb[0:1, :])
        wx = dt * jnp.exp(tot - cum)
        ey = jnp.exp(cum)
        ncx = jnp.log(jnp.maximum(dt, DT_FLOOR)) - cum
        diag = jnp.log(dt + pltpu.roll(dt, LANE - HEADS_PER_GROUP, 1))
        for r_off, val in ((R_CUM, cum * LOG2E), (R_NCX, ncx * LOG2E), (R_EY, ey), (R_WX, wx),
                           (R_DIAG, diag * LOG2E)):
            vt = val.T
            for g in range(N_GROUPS):
                r0 = g * ROWS_PER_GROUP + r_off
                row_ref[0, r0:r0 + N_DH, sl] = vt[g * N_DH:(g + 1) * N_DH, :]


def _dtprep(raw, bias, alog, tl):
    b, l, _ = raw.shape
    return pl.pallas_call(
        functools.partial(_dtprep_kernel, n_chunks=tl // CHUNK),
        grid=(b, l // tl),
        in_specs=[pl.BlockSpec((1, tl, LANE), lambda i, j: (i, j, 0)),
                  pl.BlockSpec((1, LANE), lambda i, j: (0, 0)),
                  pl.BlockSpec((1, LANE), lambda i, j: (0, 0))],
        out_specs=pl.BlockSpec((1, N_GROUPS * ROWS_PER_GROUP, tl), lambda i, j: (i, 0, j)),
        out_shape=jax.ShapeDtypeStruct((b, N_GROUPS * ROWS_PER_GROUP, l), F32),
        compiler_params=pltpu.CompilerParams(dimension_semantics=("arbitrary", "arbitrary"),
                                             vmem_limit_bytes=VMEM_LIMIT),
        name="dtprep",
    )(raw, bias, alog)


def _split_terms(x, n):
    terms = []
    for _ in range(n - 1):
        t = x.astype(BF16).astype(F32)
        terms.append(t)
        x = x - t
    terms.append(x)
    return terms


def _scale_rows(rows):
    return jnp.concatenate(_split_terms(rows[R_EY:R_EY + N_DH], 2) + _split_terms(rows[R_WX:R_WX + N_DH], 2),
                           axis=0).astype(BF16)


def _conv_silu(ext_refs, s, p0, cw, cbias):
    sl = slice(s * LANE, (s + 1) * LANE)
    acc = cbias[:, sl].astype(BF16)
    for k in range(D_CONV):
        pair = p0 + (HALO + k - D_CONV // 2 - k % 2) // 2
        tap = pltpu.bitcast(ext_refs[k % 2][s, pl.ds(pair, CHUNK // 2), :], BF16)
        acc = acc + cw[k:k + 1, sl].astype(BF16) * tap
    return acc * jax.nn.sigmoid(acc)


def _ssd_kernel(x_ref, b_ref, c_ref, xp_ref, bp_ref, cp_ref, xn_ref, bn_ref, cn_ref, z_ref, row_ref,
                cx_ref, cb_ref, crow_ref, cwx_ref, cwb_ref, cwc_ref, cbx_ref, cbb_ref, cbc_ref,
                dskip_ref, gain_ref, onesel_ref, sel_ref, out_ref,
                cache_ref, yacc_ref, sf_ref, sb_ref, ext0_ref, ext1_ref, *buf_refs, tl, n_blk, ctx_len):
    p = pl.program_id(2)
    i = pl.program_id(3)
    n_ch = tl // CHUNK
    args_bufs = buf_refs[0:N_BUFS]
    ew_bufs = buf_refs[N_BUFS:2 * N_BUFS]
    st_bufs = buf_refs[2 * N_BUFS:3 * N_BUFS]
    ext_refs = (ext0_ref, ext1_ref)
    cw = jnp.concatenate([cwx_ref[...], cwb_ref[...], cwc_ref[...]], axis=1)
    cbias = jnp.concatenate([cbx_ref[...], cbb_ref[...], cbc_ref[...]], axis=1)

    def slabs_of(x, bm, cm):
        return [x[:, :LANE], x[:, LANE:], bm, cm]

    def slab_refs(xr, br, cr):
        return [(xr, 0), (xr, LANE), (br, 0), (cr, 0)]

    def fill_ext(src, first, last, n_rows):
        half = HALO // 2
        n_piece = n_rows // CHUNK

        def copy_piece(k, carry):
            r = pl.multiple_of(k * CHUNK, CHUNK)
            q = pl.multiple_of(k * (CHUNK // 2), CHUNK // 2)
            for s, (ref, lane0) in enumerate(src):
                ext0_ref[s, pl.ds(half + q, CHUNK // 2), :] = pltpu.bitcast(
                    ref[0, pl.ds(r, CHUNK), lane0:lane0 + LANE], jnp.uint32)
            return carry

        def shift_piece(k, carry):
            q = pl.multiple_of(k * (CHUNK // 2), CHUNK // 2)
            for s in range(len(src)):
                lo = ext0_ref[s, pl.ds(q, CHUNK // 2), :]
                hi = ext0_ref[s, pl.ds(q + 1, CHUNK // 2), :]
                ext1_ref[s, pl.ds(q, CHUNK // 2), :] = (lo >> 16) | (hi << 16)
            return carry

        for s in range(len(src)):
            ext0_ref[s, 0:half, :] = pltpu.bitcast(first[s], jnp.uint32)
            ext0_ref[s, half + n_rows // 2:2 * half + n_rows // 2, :] = pltpu.bitcast(last[s], jnp.uint32)
        lax.fori_loop(0, n_piece, copy_piece, 0, unroll=2)
        lax.fori_loop(0, n_piece, shift_piece, 0, unroll=2)
        for s in range(len(src)):
            q = n_rows // 2
            ext1_ref[s, q:q + half, :] = (ext0_ref[s, q:q + half, :] >> 16) | (ext0_ref[s, q + 1:q + half + 1, :] << 16)

    def row_of(c):
        return c * CHUNK if isinstance(c, int) else pl.multiple_of(c * CHUNK, CHUNK)

    def expand(rowt, d):
        return _dot_tn(_scale_rows(rowt), sel_ref[d])

    def run_pipelined(prepare_parts, consume_parts, weave, order, per_trip, aheads):
        def step(k, d, in_loop):
            handed = None
            for item in weave.split():
                n = int(item[1:])
                if item[0] == "P":
                    if in_loop or k + aheads[n] < n_ch:
                        prepare_parts[n](order(k + aheads[n]), (d + aheads[n]) % N_BUFS)
                else:
                    handed = consume_parts[n](order(k), d % N_BUFS, handed)

        for n, part in enumerate(prepare_parts):
            for a in range(aheads[n]):
                part(order(a), a)

        def trip(t, carry):
            for d in range(per_trip):
                step(per_trip * t + d, d, True)
            return carry

        lax.fori_loop(0, n_ch // per_trip - 1, trip, 0)
        for d in range(per_trip):
            step(n_ch - per_trip + d, d, False)

    @pl.when(jnp.logical_and(p == 0, i == 0))
    def _init_from_context():
        zpad = [jnp.zeros((HALO, LANE), BF16)] * (N_SLAB - 1)
        fill_ext(slab_refs(cx_ref, cb_ref, None)[:N_SLAB - 1], zpad, zpad, ctx_len)
        n_cc = ctx_len // CHUNK
        sf = jnp.zeros((D_STATE, GROUP_W), F32)
        st_b, ey_b = [], []
        for cc in range(n_cc):
            slabs = [_conv_silu(ext_refs, s, cc * CHUNK // 2, cw, cbias) for s in range(N_SLAB - 1)]
            xs = jnp.concatenate(slabs[:2], axis=1).astype(F32)
            bm = slabs[2]
            rowt = crow_ref[0, :, cc * CHUNK:(cc + 1) * CHUNK]
            ewf = expand(rowt, 0)
            ewb = expand(rowt, 1)
            sf = sf * ewf[CHUNK - 1:CHUNK, :GROUP_W] + _dot_tn(bm, (xs * ewf[:, GROUP_W:]).astype(BF16))
            st_b.append(_dot_tn(bm, (xs * ewb[:, GROUP_W:]).astype(BF16)))
            ey_b.append(ewb[0:1, :GROUP_W])
        sb = jnp.zeros((D_STATE, GROUP_W), F32)
        for cc in reversed(range(n_cc)):
            sb = sb * ey_b[cc] + st_b[cc]
        sf_ref[...] = sf
        sb_ref[...] = sb

    @pl.when(p == 0)
    def _forward():
        blk = i
        base = blk * tl
        has_prev = blk > 0
        has_next = blk < n_blk - 1
        fill_ext(slab_refs(x_ref, b_ref, c_ref),
                 [jnp.where(has_prev, v, jnp.zeros_like(v)) for v in slabs_of(xp_ref[0], bp_ref[0], cp_ref[0])],
                 [jnp.where(has_next, v, jnp.zeros_like(v)) for v in slabs_of(xn_ref[0], bn_ref[0], cn_ref[0])],
                 tl)
        ii = lax.broadcasted_iota(jnp.int32, (CHUNK, CHUNK), 0)
        jj = lax.broadcasted_iota(jnp.int32, (CHUNK, CHUNK), 1)
        lower = jj < ii
        upper = jj > ii

        def prepare_conv(c, buf):
            r = row_of(c)
            p0 = c * (CHUNK // 2) if isinstance(c, int) else pl.multiple_of(c * (CHUNK // 2), CHUNK // 2)
            slabs = [_conv_silu(ext_refs, s, p0, cw, cbias) for s in range(N_SLAB)]
            for s in range(N_SLAB):
                cache_ref[pl.ds(base + r, CHUNK), s * LANE:(s + 1) * LANE] = slabs[s]
            yacc_ref[pl.ds(base + r, CHUNK), :] = jnp.concatenate(slabs[:2], axis=1).astype(F32) * dskip_ref[...]

        def prepare_scales(c, buf):
            rowt = row_ref[0, :, pl.ds(row_of(c), CHUNK)]
            onesel = onesel_ref[...]
            ones = jnp.ones((N_DH, CHUNK), F32)
            at = jnp.concatenate(_split_terms(rowt[R_CUM:R_CUM + N_DH], N_CUM_TERMS) + [ones] * N_CUM_TERMS,
                                 axis=0).astype(BF16)
            wide = [jnp.concatenate([t] * N_DH, axis=1) * onesel
                    for t in _split_terms(rowt[R_NCX:R_NCX + N_DH], N_CUM_TERMS)]
            bmat = jnp.concatenate([onesel] * N_CUM_TERMS + wide, axis=0).astype(BF16)
            args_bufs[buf][...] = _dot_tn(at, bmat)
            ew_bufs[buf][...] = expand(rowt, 0)

        def consume_state(c, buf, _):
            r = row_of(c)
            xbc_b = cache_ref[pl.ds(base + r, CHUNK), :]
            xs_b = xbc_b[:, :GROUP_W]
            bm = xbc_b[:, GROUP_W:GROUP_W + D_STATE]
            cm = xbc_b[:, GROUP_W + D_STATE:]
            return r, xs_b, bm, cm, _dot_nt(cm, bm)

        def consume_decay(c, buf, handed):
            r, xs_b, bm, cm, cbm = handed
            yd = []
            for j in range(HEADS_PER_GROUP):
                argf = args_bufs[buf][:, j * CHUNK:(j + 1) * CHUNK]
                argb = args_bufs[buf][:, (HEADS_PER_GROUP + j) * CHUNK:(HEADS_PER_GROUP + j + 1) * CHUNK]
                diag = row_ref[0, R_DIAG + j:R_DIAG + j + 1, pl.ds(r, CHUNK)]
                arg = jnp.where(lower, argf, jnp.where(upper, argb, diag))
                wmat = (cbm * jnp.exp2(arg)).astype(BF16)
                yd.append(_dot(wmat, xs_b[:, j * HEAD_DIM:(j + 1) * HEAD_DIM]))
            ef = ew_bufs[buf][:, :GROUP_W]
            wxf = ew_bufs[buf][:, GROUP_W:]
            sf = sf_ref[...]
            yacc_ref[pl.ds(base + r, CHUNK), :] += jnp.concatenate(yd, axis=1) + _dot(cm, sf.astype(BF16)) * ef
            sf_ref[...] = sf * ef[CHUNK - 1:CHUNK, :] + _dot_tn(bm, xs_b * wxf.astype(BF16))

        run_pipelined([prepare_conv, prepare_scales], [consume_state, consume_decay], FWD_WEAVE,
                      lambda k: k, FWD_PER_TRIP, FWD_AHEAD)

    @pl.when(p == 1)
    def _backward():
        blk = n_blk - 1 - i
        base = blk * tl

        def prepare(c, buf):
            r = row_of(c)
            ew = expand(row_ref[0, :, pl.ds(r, CHUNK)], 1)
            ew_bufs[buf][...] = ew
            xs_b = cache_ref[pl.ds(base + r, CHUNK), 0:GROUP_W]
            bm = cache_ref[pl.ds(base + r, CHUNK), GROUP_W:GROUP_W + D_STATE]
            st_bufs[buf][...] = _dot_tn(bm, xs_b * ew[:, GROUP_W:].astype(BF16))

        def consume(c, buf, _):
            r = row_of(c)
            cm = cache_ref[pl.ds(base + r, CHUNK), GROUP_W + D_STATE:]
            eb = ew_bufs[buf][:, :GROUP_W]
            sb = sb_ref[...]
            y = yacc_ref[pl.ds(base + r, CHUNK), :] + _dot(cm, sb.astype(BF16)) * eb
            sb_ref[...] = sb * eb[0:1, :] + st_bufs[buf][...]
            z = z_ref[0, pl.ds(r, CHUNK), :]
            y = y * (z * jax.nn.sigmoid(z)).astype(F32)
            out_ref[0, pl.ds(r, CHUNK), :] = _rmsnorm(y, gain_ref[...]).astype(BF16)

        run_pipelined([prepare], [consume], BWD_WEAVE, lambda k: n_ch - 1 - k, BWD_PER_TRIP, BWD_AHEAD)


def _ssd_selectors():
    onesel = (np.arange(N_DH * CHUNK)[None, :] // CHUNK == np.arange(N_DH)[:, None]).astype(np.float32)
    sel = np.zeros((2, 4 * N_DH, 2 * GROUP_W), np.float32)
    for d in range(2):
        for j in range(HEADS_PER_GROUP):
            dh = d * HEADS_PER_GROUP + j
            for term in range(2):
                sel[d, term * N_DH + dh, j * HEAD_DIM:(j + 1) * HEAD_DIM] = 1.0
                sel[d, (2 + term) * N_DH + dh, GROUP_W + j * HEAD_DIM:GROUP_W + (j + 1) * HEAD_DIM] = 1.0
    return jnp.asarray(onesel, F32), jnp.asarray(sel, BF16)


def _ssd(u, rows, cu, crows, conv_w, conv_b, dskip, gain, tl):
    b, l, _ = u.shape
    ctx_len = cu.shape[1]
    assert ctx_len <= tl
    n_blk = l // tl
    hb = tl // HALO
    last_halo = l // HALO - 1
    assert (tl // CHUNK) % FWD_PER_TRIP == 0 and (tl // CHUNK) % BWD_PER_TRIP == 0
    onesel, sel = _ssd_selectors()
    xg, bg, cg = 0, U_B // D_STATE, U_C // D_STATE

    def in_blk(p, i):
        return i * (1 - p) + (n_blk - 1) * p

    def seq_blk(p, i):
        return i * (1 - p) + (n_blk - 1 - i) * p

    def out_blk(p, i):
        return (n_blk - 1) * (1 - p) + (n_blk - 1 - i) * p

    def prev_halo(p, i):
        return jnp.maximum(in_blk(p, i) * hb - 1, 0)

    def next_halo(p, i):
        return jnp.minimum((in_blk(p, i) + 1) * hb, last_halo)

    def xbc_specs(rows_blk, row_index):
        return [pl.BlockSpec((1, rows_blk, GROUP_W), lambda bb, g, p, i: (bb, row_index(p, i), xg + g)),
                pl.BlockSpec((1, rows_blk, D_STATE), lambda bb, g, p, i: (bb, row_index(p, i), bg + g)),
                pl.BlockSpec((1, rows_blk, D_STATE), lambda bb, g, p, i: (bb, row_index(p, i), cg + g))]

    def param_specs(n_rows):
        return [pl.BlockSpec((n_rows, GROUP_W), lambda bb, g, p, i: (0, xg + g)),
                pl.BlockSpec((n_rows, D_STATE), lambda bb, g, p, i: (0, bg + g)),
                pl.BlockSpec((n_rows, D_STATE), lambda bb, g, p, i: (0, cg + g))]

    grid = (b, N_GROUPS, 2, n_blk)
    in_specs = (
        xbc_specs(tl, in_blk) + xbc_specs(HALO, prev_halo) + xbc_specs(HALO, next_halo)
        + [pl.BlockSpec((1, tl, GROUP_W), lambda bb, g, p, i: (bb, out_blk(p, i), U_Z // GROUP_W + g)),
           pl.BlockSpec((1, ROWS_PER_GROUP, tl), lambda bb, g, p, i: (bb, g, seq_blk(p, i)))]
        + xbc_specs(ctx_len, lambda p, i: 0)[:2]
        + [pl.BlockSpec((1, ROWS_PER_GROUP, ctx_len), lambda bb, g, p, i: (bb, g, 0))]
        + param_specs(SUBLANE) + param_specs(1)
        + [pl.BlockSpec((1, GROUP_W), lambda bb, g, p, i: (0, g)),
           pl.BlockSpec((1, GROUP_W), lambda bb, g, p, i: (0, g)),
           pl.BlockSpec(onesel.shape, lambda bb, g, p, i: (0, 0)),
           pl.BlockSpec(sel.shape, lambda bb, g, p, i: (0, 0, 0))])
    return pl.pallas_call(
        functools.partial(_ssd_kernel, tl=tl, n_blk=n_blk, ctx_len=ctx_len),
        grid=grid,
        in_specs=in_specs,
        out_specs=pl.BlockSpec((1, tl, GROUP_W), lambda bb, g, p, i: (bb, out_blk(p, i), g)),
        out_shape=jax.ShapeDtypeStruct((b, l, D_INNER), BF16),
        scratch_shapes=[pltpu.VMEM((l, N_SLAB * LANE), BF16),
                        pltpu.VMEM((l, GROUP_W), F32),
                        pltpu.VMEM((D_STATE, GROUP_W), F32),
                        pltpu.VMEM((D_STATE, GROUP_W), F32),
                        pltpu.VMEM((N_SLAB, (tl + 2 * HALO) // 2, LANE), jnp.uint32),
                        pltpu.VMEM((N_SLAB, (tl + 2 * HALO) // 2, LANE), jnp.uint32)]
        + [pltpu.VMEM((CHUNK, N_DH * CHUNK), F32)] * N_BUFS
        + [pltpu.VMEM((CHUNK, 2 * GROUP_W), F32)] * N_BUFS
        + [pltpu.VMEM((D_STATE, GROUP_W), F32)] * N_BUFS,
        compiler_params=pltpu.CompilerParams(
            dimension_semantics=("arbitrary", "arbitrary", "arbitrary", "arbitrary"),
            vmem_limit_bytes=VMEM_LIMIT),
        name="ssd",
    )(u, u, u, u, u, u, u, u, u, u, rows, cu, cu, crows,
      conv_w, conv_w, conv_w, conv_b, conv_b, conv_b, dskip, gain, onesel, sel)


def _dft_mats():
    def cs(n):
        k = np.arange(n)
        ang = 2.0 * np.pi * np.outer(k, k) / n
        return np.cos(ang), np.sin(ang)

    c3, s3 = cs(FFT_GROUP_DIM)
    w1 = np.concatenate([c3, -s3], axis=1)
    c2, s2 = cs(GRID_W)
    m2 = np.block([[c2, s2], [-s2, c2]])
    return w1, m2


def _fft12_kernel(f_ref, w1_ref, m2_ref, o_ref, *, tm):
    f = f_ref[0]
    w1 = w1_ref[...]
    ps, qs = [], []
    for g in range(FFT_GROUPS):
        pq = _dot(f[:, g * FFT_GROUP_DIM:(g + 1) * FFT_GROUP_DIM], w1)
        ps.append(pq[:, :FFT_GROUP_DIM])
        qs.append(pq[:, FFT_GROUP_DIM:])
    pr = jnp.concatenate(ps, axis=1).astype(BF16)
    qr = jnp.concatenate(qs, axis=1).astype(BF16)
    m2 = m2_ref[...]
    pad = jnp.zeros((FFT_PITCH - GRID_W, D_MODEL), jnp.uint32)
    for r in range(tm // GRID_W):
        sl = slice(r * GRID_W, (r + 1) * GRID_W)
        o = _dot(m2, jnp.concatenate([pr[sl], qr[sl]], axis=0))
        r0 = r * FFT_PITCH
        o_ref[0, r0:r0 + GRID_W, :] = _pack_complex(o[:GRID_W], o[GRID_W:])
        o_ref[0, r0 + GRID_W:r0 + FFT_PITCH, :] = pad


def _pack_complex(re, im):
    re_bits = pltpu.bitcast(re.astype(BF16).astype(F32), jnp.uint32)
    im_bits = pltpu.bitcast(im.astype(BF16).astype(F32), jnp.uint32)
    return (re_bits >> 16) | (im_bits & jnp.uint32(0xFFFF0000))


def _unpack_complex(word):
    re = pltpu.bitcast(word << 16, F32)
    im = pltpu.bitcast(word & jnp.uint32(0xFFFF0000), F32)
    return re, im


def _fft3_kernel(ri_ref, m3_ref, o_ref, scr_ref, *, rows):
    m3 = m3_ref[...]

    def column_pair(t, carry):
        w = 2 * t
        cols = [jnp.concatenate(_unpack_complex(ri_ref[0, pl.ds(w + dw, rows, stride=FFT_PITCH), :]), axis=0)
                for dw in range(2)]
        o = _dot(m3, jnp.concatenate(cols, axis=1).astype(BF16))
        for dw in range(2):
            scr_ref[pl.ds(w + dw, rows, stride=FFT_PITCH), :] = o[:, dw * LANE:(dw + 1) * LANE]
        return carry

    lax.fori_loop(0, GRID_W // 2, column_pair, 0, unroll=4)
    for k in range(rows):
        o_ref[0, k * GRID_W:(k + 1) * GRID_W, :] = scr_ref[k * FFT_PITCH:k * FFT_PITCH + GRID_W, :].astype(BF16)


def _fourier(u, tm):
    b, l, _ = u.shape
    rows = l // GRID_W
    w1, m2 = _dft_mats()
    ri = pl.pallas_call(
        functools.partial(_fft12_kernel, tm=tm),
        grid=(b, l // tm),
        in_specs=[pl.BlockSpec((1, tm, D_MODEL), lambda i, j: (i, j, U_FFT // D_MODEL)),
                  pl.BlockSpec((FFT_GROUP_DIM, 2 * FFT_GROUP_DIM), lambda i, j: (0, 0)),
                  pl.BlockSpec((2 * GRID_W, 2 * GRID_W), lambda i, j: (0, 0))],
        out_specs=pl.BlockSpec((1, tm // GRID_W * FFT_PITCH, D_MODEL), lambda i, j: (i, j, 0)),
        out_shape=jax.ShapeDtypeStruct((b, rows * FFT_PITCH, D_MODEL), jnp.uint32),
        compiler_params=pltpu.CompilerParams(dimension_semantics=("arbitrary", "arbitrary"),
                                             vmem_limit_bytes=VMEM_LIMIT),
        name="fft12",
    )(u, jnp.asarray(w1, BF16), jnp.asarray(m2, BF16))

    k = np.arange(rows)
    ang = 2.0 * np.pi * np.outer(k, k) / rows
    scale = 1.0 / np.sqrt(float(rows * GRID_W * FFT_GROUP_DIM))
    m3 = np.concatenate([np.cos(ang), np.sin(ang)], axis=1) * scale
    return pl.pallas_call(
        functools.partial(_fft3_kernel, rows=rows),
        grid=(b, D_MODEL // LANE),
        in_specs=[pl.BlockSpec((1, rows * FFT_PITCH, LANE), lambda i, j: (i, 0, j)),
                  pl.BlockSpec((rows, 2 * rows), lambda i, j: (0, 0))],
        out_specs=pl.BlockSpec((1, l, LANE), lambda i, j: (i, 0, j)),
        out_shape=jax.ShapeDtypeStruct((b, l, D_MODEL), BF16),
        scratch_shapes=[pltpu.VMEM((rows * FFT_PITCH, LANE), F32)],
        compiler_params=pltpu.CompilerParams(dimension_semantics=("arbitrary", "arbitrary"),
                                             vmem_limit_bytes=VMEM_LIMIT),
        name="fft3",
    )(ri, jnp.asarray(m3, BF16))


FF_SPLIT = 2


def _tail_kernel(y_ref, f_ref, ga_ref, gb_ref, x_ref, g1_ref, sh2_ref, sc2_ref, g2_ref,
                 wso_ref, wfo_ref, wo_ref, n2_ref, wff_ref, wfd_ref, fg_ref, o_ref):
    a = _dot(y_ref[...], wso_ref[...])
    bq = _dot(f_ref[...], wfo_ref[...])
    merged = jax.nn.sigmoid(ga_ref[...].astype(F32)) * a + jax.nn.sigmoid(gb_ref[...].astype(F32)) * bq
    xn = x_ref[...] + g1_ref[0] * _dot(merged.astype(BF16), wo_ref[...])
    h = (_rmsnorm(xn, n2_ref[...]) * (1.0 + sc2_ref[0]) + sh2_ref[0]).astype(BF16)
    ffw = D_FF // FF_SPLIT
    acc = None
    for s in range(FF_SPLIT):
        gate = _dot(h, wff_ref[:, s * ffw:(s + 1) * ffw])
        up = _dot(h, wff_ref[:, D_FF + s * ffw:D_FF + (s + 1) * ffw])
        act = (gate * jax.nn.sigmoid(gate) * up).astype(BF16)
        part = _dot(act, wfd_ref[s * ffw:(s + 1) * ffw, :])
        acc = part if acc is None else acc + part
    o_ref[...] = _rmsnorm(xn + g2_ref[0] * acc, fg_ref[...])


def _tail(y2d, f2d, u2d, x2d, g1, sh2, sc2, g2, wso, wfo, wo, n2, wff, wfd, fg, tm, rows_per_mod):
    m = x2d.shape[0]
    bpm = rows_per_mod // tm

    def const(shape):
        return pl.BlockSpec(shape, lambda i: (0, 0), pipeline_mode=pl.Buffered(1))

    def mod():
        return pl.BlockSpec((1, 1, D_MODEL), lambda i: (i // bpm, 0, 0))

    return pl.pallas_call(
        _tail_kernel,
        grid=(m // tm,),
        in_specs=[pl.BlockSpec((tm, D_INNER), lambda i: (i, 0)),
                  pl.BlockSpec((tm, D_MODEL), lambda i: (i, 0)),
                  pl.BlockSpec((tm, D_MODEL), lambda i: (i, U_GATE // D_MODEL)),
                  pl.BlockSpec((tm, D_MODEL), lambda i: (i, U_GATE // D_MODEL + 1)),
                  pl.BlockSpec((tm, D_MODEL), lambda i: (i, 0)),
                  mod(), mod(), mod(), mod(),
                  const((D_INNER, D_MODEL)), const((D_MODEL, D_MODEL)), const((D_MODEL, D_MODEL)),
                  const((1, D_MODEL)),
                  const((D_MODEL, 2 * D_FF)), const((D_FF, D_MODEL)),
                  const((1, D_MODEL))],
        out_specs=pl.BlockSpec((tm, D_MODEL), lambda i: (i, 0)),
        out_shape=jax.ShapeDtypeStruct((m, D_MODEL), F32),
        compiler_params=pltpu.CompilerParams(dimension_semantics=("arbitrary",),
                                             vmem_limit_bytes=VMEM_LIMIT),
        name="tail",
    )(y2d, f2d, u2d, u2d, x2d, g1, sh2, sc2, g2, wso, wfo, wo, n2, wff, wfd, fg)


def _dt_lanes(a):
    lead = a.shape[:-1]
    a = a.reshape(lead + (2, N_GROUPS, HEADS_PER_GROUP))
    a = jnp.swapaxes(a, -3, -2).reshape(lead + (2 * N_HEADS,))
    return jnp.concatenate([a, jnp.zeros(lead + (LANE - 2 * N_HEADS,), a.dtype)], axis=-1)


def _block(x, c, ctx, c_ctx, w_ada, b_ada, norm1_g, w_in, conv_w, conv_b, dt_bias, a_log, d_skip,
           ssd_norm_g, w_ssd_out, w_fft_out, w_o, norm2_g, w_ffn_in, w_ffn_out, final_g,
           proj_tm, prep_tl, ssd_tl, fft_tm, tail_tm):
    b, l, d = x.shape
    ctx_len = ctx.shape[1]
    assert d == D_MODEL and l % (GRID_W * 2) == 0 and ctx_len % CHUNK == 0 and b + 1 <= SUBLANE

    w_in_t = w_in[0].T
    w_main_t = w_in_t.astype(BF16)
    w_dt = _dt_lanes(w_in_t[REF_DT:REF_Z].T).T
    w_dt_hi = w_dt.astype(BF16)
    w_dt_lo = (w_dt - w_dt_hi.astype(F32)).astype(BF16)
    w_dt2 = jnp.concatenate([w_dt_hi, w_dt_lo], axis=0)
    conv_w_p = jnp.concatenate([conv_w[0], jnp.zeros((SUBLANE - D_CONV, CONV_DIM), F32)], axis=0)
    conv_b_p = conv_b[0].reshape(1, CONV_DIM)
    bias_v = _dt_lanes(dt_bias[0].reshape(-1)).reshape(1, LANE)
    alog_v = _dt_lanes(a_log[0].reshape(-1)).reshape(1, LANE)
    dskip_v = jnp.repeat(d_skip[0], HEAD_DIM).reshape(1, D_INNER)
    gain_v = ssd_norm_g[0].reshape(1, D_INNER)
    wso = w_ssd_out[0].astype(BF16)
    wfo = w_fft_out[0].astype(BF16)
    wo = w_o[0].astype(BF16)
    wff = w_ffn_in[0].astype(BF16)
    wfd = w_ffn_out[0].astype(BF16)

    cvecs = jnp.concatenate([c, c_ctx[None, :], jnp.zeros((SUBLANE - b - 1, D_MODEL), F32)], axis=0)
    mods = _mods(cvecs, b + 1, w_ada[0], b_ada[0])
    lat = mods[:b].reshape(b, 6, 1, D_MODEL)
    sh1, sc1, g1, sh2, sc2, g2 = (lat[:, k] for k in range(6))
    cm = mods[b].reshape(6, 1, 1, D_MODEL)

    x2d = x.reshape(b * l, D_MODEL)
    n1 = norm1_g[0].reshape(1, D_MODEL)
    u2d, dt_raw = _inproj(x2d, sh1, sc1, n1, w_main_t, w_dt2, U_WIDTH, proj_tm, l)
    u = u2d.reshape(b, l, U_WIDTH)
    cu2d, cdt_raw = _inproj(ctx.reshape(b * ctx_len, D_MODEL), cm[0], cm[1], n1, w_main_t, w_dt2,
                            CONV_DIM, b * ctx_len, b * ctx_len)
    cu = cu2d.reshape(b, ctx_len, CONV_DIM)

    rows = _dtprep(dt_raw.reshape(b, l, LANE), bias_v, alog_v, prep_tl)
    crows = _dtprep(cdt_raw.reshape(b, ctx_len, LANE), bias_v, alog_v, ctx_len)
    y_ssd = _ssd(u, rows, cu, crows, conv_w_p, conv_b_p, dskip_v, gain_v, ssd_tl)

    f_mix = _fourier(u, fft_tm)

    out = _tail(y_ssd.reshape(b * l, D_INNER), f_mix.reshape(b * l, D_MODEL), u2d, x2d,
                g1, sh2, sc2, g2, wso, wfo, wo, norm2_g[0].reshape(1, D_MODEL), wff, wfd,
                final_g.reshape(1, D_MODEL), tail_tm, l)
    return out.reshape(b, l, D_MODEL)


def kernel(x, c, ctx, c_ctx, w_ada, b_ada, norm1_g, w_in, conv_w, conv_b, dt_bias, a_log, d_skip,
           ssd_norm_g, w_ssd_out, w_fft_out, w_o, norm2_g, w_ffn_in, w_ffn_out, final_g):
    return _block(x, c, ctx, c_ctx, w_ada, b_ada, norm1_g, w_in, conv_w, conv_b, dt_bias, a_log, d_skip,
                  ssd_norm_g, w_ssd_out, w_fft_out, w_o, norm2_g, w_ffn_in, w_ffn_out, final_g,
                  proj_tm=2048, prep_tl=1024, ssd_tl=4096, fft_tm=512, tail_tm=512)
```

```python
import functools
import math

import jax
import jax.numpy as jnp
import numpy as np
from jax import lax
from jax.experimental import pallas as pl
from jax.experimental.pallas import tpu as pltpu

F32 = jnp.float32
BF16 = jnp.bfloat16

D_MODEL = 1024
EPS = 1e-6
GRID_W = 64

N_GROUPS = 8
HEADS_PER_GROUP = 4
HEAD_DIM = 64
N_HEADS = N_GROUPS * HEADS_PER_GROUP
D_STATE = 128
D_INNER = N_HEADS * HEAD_DIM
GROUP_W = HEADS_PER_GROUP * HEAD_DIM
GS = N_GROUPS * D_STATE
CONV_DIM = D_INNER + 2 * GS
D_CONV = 5
CHUNK = 128

FFT_GROUPS = 8
FFT_GROUP_DIM = 128
D_FF = 2816

REF_DT = CONV_DIM
REF_Z = REF_DT + 2 * N_HEADS

U_B = D_INNER
U_C = U_B + GS
U_Z = CONV_DIM
U_FFT = U_Z + D_INNER
U_GATE = U_FFT + D_MODEL
U_WIDTH = U_GATE + 2 * D_MODEL

LANE = 128
SUBLANE = 8
HALO = 16
VMEM_LIMIT = 56 * 1024 * 1024

N_DH = 2 * HEADS_PER_GROUP
R_CUM, R_NCX, R_EY, R_WX, R_DIAG = 0, 8, 16, 24, 32
ROWS_PER_GROUP = 40
N_SLAB = (GROUP_W + 2 * D_STATE) // LANE
DT_FLOOR = 1e-37
LOG2E = math.log2(math.e)
N_CUM_TERMS = 3
FWD_PER_TRIP = 4
BWD_PER_TRIP = 8
N_BUFS = 4
FWD_AHEAD = (1, 1)
BWD_AHEAD = (3,)
FWD_WEAVE = "C0 C1 P0 P1"
BWD_WEAVE = "P0 C0"
K_ARGS = 2 * N_CUM_TERMS * N_DH
FFT_PITCH = GRID_W + 4


def _dot(a, b):
    return jnp.dot(a, b, preferred_element_type=F32)


def _dot_tn(a, b):
    return lax.dot_general(a, b, (((0,), (0,)), ((), ())), preferred_element_type=F32)


def _dot_nt(a, b):
    return lax.dot_general(a, b, (((1,), (1,)), ((), ())), preferred_element_type=F32)


def _rmsnorm(x, g):
    return x * lax.rsqrt(jnp.mean(x * x, axis=-1, keepdims=True) + EPS) * g


def _mods_kernel(ct_ref, w_ref, b_ref, o_ref, *, n_vecs):
    s = ct_ref[...]
    s = s * jax.nn.sigmoid(s)
    w = w_ref[...]
    rows = [jnp.sum(w * s[:, r:r + 1], axis=0, keepdims=True) for r in range(n_vecs)]
    rows.append(jnp.zeros((SUBLANE - n_vecs, w.shape[1]), F32))
    o_ref[...] = jnp.concatenate(rows, axis=0) + b_ref[...]


def _mods(cvecs, n_vecs, w_ada, b_ada):
    n = w_ada.shape[1]
    tn = 1536
    return pl.pallas_call(
        functools.partial(_mods_kernel, n_vecs=n_vecs),
        grid=(n // tn,),
        in_specs=[pl.BlockSpec((D_MODEL, SUBLANE), lambda j: (0, 0)),
                  pl.BlockSpec((D_MODEL, tn), lambda j: (0, j)),
                  pl.BlockSpec((1, tn), lambda j: (0, j))],
        out_specs=pl.BlockSpec((SUBLANE, tn), lambda j: (0, j)),
        out_shape=jax.ShapeDtypeStruct((SUBLANE, n), F32),
        compiler_params=pltpu.CompilerParams(dimension_semantics=("arbitrary",),
                                             vmem_limit_bytes=VMEM_LIMIT),
        name="mods",
    )(cvecs.T, w_ada, b_ada.reshape(1, n))


def _inproj_kernel(x_ref, sh_ref, sc_ref, g_ref, w_ref, wdt_ref, u_ref, dt_ref, h_ref):
    @pl.when(pl.program_id(1) == 0)
    def _():
        h = _rmsnorm(x_ref[...], g_ref[...]) * (1.0 + sc_ref[0]) + sh_ref[0]
        hb = h.astype(BF16)
        h_ref[...] = hb
        hl = (h - hb.astype(F32)).astype(BF16)
        a = _dot_nt(hb, wdt_ref[...])
        dt_ref[...] = a[:, :LANE] + a[:, LANE:] + _dot_nt(hl, wdt_ref[:LANE, :])

    u_ref[...] = _dot_nt(h_ref[...], w_ref[...]).astype(BF16)


def _inproj(x2d, shift, scale, gain, w_t, w_dt_t, n_cols, tm, rows_per_mod):
    m = x2d.shape[0]
    tn = 1024
    bpm = rows_per_mod // tm

    def w_rows(i, j):
        return pl.multiple_of(j * tn + jnp.where(j * tn >= REF_DT, REF_Z - REF_DT, 0), 2 * HALO), 0

    return pl.pallas_call(
        _inproj_kernel,
        grid=(m // tm, n_cols // tn),
        in_specs=[pl.BlockSpec((tm, D_MODEL), lambda i, j: (i, 0)),
                  pl.BlockSpec((1, 1, D_MODEL), lambda i, j: (i // bpm, 0, 0)),
                  pl.BlockSpec((1, 1, D_MODEL), lambda i, j: (i // bpm, 0, 0)),
                  pl.BlockSpec((1, D_MODEL), lambda i, j: (0, 0)),
                  pl.BlockSpec((pl.Element(tn), pl.Element(D_MODEL)), w_rows),
                  pl.BlockSpec((2 * LANE, D_MODEL), lambda i, j: (0, 0))],
        out_specs=[pl.BlockSpec((tm, tn), lambda i, j: (i, j)),
                   pl.BlockSpec((tm, LANE), lambda i, j: (i, 0))],
        out_shape=[jax.ShapeDtypeStruct((m, n_cols), BF16),
                   jax.ShapeDtypeStruct((m, LANE), F32)],
        scratch_shapes=[pltpu.VMEM((tm, D_MODEL), BF16)],
        compiler_params=pltpu.CompilerParams(dimension_semantics=("arbitrary", "arbitrary"),
                                             vmem_limit_bytes=VMEM_LIMIT),
        name="inproj",
    )(x2d, shift, scale, gain, w_t, w_dt_t)


def _dtprep_kernel(raw_ref, bias_ref, alog_ref, row_ref, *, n_chunks):
    a = -jnp.exp(alog_ref[...])
    bias = bias_ref[...]
    ii = lax.broadcasted_iota(jnp.int32, (CHUNK, CHUNK), 0)
    jj = lax.broadcasted_iota(jnp.int32, (CHUNK, CHUNK), 1)
    tril = (jj <= ii).astype(BF16)
    triu = (jj >= ii).astype(BF16)
    lane = lax.broadcasted_iota(jnp.int32, (1, LANE), 1)
    is_fwd = (lane % N_DH) < HEADS_PER_GROUP

    def tri_sum(tri, terms):
        s = _dot(tri, terms)
        return s[:, :LANE] + s[:, LANE:2 * LANE] + s[:, 2 * LANE:]

    for c in range(n_chunks):
        sl = slice(c * CHUNK, (c + 1) * CHUNK)
        dt = jax.nn.softplus(raw_ref[0, sl, :] + bias)
        dta = dt * a
        terms = jnp.concatenate(_split_terms(dta, N_CUM_TERMS), axis=1).astype(BF16)
        cf = tri_sum(tril, terms)
        cb = tri_sum(triu, terms)
        cum = jnp.where(is_fwd, cf, cb)
        tot = jnp.where(is_fwd, cf[CHUNK - 1:CHUNK, :], cb[0:1, :])
        wx = dt * jnp.exp(tot - cum)
        ey = jnp.exp(cum)
        ncx = jnp.log(jnp.maximum(dt, DT_FLOOR)) - cum
        diag = jnp.log(dt + pltpu.roll(dt, LANE - HEADS_PER_GROUP, 1))
        for r_off, val in ((R_CUM, cum * LOG2E), (R_NCX, ncx * LOG2E), (R_EY, ey), (R_WX, wx),
                           (R_DIAG, diag * LOG2E)):
            vt = val.T
            for g in range(N_GROUPS):
                r0 = g * ROWS_PER_GROUP + r_off
                row_ref[0, r0:r0 + N_DH, sl] = vt[g * N_DH:(g + 1) * N_DH, :]


def _dtprep(raw, bias, alog, tl):
    b, l, _ = raw.shape
    return pl.pallas_call(
        functools.partial(_dtprep_kernel, n_chunks=tl // CHUNK),
        grid=(b, l // tl),
        in_specs=[pl.BlockSpec((1, tl, LANE), lambda i, j: (i, j, 0)),
                  pl.BlockSpec((1, LANE), lambda i, j: (0, 0)),
                  pl.BlockSpec((1, LANE), lambda i, j: (0, 0))],
        out_specs=pl.BlockSpec((1, N_GROUPS * ROWS_PER_GROUP, tl), lambda i, j: (i, 0, j)),
        out_shape=jax.ShapeDtypeStruct((b, N_GROUPS * ROWS_PER_GROUP, l), F32),
        compiler_params=pltpu.CompilerParams(dimension_semantics=("arbitrary", "arbitrary"),
                                             vmem_limit_bytes=VMEM_LIMIT),
        name="dtprep",
    )(raw, bias, alog)


def _split_terms(x, n):
    terms = []
    for _ in range(n - 1):
        t = x.astype(BF16).astype(F32)
        terms.append(t)
        x = x - t
    terms.append(x)
    return terms


def _scale_rows(rows):
    return jnp.concatenate(_split_terms(rows[R_EY:R_EY + N_DH], 2) + _split_terms(rows[R_WX:R_WX + N_DH], 2),
                           axis=0).astype(BF16)


def _conv_silu(ext_refs, s, p0, cw, cbias):
    sl = slice(s * LANE, (s + 1) * LANE)
    acc = cbias[:, sl].astype(BF16)
    for k in range(D_CONV):
        pair = p0 + (HALO + k - D_CONV // 2 - k % 2) // 2
        tap = pltpu.bitcast(ext_refs[k % 2][s, pl.ds(pair, CHUNK // 2), :], BF16)
        acc = acc + cw[k:k + 1, sl].astype(BF16) * tap
    return acc * jax.nn.sigmoid(acc)


def _ssd_kernel(x_ref, b_ref, c_ref, xp_ref, bp_ref, cp_ref, xn_ref, bn_ref, cn_ref, z_ref, row_ref,
                cx_ref, cb_ref, crow_ref, cwx_ref, cwb_ref, cwc_ref, cbx_ref, cbb_ref, cbc_ref,
                dskip_ref, gain_ref, onesel_ref, sel_ref, out_ref,
                cache_ref, yacc_ref, sf_ref, sb_ref, ext0_ref, ext1_ref, *buf_refs, tl, n_blk, ctx_len):
    p = pl.program_id(2)
    i = pl.program_id(3)
    n_ch = tl // CHUNK
    args_bufs = buf_refs[0:N_BUFS]
    ew_bufs = buf_refs[N_BUFS:2 * N_BUFS]
    st_bufs = buf_refs[2 * N_BUFS:3 * N_BUFS]
    ext_refs = (ext0_ref, ext1_ref)
    cw = jnp.concatenate([cwx_ref[...], cwb_ref[...], cwc_ref[...]], axis=1)
    cbias = jnp.concatenate([cbx_ref[...], cbb_ref[...], cbc_ref[...]], axis=1)

    def slabs_of(x, bm, cm):
        return [x[:, :LANE], x[:, LANE:], bm, cm]

    def slab_refs(xr, br, cr):
        return [(xr, 0), (xr, LANE), (br, 0), (cr, 0)]

    def fill_ext(src, first, last, n_rows):
        half = HALO // 2
        n_piece = n_rows // CHUNK

        def copy_piece(k, carry):
            r = pl.multiple_of(k * CHUNK, CHUNK)
            q = pl.multiple_of(k * (CHUNK // 2), CHUNK // 2)
            for s, (ref, lane0) in enumerate(src):
                ext0_ref[s, pl.ds(half + q, CHUNK // 2), :] = pltpu.bitcast(
                    ref[0, pl.ds(r, CHUNK), lane0:lane0 + LANE], jnp.uint32)
            return carry

        def shift_piece(k, carry):
            q = pl.multiple_of(k * (CHUNK // 2), CHUNK // 2)
            for s in range(len(src)):
                lo = ext0_ref[s, pl.ds(q, CHUNK // 2), :]
                hi = ext0_ref[s, pl.ds(q + 1, CHUNK // 2), :]
                ext1_ref[s, pl.ds(q, CHUNK // 2), :] = (lo >> 16) | (hi << 16)
            return carry

        for s in range(len(src)):
            ext0_ref[s, 0:half, :] = pltpu.bitcast(first[s], jnp.uint32)
            ext0_ref[s, half + n_rows // 2:2 * half + n_rows // 2, :] = pltpu.bitcast(last[s], jnp.uint32)
        lax.fori_loop(0, n_piece, copy_piece, 0, unroll=2)
        lax.fori_loop(0, n_piece, shift_piece, 0, unroll=2)
        for s in range(len(src)):
            q = n_rows // 2
            ext1_ref[s, q:q + half, :] = (ext0_ref[s, q:q + half, :] >> 16) | (ext0_ref[s, q + 1:q + half + 1, :] << 16)

    def row_of(c):
        return c * CHUNK if isinstance(c, int) else pl.multiple_of(c * CHUNK, CHUNK)

    def expand(rowt, d):
        return _dot_tn(_scale_rows(rowt), sel_ref[d])

    def run_pipelined(prepare_parts, consume_parts, weave, order, per_trip, aheads):
        def step(k, d, in_loop):
            handed = None
            for item in weave.split():
                n = int(item[1:])
                if item[0] == "P":
                    if in_loop or k + aheads[n] < n_ch:
                        prepare_parts[n](order(k + aheads[n]), (d + aheads[n]) % N_BUFS)
                else:
                    handed = consume_parts[n](order(k), d % N_BUFS, handed)

        for n, part in enumerate(prepare_parts):
            for a in range(aheads[n]):
                part(order(a), a)

        def trip(t, carry):
            for d in range(per_trip):
                step(per_trip * t + d, d, True)
            return carry

        lax.fori_loop(0, n_ch // per_trip - 1, trip, 0)
        for d in range(per_trip):
            step(n_ch - per_trip + d, d, False)

    @pl.when(jnp.logical_and(p == 0, i == 0))
    def _init_from_context():
        zpad = [jnp.zeros((HALO, LANE), BF16)] * (N_SLAB - 1)
        fill_ext(slab_refs(cx_ref, cb_ref, None)[:N_SLAB - 1], zpad, zpad, ctx_len)
        n_cc = ctx_len // CHUNK
        sf = jnp.zeros((D_STATE, GROUP_W), F32)
        st_b, ey_b = [], []
        for cc in range(n_cc):
            slabs = [_conv_silu(ext_refs, s, cc * CHUNK // 2, cw, cbias) for s in range(N_SLAB - 1)]
            xs = jnp.concatenate(slabs[:2], axis=1).astype(F32)
            bm = slabs[2]
            rowt = crow_ref[0, :, cc * CHUNK:(cc + 1) * CHUNK]
            ewf = expand(rowt, 0)
            ewb = expand(rowt, 1)
            sf = sf * ewf[CHUNK - 1:CHUNK, :GROUP_W] + _dot_tn(bm, (xs * ewf[:, GROUP_W:]).astype(BF16))
            st_b.append(_dot_tn(bm, (xs * ewb[:, GROUP_W:]).astype(BF16)))
            ey_b.append(ewb[0:1, :GROUP_W])
        sb = jnp.zeros((D_STATE, GROUP_W), F32)
        for cc in reversed(range(n_cc)):
            sb = sb * ey_b[cc] + st_b[cc]
        sf_ref[...] = sf
        sb_ref[...] = sb

    @pl.when(p == 0)
    def _forward():
        blk = i
        base = blk * tl
        has_prev = blk > 0
        has_next = blk < n_blk - 1
        fill_ext(slab_refs(x_ref, b_ref, c_ref),
                 [jnp.where(has_prev, v, jnp.zeros_like(v)) for v in slabs_of(xp_ref[0], bp_ref[0], cp_ref[0])],
                 [jnp.where(has_next, v, jnp.zeros_like(v)) for v in slabs_of(xn_ref[0], bn_ref[0], cn_ref[0])],
                 tl)
        ii = lax.broadcasted_iota(jnp.int32, (CHUNK, CHUNK), 0)
        jj = lax.broadcasted_iota(jnp.int32, (CHUNK, CHUNK), 1)
        lower = jj < ii
        upper = jj > ii

        def prepare_conv(c, buf):
            r = row_of(c)
            p0 = c * (CHUNK // 2) if isinstance(c, int) else pl.multiple_of(c * (CHUNK // 2), CHUNK // 2)
            slabs = [_conv_silu(ext_refs, s, p0, cw, cbias) for s in range(N_SLAB)]
            for s in range(N_SLAB):
                cache_ref[pl.ds(base + r, CHUNK), s * LANE:(s + 1) * LANE] = slabs[s]
            yacc_ref[pl.ds(base + r, CHUNK), :] = jnp.concatenate(slabs[:2], axis=1).astype(F32) * dskip_ref[...]

        def prepare_scales(c, buf):
            rowt = row_ref[0, :, pl.ds(row_of(c), CHUNK)]
            onesel = onesel_ref[...]
            ones = jnp.ones((N_DH, CHUNK), F32)
            at = jnp.concatenate(_split_terms(rowt[R_CUM:R_CUM + N_DH], N_CUM_TERMS) + [ones] * N_CUM_TERMS,
                                 axis=0).astype(BF16)
            wide = [jnp.concatenate([t] * N_DH, axis=1) * onesel
                    for t in _split_terms(rowt[R_NCX:R_NCX + N_DH], N_CUM_TERMS)]
            bmat = jnp.concatenate([onesel] * N_CUM_TERMS + wide, axis=0).astype(BF16)
            args_bufs[buf][...] = _dot_tn(at, bmat)
            ew_bufs[buf][...] = expand(rowt, 0)

        def consume_state(c, buf, _):
            r = row_of(c)
            xbc_b = cache_ref[pl.ds(base + r, CHUNK), :]
            xs_b = xbc_b[:, :GROUP_W]
            bm = xbc_b[:, GROUP_W:GROUP_W + D_STATE]
            cm = xbc_b[:, GROUP_W + D_STATE:]
            return r, xs_b, bm, cm, _dot_nt(cm, bm)

        def consume_decay(c, buf, handed):
            r, xs_b, bm, cm, cbm = handed
            yd = []
            for j in range(HEADS_PER_GROUP):
                argf = args_bufs[buf][:, j * CHUNK:(j + 1) * CHUNK]
                argb = args_bufs[buf][:, (HEADS_PER_GROUP + j) * CHUNK:(HEADS_PER_GROUP + j + 1) * CHUNK]
                diag = row_ref[0, R_DIAG + j:R_DIAG + j + 1, pl.ds(r, CHUNK)]
                arg = jnp.where(lower, argf, jnp.where(upper, argb, diag))
                wmat = (cbm * jnp.exp2(arg)).astype(BF16)
                yd.append(_dot(wmat, xs_b[:, j * HEAD_DIM:(j + 1) * HEAD_DIM]))
            ef = ew_bufs[buf][:, :GROUP_W]
            wxf = ew_bufs[buf][:, GROUP_W:]
            sf = sf_ref[...]
            yacc_ref[pl.ds(base + r, CHUNK), :] += jnp.concatenate(yd, axis=1) + _dot(cm, sf.astype(BF16)) * ef
            sf_ref[...] = sf * ef[CHUNK - 1:CHUNK, :] + _dot_tn(bm, xs_b * wxf.astype(BF16))

        run_pipelined([prepare_conv, prepare_scales], [consume_state, consume_decay], FWD_WEAVE,
                      lambda k: k, FWD_PER_TRIP, FWD_AHEAD)

    @pl.when(p == 1)
    def _backward():
        blk = n_blk - 1 - i
        base = blk * tl

        def prepare(c, buf):
            r = row_of(c)
            ew = expand(row_ref[0, :, pl.ds(r, CHUNK)], 1)
            ew_bufs[buf][...] = ew
            xs_b = cache_ref[pl.ds(base + r, CHUNK), 0:GROUP_W]
            bm = cache_ref[pl.ds(base + r, CHUNK), GROUP_W:GROUP_W + D_STATE]
            st_bufs[buf][...] = _dot_tn(bm, xs_b * ew[:, GROUP_W:].astype(BF16))

        def consume(c, buf, _):
            r = row_of(c)
            cm = cache_ref[pl.ds(base + r, CHUNK), GROUP_W + D_STATE:]
            eb = ew_bufs[buf][:, :GROUP_W]
            sb = sb_ref[...]
            y = yacc_ref[pl.ds(base + r, CHUNK), :] + _dot(cm, sb.astype(BF16)) * eb
            sb_ref[...] = sb * eb[0:1, :] + st_bufs[buf][...]
            z = z_ref[0, pl.ds(r, CHUNK), :]
            y = y * (z * jax.nn.sigmoid(z)).astype(F32)
            out_ref[0, pl.ds(r, CHUNK), :] = _rmsnorm(y, gain_ref[...]).astype(BF16)

        run_pipelined([prepare], [consume], BWD_WEAVE, lambda k: n_ch - 1 - k, BWD_PER_TRIP, BWD_AHEAD)


def _ssd_selectors():
    onesel = (np.arange(N_DH * CHUNK)[None, :] // CHUNK == np.arange(N_DH)[:, None]).astype(np.float32)
    sel = np.zeros((2, 4 * N_DH, 2 * GROUP_W), np.float32)
    for d in range(2):
        for j in range(HEADS_PER_GROUP):
            dh = d * HEADS_PER_GROUP + j
            for term in range(2):
                sel[d, term * N_DH + dh, j * HEAD_DIM:(j + 1) * HEAD_DIM] = 1.0
                sel[d, (2 + term) * N_DH + dh, GROUP_W + j * HEAD_DIM:GROUP_W + (j + 1) * HEAD_DIM] = 1.0
    return jnp.asarray(onesel, F32), jnp.asarray(sel, BF16)


def _ssd(u, rows, cu, crows, conv_w, conv_b, dskip, gain, tl):
    b, l, _ = u.shape
    ctx_len = cu.shape[1]
    assert ctx_len <= tl
    n_blk = l // tl
    hb = tl // HALO
    last_halo = l // HALO - 1
    assert (tl // CHUNK) % FWD_PER_TRIP == 0 and (tl // CHUNK) % BWD_PER_TRIP == 0
    onesel, sel = _ssd_selectors()
    xg, bg, cg = 0, U_B // D_STATE, U_C // D_STATE

    def in_blk(p, i):
        return i * (1 - p) + (n_blk - 1) * p

    def seq_blk(p, i):
        return i * (1 - p) + (n_blk - 1 - i) * p

    def out_blk(p, i):
        return (n_blk - 1) * (1 - p) + (n_blk - 1 - i) * p

    def prev_halo(p, i):
        return jnp.maximum(in_blk(p, i) * hb - 1, 0)

    def next_halo(p, i):
        return jnp.minimum((in_blk(p, i) + 1) * hb, last_halo)

    def xbc_specs(rows_blk, row_index):
        return [pl.BlockSpec((1, rows_blk, GROUP_W), lambda bb, g, p, i: (bb, row_index(p, i), xg + g)),
                pl.BlockSpec((1, rows_blk, D_STATE), lambda bb, g, p, i: (bb, row_index(p, i), bg + g)),
                pl.BlockSpec((1, rows_blk, D_STATE), lambda bb, g, p, i: (bb, row_index(p, i), cg + g))]

    def param_specs(n_rows):
        return [pl.BlockSpec((n_rows, GROUP_W), lambda bb, g, p, i: (0, xg + g)),
                pl.BlockSpec((n_rows, D_STATE), lambda bb, g, p, i: (0, bg + g)),
                pl.BlockSpec((n_rows, D_STATE), lambda bb, g, p, i: (0, cg + g))]

    grid = (b, N_GROUPS, 2, n_blk)
    in_specs = (
        xbc_specs(tl, in_blk) + xbc_specs(HALO, prev_halo) + xbc_specs(HALO, next_halo)
        + [pl.BlockSpec((1, tl, GROUP_W), lambda bb, g, p, i: (bb, out_blk(p, i), U_Z // GROUP_W + g)),
           pl.BlockSpec((1, ROWS_PER_GROUP, tl), lambda bb, g, p, i: (bb, g, seq_blk(p, i)))]
        + xbc_specs(ctx_len, lambda p, i: 0)[:2]
        + [pl.BlockSpec((1, ROWS_PER_GROUP, ctx_len), lambda bb, g, p, i: (bb, g, 0))]
        + param_specs(SUBLANE) + param_specs(1)
        + [pl.BlockSpec((1, GROUP_W), lambda bb, g, p, i: (0, g)),
           pl.BlockSpec((1, GROUP_W), lambda bb, g, p, i: (0, g)),
           pl.BlockSpec(onesel.shape, lambda bb, g, p, i: (0, 0)),
           pl.BlockSpec(sel.shape, lambda bb, g, p, i: (0, 0, 0))])
    return pl.pallas_call(
        functools.partial(_ssd_kernel, tl=tl, n_blk=n_blk, ctx_len=ctx_len),
        grid=grid,
        in_specs=in_specs,
        out_specs=pl.BlockSpec((1, tl, GROUP_W), lambda bb, g, p, i: (bb, out_blk(p, i), g)),
        out_shape=jax.ShapeDtypeStruct((b, l, D_INNER), BF16),
        scratch_shapes=[pltpu.VMEM((l, N_SLAB * LANE), BF16),
                        pltpu.VMEM((l, GROUP_W), F32),
                        pltpu.VMEM((D_STATE, GROUP_W), F32),
                        pltpu.VMEM((D_STATE, GROUP_W), F32),
                        pltpu.VMEM((N_SLAB, (tl + 2 * HALO) // 2, LANE), jnp.uint32),
                        pltpu.VMEM((N_SLAB, (tl + 2 * HALO) // 2, LANE), jnp.uint32)]
        + [pltpu.VMEM((CHUNK, N_DH * CHUNK), F32)] * N_BUFS
        + [pltpu.VMEM((CHUNK, 2 * GROUP_W), F32)] * N_BUFS
        + [pltpu.VMEM((D_STATE, GROUP_W), F32)] * N_BUFS,
        compiler_params=pltpu.CompilerParams(
            dimension_semantics=("arbitrary", "arbitrary", "arbitrary", "arbitrary"),
            vmem_limit_bytes=VMEM_LIMIT),
        name="ssd",
    )(u, u, u, u, u, u, u, u, u, u, rows, cu, cu, crows,
      conv_w, conv_w, conv_w, conv_b, conv_b, conv_b, dskip, gain, onesel, sel)


def _dft_mats():
    def cs(n):
        k = np.arange(n)
        ang = 2.0 * np.pi * np.outer(k, k) / n
        return np.cos(ang), np.sin(ang)

    c3, s3 = cs(FFT_GROUP_DIM)
    w1 = np.concatenate([c3, -s3], axis=1)
    c2, s2 = cs(GRID_W)
    m2 = np.block([[c2, s2], [-s2, c2]])
    return w1, m2


def _fft12_kernel(f_ref, w1_ref, m2_ref, o_ref, *, tm):
    f = f_ref[0]
    w1 = w1_ref[...]
    ps, qs = [], []
    for g in range(FFT_GROUPS):
        pq = _dot(f[:, g * FFT_GROUP_DIM:(g + 1) * FFT_GROUP_DIM], w1)
        ps.append(pq[:, :FFT_GROUP_DIM])
        qs.append(pq[:, FFT_GROUP_DIM:])
    pr = jnp.concatenate(ps, axis=1).astype(BF16)
    qr = jnp.concatenate(qs, axis=1).astype(BF16)
    m2 = m2_ref[...]
    pad = jnp.zeros((FFT_PITCH - GRID_W, D_MODEL), jnp.uint32)
    for r in range(tm // GRID_W):
        sl = slice(r * GRID_W, (r + 1) * GRID_W)
        o = _dot(m2, jnp.concatenate([pr[sl], qr[sl]], axis=0))
        r0 = r * FFT_PITCH
        o_ref[0, r0:r0 + GRID_W, :] = _pack_complex(o[:GRID_W], o[GRID_W:])
        o_ref[0, r0 + GRID_W:r0 + FFT_PITCH, :] = pad


def _pack_complex(re, im):
    re_bits = pltpu.bitcast(re.astype(BF16).astype(F32), jnp.uint32)
    im_bits = pltpu.bitcast(im.astype(BF16).astype(F32), jnp.uint32)
    return (re_bits >> 16) | (im_bits & jnp.uint32(0xFFFF0000))


def _unpack_complex(word):
    re = pltpu.bitcast(word << 16, F32)
    im = pltpu.bitcast(word & jnp.uint32(0xFFFF0000), F32)
    return re, im


def _fft3_kernel(ri_ref, m3_ref, o_ref, scr_ref, *, rows):
    m3 = m3_ref[...]

    def column_pair(t, carry):
        w = 2 * t
        cols = [jnp.concatenate(_unpack_complex(ri_ref[0, pl.ds(w + dw, rows, stride=FFT_PITCH), :]), axis=0)
                for dw in range(2)]
        o = _dot(m3, jnp.concatenate(cols, axis=1).astype(BF16))
        for dw in range(2):
            scr_ref[pl.ds(w + dw, rows, stride=FFT_PITCH), :] = o[:, dw * LANE:(dw + 1) * LANE]
        return carry

    lax.fori_loop(0, GRID_W // 2, column_pair, 0, unroll=4)
    for k in range(rows):
        o_ref[0, k * GRID_W:(k + 1) * GRID_W, :] = scr_ref[k * FFT_PITCH:k * FFT_PITCH + GRID_W, :].astype(BF16)


def _fourier(u, tm):
    b, l, _ = u.shape
    rows = l // GRID_W
    w1, m2 = _dft_mats()
    ri = pl.pallas_call(
        functools.partial(_fft12_kernel, tm=tm),
        grid=(b, l // tm),
        in_specs=[pl.BlockSpec((1, tm, D_MODEL), lambda i, j: (i, j, U_FFT // D_MODEL)),
                  pl.BlockSpec((FFT_GROUP_DIM, 2 * FFT_GROUP_DIM), lambda i, j: (0, 0)),
                  pl.BlockSpec((2 * GRID_W, 2 * GRID_W), lambda i, j: (0, 0))],
        out_specs=pl.BlockSpec((1, tm // GRID_W * FFT_PITCH, D_MODEL), lambda i, j: (i, j, 0)),
        out_shape=jax.ShapeDtypeStruct((b, rows * FFT_PITCH, D_MODEL), jnp.uint32),
        compiler_params=pltpu.CompilerParams(dimension_semantics=("arbitrary", "arbitrary"),
                                             vmem_limit_bytes=VMEM_LIMIT),
        name="fft12",
    )(u, jnp.asarray(w1, BF16), jnp.asarray(m2, BF16))

    k = np.arange(rows)
    ang = 2.0 * np.pi * np.outer(k, k) / rows
    scale = 1.0 / np.sqrt(float(rows * GRID_W * FFT_GROUP_DIM))
    m3 = np.concatenate([np.cos(ang), np.sin(ang)], axis=1) * scale
    return pl.pallas_call(
        functools.partial(_fft3_kernel, rows=rows),
        grid=(b, D_MODEL // LANE),
        in_specs=[pl.BlockSpec((1, rows * FFT_PITCH, LANE), lambda i, j: (i, 0, j)),
                  pl.BlockSpec((rows, 2 * rows), lambda i, j: (0, 0))],
        out_specs=pl.BlockSpec((1, l, LANE), lambda i, j: (i, 0, j)),
        out_shape=jax.ShapeDtypeStruct((b, l, D_MODEL), BF16),
        scratch_shapes=[pltpu.VMEM((rows * FFT_PITCH, LANE), F32)],
        compiler_params=pltpu.CompilerParams(dimension_semantics=("arbitrary", "arbitrary"),
                                             vmem_limit_bytes=VMEM_LIMIT),
        name="fft3",
    )(ri, jnp.asarray(m3, BF16))


FF_SPLIT = 2


def _tail_kernel(y_ref, f_ref, ga_ref, gb_ref, x_ref, g1_ref, sh2_ref, sc2_ref, g2_ref,
                 wso_ref, wfo_ref, wo_ref, n2_ref, wff_ref, wfd_ref, fg_ref, o_ref):
    a = _dot(y_ref[...], wso_ref[...])
    bq = _dot(f_ref[...], wfo_ref[...])
    merged = jax.nn.sigmoid(ga_ref[...].astype(F32)) * a + jax.nn.sigmoid(gb_ref[...].astype(F32)) * bq
    xn = x_ref[...] + g1_ref[0] * _dot(merged.astype(BF16), wo_ref[...])
    h = (_rmsnorm(xn, n2_ref[...]) * (1.0 + sc2_ref[0]) + sh2_ref[0]).astype(BF16)
    ffw = D_FF // FF_SPLIT
    acc = None
    for s in range(FF_SPLIT):
        gate = _dot(h, wff_ref[:, s * ffw:(s + 1) * ffw])
        up = _dot(h, wff_ref[:, D_FF + s * ffw:D_FF + (s + 1) * ffw])
        act = (gate * jax.nn.sigmoid(gate) * up).astype(BF16)
        part = _dot(act, wfd_ref[s * ffw:(s + 1) * ffw, :])
        acc = part if acc is None else acc + part
    o_ref[...] = _rmsnorm(xn + g2_ref[0] * acc, fg_ref[...])


def _tail(y2d, f2d, u2d, x2d, g1, sh2, sc2, g2, wso, wfo, wo, n2, wff, wfd, fg, tm, rows_per_mod):
    m = x2d.shape[0]
    bpm = rows_per_mod // tm

    def const(shape):
        return pl.BlockSpec(shape, lambda i: (0, 0), pipeline_mode=pl.Buffered(1))

    def mod():
        return pl.BlockSpec((1, 1, D_MODEL), lambda i: (i // bpm, 0, 0))

    return pl.pallas_call(
        _tail_kernel,
        grid=(m // tm,),
        in_specs=[pl.BlockSpec((tm, D_INNER), lambda i: (i, 0)),
                  pl.BlockSpec((tm, D_MODEL), lambda i: (i, 0)),
                  pl.BlockSpec((tm, D_MODEL), lambda i: (i, U_GATE // D_MODEL)),
                  pl.BlockSpec((tm, D_MODEL), lambda i: (i, U_GATE // D_MODEL + 1)),
                  pl.BlockSpec((tm, D_MODEL), lambda i: (i, 0)),
                  mod(), mod(), mod(), mod(),
                  const((D_INNER, D_MODEL)), const((D_MODEL, D_MODEL)), const((D_MODEL, D_MODEL)),
                  const((1, D_MODEL)),
                  const((D_MODEL, 2 * D_FF)), const((D_FF, D_MODEL)),
                  const((1, D_MODEL))],
        out_specs=pl.BlockSpec((tm, D_MODEL), lambda i: (i, 0)),
        out_shape=jax.ShapeDtypeStruct((m, D_MODEL), F32),
        compiler_params=pltpu.CompilerParams(dimension_semantics=("arbitrary",),
                                             vmem_limit_bytes=VMEM_LIMIT),
        name="tail",
    )(y2d, f2d, u2d, u2d, x2d, g1, sh2, sc2, g2, wso, wfo, wo, n2, wff, wfd, fg)


def _dt_lanes(a):
    lead = a.shape[:-1]
    a = a.reshape(lead + (2, N_GROUPS, HEADS_PER_GROUP))
    a = jnp.swapaxes(a, -3, -2).reshape(lead + (2 * N_HEADS,))
    return jnp.concatenate([a, jnp.zeros(lead + (LANE - 2 * N_HEADS,), a.dtype)], axis=-1)


def _block(x, c, ctx, c_ctx, w_ada, b_ada, norm1_g, w_in, conv_w, conv_b, dt_bias, a_log, d_skip,
           ssd_norm_g, w_ssd_out, w_fft_out, w_o, norm2_g, w_ffn_in, w_ffn_out, final_g,
           proj_tm, prep_tl, ssd_tl, fft_tm, tail_tm):
    b, l, d = x.shape
    ctx_len = ctx.shape[1]
    assert d == D_MODEL and l % (GRID_W * 2) == 0 and ctx_len % CHUNK == 0 and b + 1 <= SUBLANE

    w_in_t = w_in[0].T
    w_main_t = w_in_t.astype(BF16)
    w_dt = _dt_lanes(w_in_t[REF_DT:REF_Z].T).T
    w_dt_hi = w_dt.astype(BF16)
    w_dt_lo = (w_dt - w_dt_hi.astype(F32)).astype(BF16)
    w_dt2 = jnp.concatenate([w_dt_hi, w_dt_lo], axis=0)
    conv_w_p = jnp.concatenate([conv_w[0], jnp.zeros((SUBLANE - D_CONV, CONV_DIM), F32)], axis=0)
    conv_b_p = conv_b[0].reshape(1, CONV_DIM)
    bias_v = _dt_lanes(dt_bias[0].reshape(-1)).reshape(1, LANE)
    alog_v = _dt_lanes(a_log[0].reshape(-1)).reshape(1, LANE)
    dskip_v = jnp.repeat(d_skip[0], HEAD_DIM).reshape(1, D_INNER)
    gain_v = ssd_norm_g[0].reshape(1, D_INNER)
    wso = w_ssd_out[0].astype(BF16)
    wfo = w_fft_out[0].astype(BF16)
    wo = w_o[0].astype(BF16)
    wff = w_ffn_in[0].astype(BF16)
    wfd = w_ffn_out[0].astype(BF16)

    cvecs = jnp.concatenate([c, c_ctx[None, :], jnp.zeros((SUBLANE - b - 1, D_MODEL), F32)], axis=0)
    mods = _mods(cvecs, b + 1, w_ada[0], b_ada[0])
    lat = mods[:b].reshape(b, 6, 1, D_MODEL)
    sh1, sc1, g1, sh2, sc2, g2 = (lat[:, k] for k in range(6))
    cm = mods[b].reshape(6, 1, 1, D_MODEL)

    x2d = x.reshape(b * l, D_MODEL)
    n1 = norm1_g[0].reshape(1, D_MODEL)
    u2d, dt_raw = _inproj(x2d, sh1, sc1, n1, w_main_t, w_dt2, U_WIDTH, proj_tm, l)
    u = u2d.reshape(b, l, U_WIDTH)
    cu2d, cdt_raw = _inproj(ctx.reshape(b * ctx_len, D_MODEL), cm[0], cm[1], n1, w_main_t, w_dt2,
                            CONV_DIM, b * ctx_len, b * ctx_len)
    cu = cu2d.reshape(b, ctx_len, CONV_DIM)

    rows = _dtprep(dt_raw.reshape(b, l, LANE), bias_v, alog_v, prep_tl)
    crows = _dtprep(cdt_raw.reshape(b, ctx_len, LANE), bias_v, alog_v, ctx_len)
    y_ssd = _ssd(u, rows, cu, crows, conv_w_p, conv_b_p, dskip_v, gain_v, ssd_tl)

    f_mix = _fourier(u, fft_tm)

    out = _tail(y_ssd.reshape(b * l, D_INNER), f_mix.reshape(b * l, D_MODEL), u2d, x2d,
                g1, sh2, sc2, g2, wso, wfo, wo, norm2_g[0].reshape(1, D_MODEL), wff, wfd,
                final_g.reshape(1, D_MODEL), tail_tm, l)
    return out.reshape(b, l, D_MODEL)


def kernel(x, c, ctx, c_ctx, w_ada, b_ada, norm1_g, w_in, conv_w, conv_b, dt_bias, a_log, d_skip,
           ssd_norm_g, w_ssd_out, w_fft_out, w_o, norm2_g, w_ffn_in, w_ffn_out, final_g):
    return _block(x, c, ctx, c_ctx, w_ada, b_ada, norm1_g, w_in, conv_w, conv_b, dt_bias, a_log, d_skip,
                  ssd_norm_g, w_ssd_out, w_fft_out, w_o, norm2_g, w_ffn_in, w_ffn_out, final_g,
                  proj_tm=2048, prep_tl=2048, ssd_tl=4096, fft_tm=1024, tail_tm=512)
```

```python
import functools
import math

import jax
import jax.numpy as jnp
import numpy as np
from jax import lax
from jax.experimental import pallas as pl
from jax.experimental.pallas import tpu as pltpu

F32 = jnp.float32
BF16 = jnp.bfloat16

D_MODEL = 1024
EPS = 1e-6
GRID_W = 64

N_GROUPS = 8
HEADS_PER_GROUP = 4
HEAD_DIM = 64
N_HEADS = N_GROUPS * HEADS_PER_GROUP
D_STATE = 128
D_INNER = N_HEADS * HEAD_DIM
GROUP_W = HEADS_PER_GROUP * HEAD_DIM
GS = N_GROUPS * D_STATE
CONV_DIM = D_INNER + 2 * GS
D_CONV = 5
CHUNK = 128

FFT_GROUPS = 8
FFT_GROUP_DIM = 128
D_FF = 2816

REF_DT = CONV_DIM
REF_Z = REF_DT + 2 * N_HEADS

U_B = D_INNER
U_C = U_B + GS
U_Z = CONV_DIM
U_FFT = U_Z + D_INNER
U_GATE = U_FFT + D_MODEL
U_WIDTH = U_GATE + 2 * D_MODEL

LANE = 128
SUBLANE = 8
HALO = 16
VMEM_LIMIT = 56 * 1024 * 1024

N_DH = 2 * HEADS_PER_GROUP
R_CUM, R_NCX, R_EY, R_WX, R_DIAG = 0, 8, 16, 24, 32
ROWS_PER_GROUP = 40
N_SLAB = (GROUP_W + 2 * D_STATE) // LANE
DT_FLOOR = 1e-37
LOG2E = math.log2(math.e)
N_CUM_TERMS = 3
FWD_PER_TRIP = 4
BWD_PER_TRIP = 8
N_BUFS = 4
FWD_AHEAD = (1, 1)
BWD_AHEAD = (3,)
FWD_WEAVE = "C0 C1 P0 P1"
BWD_WEAVE = "P0 C0"
K_ARGS = 2 * N_CUM_TERMS * N_DH
FFT_PITCH = GRID_W + 4


def _dot(a, b):
    return jnp.dot(a, b, preferred_element_type=F32)


def _dot_tn(a, b):
    return lax.dot_general(a, b, (((0,), (0,)), ((), ())), preferred_element_type=F32)


def _dot_nt(a, b):
    return lax.dot_general(a, b, (((1,), (1,)), ((), ())), preferred_element_type=F32)


def _rmsnorm(x, g):
    return x * lax.rsqrt(jnp.mean(x * x, axis=-1, keepdims=True) + EPS) * g


def _mods_kernel(ct_ref, w_ref, b_ref, o_ref, *, n_vecs):
    s = ct_ref[...]
    s = s * jax.nn.sigmoid(s)
    w = w_ref[...]
    rows = [jnp.sum(w * s[:, r:r + 1], axis=0, keepdims=True) for r in range(n_vecs)]
    rows.append(jnp.zeros((SUBLANE - n_vecs, w.shape[1]), F32))
    o_ref[...] = jnp.concatenate(rows, axis=0) + b_ref[...]


def _mods(cvecs, n_vecs, w_ada, b_ada):
    n = w_ada.shape[1]
    tn = 1536
    return pl.pallas_call(
        functools.partial(_mods_kernel, n_vecs=n_vecs),
        grid=(n // tn,),
        in_specs=[pl.BlockSpec((D_MODEL, SUBLANE), lambda j: (0, 0)),
                  pl.BlockSpec((D_MODEL, tn), lambda j: (0, j)),
                  pl.BlockSpec((1, tn), lambda j: (0, j))],
        out_specs=pl.BlockSpec((SUBLANE, tn), lambda j: (0, j)),
        out_shape=jax.ShapeDtypeStruct((SUBLANE, n), F32),
        compiler_params=pltpu.CompilerParams(dimension_semantics=("arbitrary",),
                                             vmem_limit_bytes=VMEM_LIMIT),
        name="mods",
    )(cvecs.T, w_ada, b_ada.reshape(1, n))


def _inproj_kernel(x_ref, sh_ref, sc_ref, g_ref, w_ref, wdt_ref, u_ref, dt_ref, h_ref):
    @pl.when(pl.program_id(1) == 0)
    def _():
        h = _rmsnorm(x_ref[...], g_ref[...]) * (1.0 + sc_ref[0]) + sh_ref[0]
        hb = h.astype(BF16)
        h_ref[...] = hb
        hl = (h - hb.astype(F32)).astype(BF16)
        a = _dot_nt(hb, wdt_ref[...])
        dt_ref[...] = a[:, :LANE] + a[:, LANE:] + _dot_nt(hl, wdt_ref[:LANE, :])

    u_ref[...] = _dot_nt(h_ref[...], w_ref[...]).astype(BF16)


def _inproj(x2d, shift, scale, gain, w_t, w_dt_t, n_cols, tm, rows_per_mod):
    m = x2d.shape[0]
    tn = 1024
    bpm = rows_per_mod // tm

    def w_rows(i, j):
        return pl.multiple_of(j * tn + jnp.where(j * tn >= REF_DT, REF_Z - REF_DT, 0), 2 * HALO), 0

    return pl.pallas_call(
        _inproj_kernel,
        grid=(m // tm, n_cols // tn),
        in_specs=[pl.BlockSpec((tm, D_MODEL), lambda i, j: (i, 0)),
                  pl.BlockSpec((1, 1, D_MODEL), lambda i, j: (i // bpm, 0, 0)),
                  pl.BlockSpec((1, 1, D_MODEL), lambda i, j: (i // bpm, 0, 0)),
                  pl.BlockSpec((1, D_MODEL), lambda i, j: (0, 0)),
                  pl.BlockSpec((pl.Element(tn), pl.Element(D_MODEL)), w_rows),
                  pl.BlockSpec((2 * LANE, D_MODEL), lambda i, j: (0, 0))],
        out_specs=[pl.BlockSpec((tm, tn), lambda i, j: (i, j)),
                   pl.BlockSpec((tm, LANE), lambda i, j: (i, 0))],
        out_shape=[jax.ShapeDtypeStruct((m, n_cols), BF16),
                   jax.ShapeDtypeStruct((m, LANE), F32)],
        scratch_shapes=[pltpu.VMEM((tm, D_MODEL), BF16)],
        compiler_params=pltpu.CompilerParams(dimension_semantics=("arbitrary", "arbitrary"),
                                             vmem_limit_bytes=VMEM_LIMIT),
        name="inproj",
    )(x2d, shift, scale, gain, w_t, w_dt_t)


def _dtprep_kernel(raw_ref, bias_ref, alog_ref, row_ref, *, n_chunks):
    a = -jnp.exp(alog_ref[...])
    bias = bias_ref[...]
    ii = lax.broadcasted_iota(jnp.int32, (CHUNK, CHUNK), 0)
    jj = lax.broadcasted_iota(jnp.int32, (CHUNK, CHUNK), 1)
    tril = (jj <= ii).astype(BF16)
    triu = (jj >= ii).astype(BF16)
    lane = lax.broadcasted_iota(jnp.int32, (1, LANE), 1)
    is_fwd = (lane % N_DH) < HEADS_PER_GROUP

    def tri_sum(tri, terms):
        s = _dot(tri, terms)
        return s[:, :LANE] + s[:, LANE:2 * LANE] + s[:, 2 * LANE:]

    for c in range(n_chunks):
        sl = slice(c * CHUNK, (c + 1) * CHUNK)
        dt = jax.nn.softplus(raw_ref[0, sl, :] + bias)
        dta = dt * a
        terms = jnp.concatenate(_split_terms(dta, N_CUM_TERMS), axis=1).astype(BF16)
        cf = tri_sum(tril, terms)
        cb = tri_sum(triu, terms)
        cum = jnp.where(is_fwd, cf, cb)
        tot = jnp.where(is_fwd, cf[CHUNK - 1:CHUNK, :], cb[0:1, :])
        wx = dt * jnp.exp(tot - cum)
        ey = jnp.exp(cum)
        ncx = jnp.log(jnp.maximum(dt, DT_FLOOR)) - cum
        diag = jnp.log(dt + pltpu.roll(dt, LANE - HEADS_PER_GROUP, 1))
        for r_off, val in ((R_CUM, cum * LOG2E), (R_NCX, ncx * LOG2E), (R_EY, ey), (R_WX, wx),
                           (R_DIAG, diag * LOG2E)):
            vt = val.T
            for g in range(N_GROUPS):
                r0 = g * ROWS_PER_GROUP + r_off
                row_ref[0, r0:r0 + N_DH, sl] = vt[g * N_DH:(g + 1) * N_DH, :]


def _dtprep(raw, bias, alog, tl):
    b, l, _ = raw.shape
    return pl.pallas_call(
        functools.partial(_dtprep_kernel, n_chunks=tl // CHUNK),
        grid=(b, l // tl),
        in_specs=[pl.BlockSpec((1, tl, LANE), lambda i, j: (i, j, 0)),
                  pl.BlockSpec((1, LANE), lambda i, j: (0, 0)),
                  pl.BlockSpec((1, LANE), lambda i, j: (0, 0))],
        out_specs=pl.BlockSpec((1, N_GROUPS * ROWS_PER_GROUP, tl), lambda i, j: (i, 0, j)),
        out_shape=jax.ShapeDtypeStruct((b, N_GROUPS * ROWS_PER_GROUP, l), F32),
        compiler_params=pltpu.CompilerParams(dimension_semantics=("arbitrary", "arbitrary"),
                                             vmem_limit_bytes=VMEM_LIMIT),
        name="dtprep",
    )(raw, bias, alog)


def _split_terms(x, n):
    terms = []
    for _ in range(n - 1):
        t = x.astype(BF16).astype(F32)
        terms.append(t)
        x = x - t
    terms.append(x)
    return terms


def _scale_rows(rows):
    return jnp.concatenate(_split_terms(rows[R_EY:R_EY + N_DH], 2) + _split_terms(rows[R_WX:R_WX + N_DH], 2),
                           axis=0).astype(BF16)


def _conv_silu(ext_refs, s, p0, cw, cbias):
    sl = slice(s * LANE, (s + 1) * LANE)
    acc = cbias[:, sl].astype(BF16)
    for k in range(D_CONV):
        pair = p0 + (HALO + k - D_CONV // 2 - k % 2) // 2
        tap = pltpu.bitcast(ext_refs[k % 2][s, pl.ds(pair, CHUNK // 2), :], BF16)
        acc = acc + cw[k:k + 1, sl].astype(BF16) * tap
    return acc * jax.nn.sigmoid(acc)


def _ssd_kernel(x_ref, b_ref, c_ref, xp_ref, bp_ref, cp_ref, xn_ref, bn_ref, cn_ref, z_ref, row_ref,
                cx_ref, cb_ref, crow_ref, cwx_ref, cwb_ref, cwc_ref, cbx_ref, cbb_ref, cbc_ref,
                dskip_ref, gain_ref, onesel_ref, sel_ref, out_ref,
                cache_ref, yacc_ref, sf_ref, sb_ref, ext0_ref, ext1_ref, *buf_refs, tl, n_blk, ctx_len):
    p = pl.program_id(2)
    i = pl.program_id(3)
    n_ch = tl // CHUNK
    args_bufs = buf_refs[0:N_BUFS]
    ew_bufs = buf_refs[N_BUFS:2 * N_BUFS]
    st_bufs = buf_refs[2 * N_BUFS:3 * N_BUFS]
    ext_refs = (ext0_ref, ext1_ref)
    cw = jnp.concatenate([cwx_ref[...], cwb_ref[...], cwc_ref[...]], axis=1)
    cbias = jnp.concatenate([cbx_ref[...], cbb_ref[...], cbc_ref[...]], axis=1)

    def slabs_of(x, bm, cm):
        return [x[:, :LANE], x[:, LANE:], bm, cm]

    def slab_refs(xr, br, cr):
        return [(xr, 0), (xr, LANE), (br, 0), (cr, 0)]

    def fill_ext(src, first, last, n_rows):
        half = HALO // 2
        n_piece = n_rows // CHUNK

        def copy_piece(k, carry):
            r = pl.multiple_of(k * CHUNK, CHUNK)
            q = pl.multiple_of(k * (CHUNK // 2), CHUNK // 2)
            for s, (ref, lane0) in enumerate(src):
                ext0_ref[s, pl.ds(half + q, CHUNK // 2), :] = pltpu.bitcast(
                    ref[0, pl.ds(r, CHUNK), lane0:lane0 + LANE], jnp.uint32)
            return carry

        def shift_piece(k, carry):
            q = pl.multiple_of(k * (CHUNK // 2), CHUNK // 2)
            for s in range(len(src)):
                lo = ext0_ref[s, pl.ds(q, CHUNK // 2), :]
                hi = ext0_ref[s, pl.ds(q + 1, CHUNK // 2), :]
                ext1_ref[s, pl.ds(q, CHUNK // 2), :] = (lo >> 16) | (hi << 16)
            return carry

        for s in range(len(src)):
            ext0_ref[s, 0:half, :] = pltpu.bitcast(first[s], jnp.uint32)
            ext0_ref[s, half + n_rows // 2:2 * half + n_rows // 2, :] = pltpu.bitcast(last[s], jnp.uint32)
        lax.fori_loop(0, n_piece, copy_piece, 0, unroll=2)
        lax.fori_loop(0, n_piece, shift_piece, 0, unroll=2)
        for s in range(len(src)):
            q = n_rows // 2
            ext1_ref[s, q:q + half, :] = (ext0_ref[s, q:q + half, :] >> 16) | (ext0_ref[s, q + 1:q + half + 1, :] << 16)

    def row_of(c):
        return c * CHUNK if isinstance(c, int) else pl.multiple_of(c * CHUNK, CHUNK)

    def expand(rowt, d):
        return _dot_tn(_scale_rows(rowt), sel_ref[d])

    def run_pipelined(prepare_parts, consume_parts, weave, order, per_trip, aheads):
        def step(k, d, in_loop):
            handed = None
            for item in weave.split():
                n = int(item[1:])
                if item[0] == "P":
                    if in_loop or k + aheads[n] < n_ch:
                        prepare_parts[n](order(k + aheads[n]), (d + aheads[n]) % N_BUFS)
                else:
                    handed = consume_parts[n](order(k), d % N_BUFS, handed)

        for n, part in enumerate(prepare_parts):
            for a in range(aheads[n]):
                part(order(a), a)

        def trip(t, carry):
            for d in range(per_trip):
                step(per_trip * t + d, d, True)
            return carry

        lax.fori_loop(0, n_ch // per_trip - 1, trip, 0)
        for d in range(per_trip):
            step(n_ch - per_trip + d, d, False)

    @pl.when(jnp.logical_and(p == 0, i == 0))
    def _init_from_context():
        zpad = [jnp.zeros((HALO, LANE), BF16)] * (N_SLAB - 1)
        fill_ext(slab_refs(cx_ref, cb_ref, None)[:N_SLAB - 1], zpad, zpad, ctx_len)
        n_cc = ctx_len // CHUNK
        sf = jnp.zeros((D_STATE, GROUP_W), F32)
        st_b, ey_b = [], []
        for cc in range(n_cc):
            slabs = [_conv_silu(ext_refs, s, cc * CHUNK // 2, cw, cbias) for s in range(N_SLAB - 1)]
            xs = jnp.concatenate(slabs[:2], axis=1).astype(F32)
            bm = slabs[2]
            rowt = crow_ref[0, :, cc * CHUNK:(cc + 1) * CHUNK]
            ewf = expand(rowt, 0)
            ewb = expand(rowt, 1)
            sf = sf * ewf[CHUNK - 1:CHUNK, :GROUP_W] + _dot_tn(bm, (xs * ewf[:, GROUP_W:]).astype(BF16))
            st_b.append(_dot_tn(bm, (xs * ewb[:, GROUP_W:]).astype(BF16)))
            ey_b.append(ewb[0:1, :GROUP_W])
        sb = jnp.zeros((D_STATE, GROUP_W), F32)
        for cc in reversed(range(n_cc)):
            sb = sb * ey_b[cc] + st_b[cc]
        sf_ref[...] = sf
        sb_ref[...] = sb

    @pl.when(p == 0)
    def _forward():
        blk = i
        base = blk * tl
        has_prev = blk > 0
        has_next = blk < n_blk - 1
        fill_ext(slab_refs(x_ref, b_ref, c_ref),
                 [jnp.where(has_prev, v, jnp.zeros_like(v)) for v in slabs_of(xp_ref[0], bp_ref[0], cp_ref[0])],
                 [jnp.where(has_next, v, jnp.zeros_like(v)) for v in slabs_of(xn_ref[0], bn_ref[0], cn_ref[0])],
                 tl)
        ii = lax.broadcasted_iota(jnp.int32, (CHUNK, CHUNK), 0)
        jj = lax.broadcasted_iota(jnp.int32, (CHUNK, CHUNK), 1)
        lower = jj < ii
        upper = jj > ii

        def prepare_conv(c, buf):
            r = row_of(c)
            p0 = c * (CHUNK // 2) if isinstance(c, int) else pl.multiple_of(c * (CHUNK // 2), CHUNK // 2)
            slabs = [_conv_silu(ext_refs, s, p0, cw, cbias) for s in range(N_SLAB)]
            for s in range(N_SLAB):
                cache_ref[pl.ds(base + r, CHUNK), s * LANE:(s + 1) * LANE] = slabs[s]
            yacc_ref[pl.ds(base + r, CHUNK), :] = jnp.concatenate(slabs[:2], axis=1).astype(F32) * dskip_ref[...]

        def prepare_scales(c, buf):
            rowt = row_ref[0, :, pl.ds(row_of(c), CHUNK)]
            onesel = onesel_ref[...]
            ones = jnp.ones((N_DH, CHUNK), F32)
            at = jnp.concatenate(_split_terms(rowt[R_CUM:R_CUM + N_DH], N_CUM_TERMS) + [ones] * N_CUM_TERMS,
                                 axis=0).astype(BF16)
            wide = [jnp.concatenate([t] * N_DH, axis=1) * onesel
                    for t in _split_terms(rowt[R_NCX:R_NCX + N_DH], N_CUM_TERMS)]
            bmat = jnp.concatenate([onesel] * N_CUM_TERMS + wide, axis=0).astype(BF16)
            args_bufs[buf][...] = _dot_tn(at, bmat)
            ew_bufs[buf][...] = expand(rowt, 0)

        def consume_state(c, buf, _):
            r = row_of(c)
            xbc_b = cache_ref[pl.ds(base + r, CHUNK), :]
            xs_b = xbc_b[:, :GROUP_W]
            bm = xbc_b[:, GROUP_W:GROUP_W + D_STATE]
            cm = xbc_b[:, GROUP_W + D_STATE:]
            return r, xs_b, bm, cm, _dot_nt(cm, bm)

        def consume_decay(c, buf, handed):
            r, xs_b, bm, cm, cbm = handed
            yd = []
            for j in range(HEADS_PER_GROUP):
                argf = args_bufs[buf][:, j * CHUNK:(j + 1) * CHUNK]
                argb = args_bufs[buf][:, (HEADS_PER_GROUP + j) * CHUNK:(HEADS_PER_GROUP + j + 1) * CHUNK]
                diag = row_ref[0, R_DIAG + j:R_DIAG + j + 1, pl.ds(r, CHUNK)]
                arg = jnp.where(lower, argf, jnp.where(upper, argb, diag))
                wmat = (cbm * jnp.exp2(arg)).astype(BF16)
                yd.append(_dot(wmat, xs_b[:, j * HEAD_DIM:(j + 1) * HEAD_DIM]))
            ef = ew_bufs[buf][:, :GROUP_W]
            wxf = ew_bufs[buf][:, GROUP_W:]
            sf = sf_ref[...]
            yacc_ref[pl.ds(base + r, CHUNK), :] += jnp.concatenate(yd, axis=1) + _dot(cm, sf.astype(BF16)) * ef
            sf_ref[...] = sf * ef[CHUNK - 1:CHUNK, :] + _dot_tn(bm, xs_b * wxf.astype(BF16))

        run_pipelined([prepare_conv, prepare_scales], [consume_state, consume_decay], FWD_WEAVE,
                      lambda k: k, FWD_PER_TRIP, FWD_AHEAD)

    @pl.when(p == 1)
    def _backward():
        blk = n_blk - 1 - i
        base = blk * tl

        def prepare(c, buf):
            r = row_of(c)
            ew = expand(row_ref[0, :, pl.ds(r, CHUNK)], 1)
            ew_bufs[buf][...] = ew
            xs_b = cache_ref[pl.ds(base + r, CHUNK), 0:GROUP_W]
            bm = cache_ref[pl.ds(base + r, CHUNK), GROUP_W:GROUP_W + D_STATE]
            st_bufs[buf][...] = _dot_tn(bm, xs_b * ew[:, GROUP_W:].astype(BF16))

        def consume(c, buf, _):
            r = row_of(c)
            cm = cache_ref[pl.ds(base + r, CHUNK), GROUP_W + D_STATE:]
            eb = ew_bufs[buf][:, :GROUP_W]
            sb = sb_ref[...]
            y = yacc_ref[pl.ds(base + r, CHUNK), :] + _dot(cm, sb.astype(BF16)) * eb
            sb_ref[...] = sb * eb[0:1, :] + st_bufs[buf][...]
            z = z_ref[0, pl.ds(r, CHUNK), :]
            y = y * (z * jax.nn.sigmoid(z)).astype(F32)
            out_ref[0, pl.ds(r, CHUNK), :] = _rmsnorm(y, gain_ref[...]).astype(BF16)

        run_pipelined([prepare], [consume], BWD_WEAVE, lambda k: n_ch - 1 - k, BWD_PER_TRIP, BWD_AHEAD)


def _ssd_selectors():
    onesel = (np.arange(N_DH * CHUNK)[None, :] // CHUNK == np.arange(N_DH)[:, None]).astype(np.float32)
    sel = np.zeros((2, 4 * N_DH, 2 * GROUP_W), np.float32)
    for d in range(2):
        for j in range(HEADS_PER_GROUP):
            dh = d * HEADS_PER_GROUP + j
            for term in range(2):
                sel[d, term * N_DH + dh, j * HEAD_DIM:(j + 1) * HEAD_DIM] = 1.0
                sel[d, (2 + term) * N_DH + dh, GROUP_W + j * HEAD_DIM:GROUP_W + (j + 1) * HEAD_DIM] = 1.0
    return jnp.asarray(onesel, F32), jnp.asarray(sel, BF16)


def _ssd(u, rows, cu, crows, conv_w, conv_b, dskip, gain, tl):
    b, l, _ = u.shape
    ctx_len = cu.shape[1]
    assert ctx_len <= tl
    n_blk = l // tl
    hb = tl // HALO
    last_halo = l // HALO - 1
    assert (tl // CHUNK) % FWD_PER_TRIP == 0 and (tl // CHUNK) % BWD_PER_TRIP == 0
    onesel, sel = _ssd_selectors()
    xg, bg, cg = 0, U_B // D_STATE, U_C // D_STATE

    def in_blk(p, i):
        return i * (1 - p) + (n_blk - 1) * p

    def seq_blk(p, i):
        return i * (1 - p) + (n_blk - 1 - i) * p

    def out_blk(p, i):
        return (n_blk - 1) * (1 - p) + (n_blk - 1 - i) * p

    def prev_halo(p, i):
        return jnp.maximum(in_blk(p, i) * hb - 1, 0)

    def next_halo(p, i):
        return jnp.minimum((in_blk(p, i) + 1) * hb, last_halo)

    def xbc_specs(rows_blk, row_index):
        return [pl.BlockSpec((1, rows_blk, GROUP_W), lambda bb, g, p, i: (bb, row_index(p, i), xg + g)),
                pl.BlockSpec((1, rows_blk, D_STATE), lambda bb, g, p, i: (bb, row_index(p, i), bg + g)),
                pl.BlockSpec((1, rows_blk, D_STATE), lambda bb, g, p, i: (bb, row_index(p, i), cg + g))]

    def param_specs(n_rows):
        return [pl.BlockSpec((n_rows, GROUP_W), lambda bb, g, p, i: (0, xg + g)),
                pl.BlockSpec((n_rows, D_STATE), lambda bb, g, p, i: (0, bg + g)),
                pl.BlockSpec((n_rows, D_STATE), lambda bb, g, p, i: (0, cg + g))]

    grid = (b, N_GROUPS, 2, n_blk)
    in_specs = (
        xbc_specs(tl, in_blk) + xbc_specs(HALO, prev_halo) + xbc_specs(HALO, next_halo)
        + [pl.BlockSpec((1, tl, GROUP_W), lambda bb, g, p, i: (bb, out_blk(p, i), U_Z // GROUP_W + g)),
           pl.BlockSpec((1, ROWS_PER_GROUP, tl), lambda bb, g, p, i: (bb, g, seq_blk(p, i)))]
        + xbc_specs(ctx_len, lambda p, i: 0)[:2]
        + [pl.BlockSpec((1, ROWS_PER_GROUP, ctx_len), lambda bb, g, p, i: (bb, g, 0))]
        + param_specs(SUBLANE) + param_specs(1)
        + [pl.BlockSpec((1, GROUP_W), lambda bb, g, p, i: (0, g)),
           pl.BlockSpec((1, GROUP_W), lambda bb, g, p, i: (0, g)),
           pl.BlockSpec(onesel.shape, lambda bb, g, p, i: (0, 0)),
           pl.BlockSpec(sel.shape, lambda bb, g, p, i: (0, 0, 0))])
    return pl.pallas_call(
        functools.partial(_ssd_kernel, tl=tl, n_blk=n_blk, ctx_len=ctx_len),
        grid=grid,
        in_specs=in_specs,
        out_specs=pl.BlockSpec((1, tl, GROUP_W), lambda bb, g, p, i: (bb, out_blk(p, i), g)),
        out_shape=jax.ShapeDtypeStruct((b, l, D_INNER), BF16),
        scratch_shapes=[pltpu.VMEM((l, N_SLAB * LANE), BF16),
                        pltpu.VMEM((l, GROUP_W), F32),
                        pltpu.VMEM((D_STATE, GROUP_W), F32),
                        pltpu.VMEM((D_STATE, GROUP_W), F32),
                        pltpu.VMEM((N_SLAB, (tl + 2 * HALO) // 2, LANE), jnp.uint32),
                        pltpu.VMEM((N_SLAB, (tl + 2 * HALO) // 2, LANE), jnp.uint32)]
        + [pltpu.VMEM((CHUNK, N_DH * CHUNK), F32)] * N_BUFS
        + [pltpu.VMEM((CHUNK, 2 * GROUP_W), F32)] * N_BUFS
        + [pltpu.VMEM((D_STATE, GROUP_W), F32)] * N_BUFS,
        compiler_params=pltpu.CompilerParams(
            dimension_semantics=("arbitrary", "arbitrary", "arbitrary", "arbitrary"),
            vmem_limit_bytes=VMEM_LIMIT),
        name="ssd",
    )(u, u, u, u, u, u, u, u, u, u, rows, cu, cu, crows,
      conv_w, conv_w, conv_w, conv_b, conv_b, conv_b, dskip, gain, onesel, sel)


def _dft_mats():
    def cs(n):
        k = np.arange(n)
        ang = 2.0 * np.pi * np.outer(k, k) / n
        return np.cos(ang), np.sin(ang)

    c3, s3 = cs(FFT_GROUP_DIM)
    w1 = np.concatenate([c3, -s3], axis=1)
    c2, s2 = cs(GRID_W)
    m2 = np.block([[c2, s2], [-s2, c2]])
    return w1, m2


def _fft12_kernel(f_ref, w1_ref, m2_ref, o_ref, *, tm):
    f = f_ref[0]
    w1 = w1_ref[...]
    ps, qs = [], []
    for g in range(FFT_GROUPS):
        pq = _dot(f[:, g * FFT_GROUP_DIM:(g + 1) * FFT_GROUP_DIM], w1)
        ps.append(pq[:, :FFT_GROUP_DIM])
        qs.append(pq[:, FFT_GROUP_DIM:])
    pr = jnp.concatenate(ps, axis=1).astype(BF16)
    qr = jnp.concatenate(qs, axis=1).astype(BF16)
    m2 = m2_ref[...]
    pad = jnp.zeros((FFT_PITCH - GRID_W, D_MODEL), jnp.uint32)
    for r in range(tm // GRID_W):
        sl = slice(r * GRID_W, (r + 1) * GRID_W)
        o = _dot(m2, jnp.concatenate([pr[sl], qr[sl]], axis=0))
        r0 = r * FFT_PITCH
        o_ref[0, r0:r0 + GRID_W, :] = _pack_complex(o[:GRID_W], o[GRID_W:])
        o_ref[0, r0 + GRID_W:r0 + FFT_PITCH, :] = pad


def _pack_complex(re, im):
    re_bits = pltpu.bitcast(re.astype(BF16).astype(F32), jnp.uint32)
    im_bits = pltpu.bitcast(im.astype(BF16).astype(F32), jnp.uint32)
    return (re_bits >> 16) | (im_bits & jnp.uint32(0xFFFF0000))


def _unpack_complex(word):
    re = pltpu.bitcast(word << 16, F32)
    im = pltpu.bitcast(word & jnp.uint32(0xFFFF0000), F32)
    return re, im


def _fft3_kernel(ri_ref, m3_ref, o_ref, scr_ref, *, rows):
    m3 = m3_ref[...]

    def column_pair(t, carry):
        w = 2 * t
        cols = [jnp.concatenate(_unpack_complex(ri_ref[0, pl.ds(w + dw, rows, stride=FFT_PITCH), :]), axis=0)
                for dw in range(2)]
        o = _dot(m3, jnp.concatenate(cols, axis=1).astype(BF16))
        for dw in range(2):
            scr_ref[pl.ds(w + dw, rows, stride=FFT_PITCH), :] = o[:, dw * LANE:(dw + 1) * LANE]
        return carry

    lax.fori_loop(0, GRID_W // 2, column_pair, 0, unroll=4)
    for k in range(rows):
        o_ref[0, k * GRID_W:(k + 1) * GRID_W, :] = scr_ref[k * FFT_PITCH:k * FFT_PITCH + GRID_W, :].astype(BF16)


def _fourier(u, tm):
    b, l, _ = u.shape
    rows = l // GRID_W
    w1, m2 = _dft_mats()
    ri = pl.pallas_call(
        functools.partial(_fft12_kernel, tm=tm),
        grid=(b, l // tm),
        in_specs=[pl.BlockSpec((1, tm, D_MODEL), lambda i, j: (i, j, U_FFT // D_MODEL)),
                  pl.BlockSpec((FFT_GROUP_DIM, 2 * FFT_GROUP_DIM), lambda i, j: (0, 0)),
                  pl.BlockSpec((2 * GRID_W, 2 * GRID_W), lambda i, j: (0, 0))],
        out_specs=pl.BlockSpec((1, tm // GRID_W * FFT_PITCH, D_MODEL), lambda i, j: (i, j, 0)),
        out_shape=jax.ShapeDtypeStruct((b, rows * FFT_PITCH, D_MODEL), jnp.uint32),
        compiler_params=pltpu.CompilerParams(dimension_semantics=("arbitrary", "arbitrary"),
                                             vmem_limit_bytes=VMEM_LIMIT),
        name="fft12",
    )(u, jnp.asarray(w1, BF16), jnp.asarray(m2, BF16))

    k = np.arange(rows)
    ang = 2.0 * np.pi * np.outer(k, k) / rows
    scale = 1.0 / np.sqrt(float(rows * GRID_W * FFT_GROUP_DIM))
    m3 = np.concatenate([np.cos(ang), np.sin(ang)], axis=1) * scale
    return pl.pallas_call(
        functools.partial(_fft3_kernel, rows=rows),
        grid=(b, D_MODEL // LANE),
        in_specs=[pl.BlockSpec((1, rows * FFT_PITCH, LANE), lambda i, j: (i, 0, j)),
                  pl.BlockSpec((rows, 2 * rows), lambda i, j: (0, 0))],
        out_specs=pl.BlockSpec((1, l, LANE), lambda i, j: (i, 0, j)),
        out_shape=jax.ShapeDtypeStruct((b, l, D_MODEL), BF16),
        scratch_shapes=[pltpu.VMEM((rows * FFT_PITCH, LANE), F32)],
        compiler_params=pltpu.CompilerParams(dimension_semantics=("arbitrary", "arbitrary"),
                                             vmem_limit_bytes=VMEM_LIMIT),
        name="fft3",
    )(ri, jnp.asarray(m3, BF16))


MXU_WIDTH = 256
FF_SLICES = ((0, 6 * MXU_WIDTH), (6 * MXU_WIDTH, D_FF))


def _tail_kernel(y_ref, f_ref, ga_ref, gb_ref, x_ref, g1_ref, sh2_ref, sc2_ref, g2_ref,
                 wso_ref, wfo_ref, wo_ref, n2_ref, wff_ref, wfd_ref, fg_ref, o_ref):
    a = _dot(y_ref[...], wso_ref[...])
    bq = _dot(f_ref[...], wfo_ref[...])
    merged = jax.nn.sigmoid(ga_ref[...].astype(F32)) * a + jax.nn.sigmoid(gb_ref[...].astype(F32)) * bq
    xn = x_ref[...] + g1_ref[0] * _dot(merged.astype(BF16), wo_ref[...])
    h = (_rmsnorm(xn, n2_ref[...]) * (1.0 + sc2_ref[0]) + sh2_ref[0]).astype(BF16)
    acc = None
    for lo, hi in FF_SLICES:
        gate = _dot(h, wff_ref[:, lo:hi])
        up = _dot(h, wff_ref[:, D_FF + lo:D_FF + hi])
        act = (gate * jax.nn.sigmoid(gate) * up).astype(BF16)
        part = _dot(act, wfd_ref[lo:hi, :])
        acc = part if acc is None else acc + part
    o_ref[...] = _rmsnorm(xn + g2_ref[0] * acc, fg_ref[...])


def _tail(y2d, f2d, u2d, x2d, g1, sh2, sc2, g2, wso, wfo, wo, n2, wff, wfd, fg, tm, rows_per_mod):
    m = x2d.shape[0]
    bpm = rows_per_mod // tm

    def const(shape):
        return pl.BlockSpec(shape, lambda i: (0, 0), pipeline_mode=pl.Buffered(1))

    def mod():
        return pl.BlockSpec((1, 1, D_MODEL), lambda i: (i // bpm, 0, 0))

    return pl.pallas_call(
        _tail_kernel,
        grid=(m // tm,),
        in_specs=[pl.BlockSpec((tm, D_INNER), lambda i: (i, 0)),
                  pl.BlockSpec((tm, D_MODEL), lambda i: (i, 0)),
                  pl.BlockSpec((tm, D_MODEL), lambda i: (i, U_GATE // D_MODEL)),
                  pl.BlockSpec((tm, D_MODEL), lambda i: (i, U_GATE // D_MODEL + 1)),
                  pl.BlockSpec((tm, D_MODEL), lambda i: (i, 0)),
                  mod(), mod(), mod(), mod(),
                  const((D_INNER, D_MODEL)), const((D_MODEL, D_MODEL)), const((D_MODEL, D_MODEL)),
                  const((1, D_MODEL)),
                  const((D_MODEL, 2 * D_FF)), const((D_FF, D_MODEL)),
                  const((1, D_MODEL))],
        out_specs=pl.BlockSpec((tm, D_MODEL), lambda i: (i, 0)),
        out_shape=jax.ShapeDtypeStruct((m, D_MODEL), F32),
        compiler_params=pltpu.CompilerParams(dimension_semantics=("arbitrary",),
                                             vmem_limit_bytes=VMEM_LIMIT),
        name="tail",
    )(y2d, f2d, u2d, u2d, x2d, g1, sh2, sc2, g2, wso, wfo, wo, n2, wff, wfd, fg)


def _dt_lanes(a):
    lead = a.shape[:-1]
    a = a.reshape(lead + (2, N_GROUPS, HEADS_PER_GROUP))
    a = jnp.swapaxes(a, -3, -2).reshape(lead + (2 * N_HEADS,))
    return jnp.concatenate([a, jnp.zeros(lead + (LANE - 2 * N_HEADS,), a.dtype)], axis=-1)


def _block(x, c, ctx, c_ctx, w_ada, b_ada, norm1_g, w_in, conv_w, conv_b, dt_bias, a_log, d_skip,
           ssd_norm_g, w_ssd_out, w_fft_out, w_o, norm2_g, w_ffn_in, w_ffn_out, final_g,
           proj_tm, prep_tl, ssd_tl, fft_tm, tail_tm):
    b, l, d = x.shape
    ctx_len = ctx.shape[1]
    assert d == D_MODEL and l % (GRID_W * 2) == 0 and ctx_len % CHUNK == 0 and b + 1 <= SUBLANE

    w_in_t = w_in[0].T
    w_main_t = w_in_t.astype(BF16)
    w_dt = _dt_lanes(w_in_t[REF_DT:REF_Z].T).T
    w_dt_hi = w_dt.astype(BF16)
    w_dt_lo = (w_dt - w_dt_hi.astype(F32)).astype(BF16)
    w_dt2 = jnp.concatenate([w_dt_hi, w_dt_lo], axis=0)
    conv_w_p = jnp.concatenate([conv_w[0], jnp.zeros((SUBLANE - D_CONV, CONV_DIM), F32)], axis=0)
    conv_b_p = conv_b[0].reshape(1, CONV_DIM)
    bias_v = _dt_lanes(dt_bias[0].reshape(-1)).reshape(1, LANE)
    alog_v = _dt_lanes(a_log[0].reshape(-1)).reshape(1, LANE)
    dskip_v = jnp.repeat(d_skip[0], HEAD_DIM).reshape(1, D_INNER)
    gain_v = ssd_norm_g[0].reshape(1, D_INNER)
    wso = w_ssd_out[0].astype(BF16)
    wfo = w_fft_out[0].astype(BF16)
    wo = w_o[0].astype(BF16)
    wff = w_ffn_in[0].astype(BF16)
    wfd = w_ffn_out[0].astype(BF16)

    cvecs = jnp.concatenate([c, c_ctx[None, :], jnp.zeros((SUBLANE - b - 1, D_MODEL), F32)], axis=0)
    mods = _mods(cvecs, b + 1, w_ada[0], b_ada[0])
    lat = mods[:b].reshape(b, 6, 1, D_MODEL)
    sh1, sc1, g1, sh2, sc2, g2 = (lat[:, k] for k in range(6))
    cm = mods[b].reshape(6, 1, 1, D_MODEL)

    x2d = x.reshape(b * l, D_MODEL)
    n1 = norm1_g[0].reshape(1, D_MODEL)
    u2d, dt_raw = _inproj(x2d, sh1, sc1, n1, w_main_t, w_dt2, U_WIDTH, proj_tm, l)
    u = u2d.reshape(b, l, U_WIDTH)
    cu2d, cdt_raw = _inproj(ctx.reshape(b * ctx_len, D_MODEL), cm[0], cm[1], n1, w_main_t, w_dt2,
                            CONV_DIM, b * ctx_len, b * ctx_len)
    cu = cu2d.reshape(b, ctx_len, CONV_DIM)

    rows = _dtprep(dt_raw.reshape(b, l, LANE), bias_v, alog_v, prep_tl)
    crows = _dtprep(cdt_raw.reshape(b, ctx_len, LANE), bias_v, alog_v, ctx_len)
    y_ssd = _ssd(u, rows, cu, crows, conv_w_p, conv_b_p, dskip_v, gain_v, ssd_tl)

    f_mix = _fourier(u, fft_tm)

    out = _tail(y_ssd.reshape(b * l, D_INNER), f_mix.reshape(b * l, D_MODEL), u2d, x2d,
                g1, sh2, sc2, g2, wso, wfo, wo, norm2_g[0].reshape(1, D_MODEL), wff, wfd,
                final_g.reshape(1, D_MODEL), tail_tm, l)
    return out.reshape(b, l, D_MODEL)


def kernel(x, c, ctx, c_ctx, w_ada, b_ada, norm1_g, w_in, conv_w, conv_b, dt_bias, a_log, d_skip,
           ssd_norm_g, w_ssd_out, w_fft_out, w_o, norm2_g, w_ffn_in, w_ffn_out, final_g):
    return _block(x, c, ctx, c_ctx, w_ada, b_ada, norm1_g, w_in, conv_w, conv_b, dt_bias, a_log, d_skip,
                  ssd_norm_g, w_ssd_out, w_fft_out, w_o, norm2_g, w_ffn_in, w_ffn_out, final_g,
                  proj_tm=2048, prep_tl=1024, ssd_tl=4096, fft_tm=1024, tail_tm=512)
```

```python
import functools
import math

import jax
import jax.numpy as jnp
import numpy as np
from jax import lax
from jax.experimental import pallas as pl
from jax.experimental.pallas import tpu as pltpu

F32 = jnp.float32
BF16 = jnp.bfloat16

D_MODEL = 1024
EPS = 1e-6
GRID_W = 64

N_GROUPS = 8
HEADS_PER_GROUP = 4
HEAD_DIM = 64
N_HEADS = N_GROUPS * HEADS_PER_GROUP
D_STATE = 128
D_INNER = N_HEADS * HEAD_DIM
GROUP_W = HEADS_PER_GROUP * HEAD_DIM
GS = N_GROUPS * D_STATE
CONV_DIM = D_INNER + 2 * GS
D_CONV = 5
CHUNK = 128

FFT_GROUPS = 8
FFT_GROUP_DIM = 128
D_FF = 2816

REF_DT = CONV_DIM
REF_Z = REF_DT + 2 * N_HEADS

U_B = D_INNER
U_C = U_B + GS
U_Z = CONV_DIM
U_FFT = U_Z + D_INNER
U_GATE = U_FFT + D_MODEL
U_WIDTH = U_GATE + 2 * D_MODEL

LANE = 128
SUBLANE = 8
HALO = 16
VMEM_LIMIT = 56 * 1024 * 1024

N_DH = 2 * HEADS_PER_GROUP
R_CUM, R_NCX, R_EY, R_WX, R_DIAG = 0, 8, 16, 24, 32
ROWS_PER_GROUP = 40
N_SLAB = (GROUP_W + 2 * D_STATE) // LANE
DT_FLOOR = 1e-37
LOG2E = math.log2(math.e)
N_CUM_TERMS = 3
FWD_PER_TRIP = 4
BWD_PER_TRIP = 16
N_BUFS = 4
FWD_AHEAD = (1, 1)
BWD_AHEAD = (3,)
FWD_WEAVE = "C0 C1 P0 P1"
BWD_WEAVE = "P0 C0"
K_ARGS = 2 * N_CUM_TERMS * N_DH
FFT_PITCH = GRID_W + 4


def _dot(a, b):
    return jnp.dot(a, b, preferred_element_type=F32)


def _dot_tn(a, b):
    return lax.dot_general(a, b, (((0,), (0,)), ((), ())), preferred_element_type=F32)


def _dot_nt(a, b):
    return lax.dot_general(a, b, (((1,), (1,)), ((), ())), preferred_element_type=F32)


def _rmsnorm(x, g):
    return x * lax.rsqrt(jnp.mean(x * x, axis=-1, keepdims=True) + EPS) * g


def _mods_kernel(ct_ref, w_ref, b_ref, o_ref, *, n_vecs):
    s = ct_ref[...]
    s = s * jax.nn.sigmoid(s)
    w = w_ref[...]
    rows = [jnp.sum(w * s[:, r:r + 1], axis=0, keepdims=True) for r in range(n_vecs)]
    rows.append(jnp.zeros((SUBLANE - n_vecs, w.shape[1]), F32))
    o_ref[...] = jnp.concatenate(rows, axis=0) + b_ref[...]


def _mods(cvecs, n_vecs, w_ada, b_ada):
    n = w_ada.shape[1]
    tn = 1536
    return pl.pallas_call(
        functools.partial(_mods_kernel, n_vecs=n_vecs),
        grid=(n // tn,),
        in_specs=[pl.BlockSpec((D_MODEL, SUBLANE), lambda j: (0, 0)),
                  pl.BlockSpec((D_MODEL, tn), lambda j: (0, j)),
                  pl.BlockSpec((1, tn), lambda j: (0, j))],
        out_specs=pl.BlockSpec((SUBLANE, tn), lambda j: (0, j)),
        out_shape=jax.ShapeDtypeStruct((SUBLANE, n), F32),
        compiler_params=pltpu.CompilerParams(dimension_semantics=("arbitrary",),
                                             vmem_limit_bytes=VMEM_LIMIT),
        name="mods",
    )(cvecs.T, w_ada, b_ada.reshape(1, n))


def _inproj_kernel(x_ref, sh_ref, sc_ref, g_ref, w_ref, wdt_ref, u_ref, dt_ref, h_ref):
    @pl.when(pl.program_id(1) == 0)
    def _():
        h = _rmsnorm(x_ref[...], g_ref[...]) * (1.0 + sc_ref[0]) + sh_ref[0]
        hb = h.astype(BF16)
        h_ref[...] = hb
        hl = (h - hb.astype(F32)).astype(BF16)
        a = _dot_nt(hb, wdt_ref[...])
        dt_ref[...] = a[:, :LANE] + a[:, LANE:] + _dot_nt(hl, wdt_ref[:LANE, :])

    u_ref[...] = _dot_nt(h_ref[...], w_ref[...]).astype(BF16)


def _inproj(x2d, shift, scale, gain, w_t, w_dt_t, n_cols, tm, rows_per_mod):
    m = x2d.shape[0]
    tn = 1024
    bpm = rows_per_mod // tm

    def w_rows(i, j):
        return pl.multiple_of(j * tn + jnp.where(j * tn >= REF_DT, REF_Z - REF_DT, 0), 2 * HALO), 0

    return pl.pallas_call(
        _inproj_kernel,
        grid=(m // tm, n_cols // tn),
        in_specs=[pl.BlockSpec((tm, D_MODEL), lambda i, j: (i, 0)),
                  pl.BlockSpec((1, 1, D_MODEL), lambda i, j: (i // bpm, 0, 0)),
                  pl.BlockSpec((1, 1, D_MODEL), lambda i, j: (i // bpm, 0, 0)),
                  pl.BlockSpec((1, D_MODEL), lambda i, j: (0, 0)),
                  pl.BlockSpec((pl.Element(tn), pl.Element(D_MODEL)), w_rows),
                  pl.BlockSpec((2 * LANE, D_MODEL), lambda i, j: (0, 0))],
        out_specs=[pl.BlockSpec((tm, tn), lambda i, j: (i, j)),
                   pl.BlockSpec((tm, LANE), lambda i, j: (i, 0))],
        out_shape=[jax.ShapeDtypeStruct((m, n_cols), BF16),
                   jax.ShapeDtypeStruct((m, LANE), F32)],
        scratch_shapes=[pltpu.VMEM((tm, D_MODEL), BF16)],
        compiler_params=pltpu.CompilerParams(dimension_semantics=("arbitrary", "arbitrary"),
                                             vmem_limit_bytes=VMEM_LIMIT),
        name="inproj",
    )(x2d, shift, scale, gain, w_t, w_dt_t)


def _dtprep_kernel(raw_ref, bias_ref, alog_ref, row_ref, *, n_chunks):
    a = -jnp.exp(alog_ref[...])
    bias = bias_ref[...]
    ii = lax.broadcasted_iota(jnp.int32, (CHUNK, CHUNK), 0)
    jj = lax.broadcasted_iota(jnp.int32, (CHUNK, CHUNK), 1)
    tril = (jj <= ii).astype(BF16)
    triu = (jj >= ii).astype(BF16)
    lane = lax.broadcasted_iota(jnp.int32, (1, LANE), 1)
    is_fwd = (lane % N_DH) < HEADS_PER_GROUP

    def tri_sum(tri, terms):
        s = _dot(tri, terms)
        return s[:, :LANE] + s[:, LANE:2 * LANE] + s[:, 2 * LANE:]

    for c in range(n_chunks):
        sl = slice(c * CHUNK, (c + 1) * CHUNK)
        dt = jax.nn.softplus(raw_ref[0, sl, :] + bias)
        dta = dt * a
        terms = jnp.concatenate(_split_terms(dta, N_CUM_TERMS), axis=1).astype(BF16)
        cf = tri_sum(tril, terms)
        cb = tri_sum(triu, terms)
        cum = jnp.where(is_fwd, cf, cb)
        tot = jnp.where(is_fwd, cf[CHUNK - 1:CHUNK, :], cb[0:1, :])
        wx = dt * jnp.exp(tot - cum)
        ey = jnp.exp(cum)
        ncx = jnp.log(jnp.maximum(dt, DT_FLOOR)) - cum
        diag = jnp.log(dt + pltpu.roll(dt, LANE - HEADS_PER_GROUP, 1))
        for r_off, val in ((R_CUM, cum * LOG2E), (R_NCX, ncx * LOG2E), (R_EY, ey), (R_WX, wx),
                           (R_DIAG, diag * LOG2E)):
            vt = val.T
            for g in range(N_GROUPS):
                r0 = g * ROWS_PER_GROUP + r_off
                row_ref[0, r0:r0 + N_DH, sl] = vt[g * N_DH:(g + 1) * N_DH, :]


def _dtprep(raw, bias, alog, tl):
    b, l, _ = raw.shape
    return pl.pallas_call(
        functools.partial(_dtprep_kernel, n_chunks=tl // CHUNK),
        grid=(b, l // tl),
        in_specs=[pl.BlockSpec((1, tl, LANE), lambda i, j: (i, j, 0)),
                  pl.BlockSpec((1, LANE), lambda i, j: (0, 0)),
                  pl.BlockSpec((1, LANE), lambda i, j: (0, 0))],
        out_specs=pl.BlockSpec((1, N_GROUPS * ROWS_PER_GROUP, tl), lambda i, j: (i, 0, j)),
        out_shape=jax.ShapeDtypeStruct((b, N_GROUPS * ROWS_PER_GROUP, l), F32),
        compiler_params=pltpu.CompilerParams(dimension_semantics=("arbitrary", "arbitrary"),
                                             vmem_limit_bytes=VMEM_LIMIT),
        name="dtprep",
    )(raw, bias, alog)


def _split_terms(x, n):
    terms = []
    for _ in range(n - 1):
        t = x.astype(BF16).astype(F32)
        terms.append(t)
        x = x - t
    terms.append(x)
    return terms


def _scale_rows(rows):
    return jnp.concatenate(_split_terms(rows[R_EY:R_EY + N_DH], 2) + _split_terms(rows[R_WX:R_WX + N_DH], 2),
                           axis=0).astype(BF16)


def _conv_silu(ext_refs, s, p0, cw, cbias):
    sl = slice(s * LANE, (s + 1) * LANE)
    acc = cbias[:, sl].astype(BF16)
    for k in range(D_CONV):
        pair = p0 + (HALO + k - D_CONV // 2 - k % 2) // 2
        tap = pltpu.bitcast(ext_refs[k % 2][s, pl.ds(pair, CHUNK // 2), :], BF16)
        acc = acc + cw[k:k + 1, sl].astype(BF16) * tap
    return acc * jax.nn.sigmoid(acc)


def _ssd_kernel(x_ref, b_ref, c_ref, xp_ref, bp_ref, cp_ref, xn_ref, bn_ref, cn_ref, z_ref, row_ref,
                cx_ref, cb_ref, crow_ref, cwx_ref, cwb_ref, cwc_ref, cbx_ref, cbb_ref, cbc_ref,
                dskip_ref, gain_ref, onesel_ref, sel_ref, out_ref,
                cache_ref, yacc_ref, sf_ref, sb_ref, ext0_ref, ext1_ref, *buf_refs, tl, n_blk, ctx_len):
    p = pl.program_id(2)
    i = pl.program_id(3)
    n_ch = tl // CHUNK
    args_bufs = buf_refs[0:N_BUFS]
    ew_bufs = buf_refs[N_BUFS:2 * N_BUFS]
    st_bufs = buf_refs[2 * N_BUFS:3 * N_BUFS]
    ext_refs = (ext0_ref, ext1_ref)
    cw = jnp.concatenate([cwx_ref[...], cwb_ref[...], cwc_ref[...]], axis=1)
    cbias = jnp.concatenate([cbx_ref[...], cbb_ref[...], cbc_ref[...]], axis=1)

    def slabs_of(x, bm, cm):
        return [x[:, :LANE], x[:, LANE:], bm, cm]

    def slab_refs(xr, br, cr):
        return [(xr, 0), (xr, LANE), (br, 0), (cr, 0)]

    def fill_ext(src, first, last, n_rows):
        half = HALO // 2
        n_piece = n_rows // CHUNK

        def copy_piece(k, carry):
            r = pl.multiple_of(k * CHUNK, CHUNK)
            q = pl.multiple_of(k * (CHUNK // 2), CHUNK // 2)
            for s, (ref, lane0) in enumerate(src):
                ext0_ref[s, pl.ds(half + q, CHUNK // 2), :] = pltpu.bitcast(
                    ref[0, pl.ds(r, CHUNK), lane0:lane0 + LANE], jnp.uint32)
            return carry

        def shift_piece(k, carry):
            q = pl.multiple_of(k * (CHUNK // 2), CHUNK // 2)
            for s in range(len(src)):
                lo = ext0_ref[s, pl.ds(q, CHUNK // 2), :]
                hi = ext0_ref[s, pl.ds(q + 1, CHUNK // 2), :]
                ext1_ref[s, pl.ds(q, CHUNK // 2), :] = (lo >> 16) | (hi << 16)
            return carry

        for s in range(len(src)):
            ext0_ref[s, 0:half, :] = pltpu.bitcast(first[s], jnp.uint32)
            ext0_ref[s, half + n_rows // 2:2 * half + n_rows // 2, :] = pltpu.bitcast(last[s], jnp.uint32)
        lax.fori_loop(0, n_piece, copy_piece, 0, unroll=2)
        lax.fori_loop(0, n_piece, shift_piece, 0, unroll=2)
        for s in range(len(src)):
            q = n_rows // 2
            ext1_ref[s, q:q + half, :] = (ext0_ref[s, q:q + half, :] >> 16) | (ext0_ref[s, q + 1:q + half + 1, :] << 16)

    def row_of(c):
        return c * CHUNK if isinstance(c, int) else pl.multiple_of(c * CHUNK, CHUNK)

    def expand(rowt, d):
        return _dot_tn(_scale_rows(rowt), sel_ref[d])

    def run_pipelined(prepare_parts, consume_parts, weave, order, per_trip, aheads):
        def step(k, d, in_loop):
            handed = None
            for item in weave.split():
                n = int(item[1:])
                if item[0] == "P":
                    if in_loop or k + aheads[n] < n_ch:
                        prepare_parts[n](order(k + aheads[n]), (d + aheads[n]) % N_BUFS)
                else:
                    handed = consume_parts[n](order(k), d % N_BUFS, handed)

        for n, part in enumerate(prepare_parts):
            for a in range(aheads[n]):
                part(order(a), a)

        def trip(t, carry):
            for d in range(per_trip):
                step(per_trip * t + d, d, True)
            return carry

        lax.fori_loop(0, n_ch // per_trip - 1, trip, 0)
        for d in range(per_trip):
            step(n_ch - per_trip + d, d, False)

    @pl.when(jnp.logical_and(p == 0, i == 0))
    def _init_from_context():
        zpad = [jnp.zeros((HALO, LANE), BF16)] * (N_SLAB - 1)
        fill_ext(slab_refs(cx_ref, cb_ref, None)[:N_SLAB - 1], zpad, zpad, ctx_len)
        n_cc = ctx_len // CHUNK
        sf = jnp.zeros((D_STATE, GROUP_W), F32)
        st_b, ey_b = [], []
        for cc in range(n_cc):
            slabs = [_conv_silu(ext_refs, s, cc * CHUNK // 2, cw, cbias) for s in range(N_SLAB - 1)]
            xs = jnp.concatenate(slabs[:2], axis=1).astype(F32)
            bm = slabs[2]
            rowt = crow_ref[0, :, cc * CHUNK:(cc + 1) * CHUNK]
            ewf = expand(rowt, 0)
            ewb = expand(rowt, 1)
            sf = sf * ewf[CHUNK - 1:CHUNK, :GROUP_W] + _dot_tn(bm, (xs * ewf[:, GROUP_W:]).astype(BF16))
            st_b.append(_dot_tn(bm, (xs * ewb[:, GROUP_W:]).astype(BF16)))
            ey_b.append(ewb[0:1, :GROUP_W])
        sb = jnp.zeros((D_STATE, GROUP_W), F32)
        for cc in reversed(range(n_cc)):
            sb = sb * ey_b[cc] + st_b[cc]
        sf_ref[...] = sf
        sb_ref[...] = sb

    @pl.when(p == 0)
    def _forward():
        blk = i
        base = blk * tl
        has_prev = blk > 0
        has_next = blk < n_blk - 1
        fill_ext(slab_refs(x_ref, b_ref, c_ref),
                 [jnp.where(has_prev, v, jnp.zeros_like(v)) for v in slabs_of(xp_ref[0], bp_ref[0], cp_ref[0])],
                 [jnp.where(has_next, v, jnp.zeros_like(v)) for v in slabs_of(xn_ref[0], bn_ref[0], cn_ref[0])],
                 tl)
        ii = lax.broadcasted_iota(jnp.int32, (CHUNK, CHUNK), 0)
        jj = lax.broadcasted_iota(jnp.int32, (CHUNK, CHUNK), 1)
        lower = jj < ii
        upper = jj > ii

        def prepare_conv(c, buf):
            r = row_of(c)
            p0 = c * (CHUNK // 2) if isinstance(c, int) else pl.multiple_of(c * (CHUNK // 2), CHUNK // 2)
            slabs = [_conv_silu(ext_refs, s, p0, cw, cbias) for s in range(N_SLAB)]
            for s in range(N_SLAB):
                cache_ref[pl.ds(base + r, CHUNK), s * LANE:(s + 1) * LANE] = slabs[s]
            yacc_ref[pl.ds(base + r, CHUNK), :] = jnp.concatenate(slabs[:2], axis=1).astype(F32) * dskip_ref[...]

        def prepare_scales(c, buf):
            rowt = row_ref[0, :, pl.ds(row_of(c), CHUNK)]
            onesel = onesel_ref[...]
            ones = jnp.ones((N_DH, CHUNK), F32)
            at = jnp.concatenate(_split_terms(rowt[R_CUM:R_CUM + N_DH], N_CUM_TERMS) + [ones] * N_CUM_TERMS,
                                 axis=0).astype(BF16)
            wide = [jnp.concatenate([t] * N_DH, axis=1) * onesel
                    for t in _split_terms(rowt[R_NCX:R_NCX + N_DH], N_CUM_TERMS)]
            bmat = jnp.concatenate([onesel] * N_CUM_TERMS + wide, axis=0).astype(BF16)
            args_bufs[buf][...] = _dot_tn(at, bmat)
            ew_bufs[buf][...] = expand(rowt, 0)

        def consume_state(c, buf, _):
            r = row_of(c)
            xbc_b = cache_ref[pl.ds(base + r, CHUNK), :]
            xs_b = xbc_b[:, :GROUP_W]
            bm = xbc_b[:, GROUP_W:GROUP_W + D_STATE]
            cm = xbc_b[:, GROUP_W + D_STATE:]
            return r, xs_b, bm, cm, _dot_nt(cm, bm)

        def consume_decay(c, buf, handed):
            r, xs_b, bm, cm, cbm = handed
            yd = []
            for j in range(HEADS_PER_GROUP):
                argf = args_bufs[buf][:, j * CHUNK:(j + 1) * CHUNK]
                argb = args_bufs[buf][:, (HEADS_PER_GROUP + j) * CHUNK:(HEADS_PER_GROUP + j + 1) * CHUNK]
                diag = row_ref[0, R_DIAG + j:R_DIAG + j + 1, pl.ds(r, CHUNK)]
                arg = jnp.where(lower, argf, jnp.where(upper, argb, diag))
                wmat = (cbm * jnp.exp2(arg)).astype(BF16)
                yd.append(_dot(wmat, xs_b[:, j * HEAD_DIM:(j + 1) * HEAD_DIM]))
            ef = ew_bufs[buf][:, :GROUP_W]
            wxf = ew_bufs[buf][:, GROUP_W:]
            sf = sf_ref[...]
            yacc_ref[pl.ds(base + r, CHUNK), :] += jnp.concatenate(yd, axis=1) + _dot(cm, sf.astype(BF16)) * ef
            sf_ref[...] = sf * ef[CHUNK - 1:CHUNK, :] + _dot_tn(bm, xs_b * wxf.astype(BF16))

        run_pipelined([prepare_conv, prepare_scales], [consume_state, consume_decay], FWD_WEAVE,
                      lambda k: k, FWD_PER_TRIP, FWD_AHEAD)

    @pl.when(p == 1)
    def _backward():
        blk = n_blk - 1 - i
        base = blk * tl

        def prepare(c, buf):
            r = row_of(c)
            ew = expand(row_ref[0, :, pl.ds(r, CHUNK)], 1)
            ew_bufs[buf][...] = ew
            xs_b = cache_ref[pl.ds(base + r, CHUNK), 0:GROUP_W]
            bm = cache_ref[pl.ds(base + r, CHUNK), GROUP_W:GROUP_W + D_STATE]
            st_bufs[buf][...] = _dot_tn(bm, xs_b * ew[:, GROUP_W:].astype(BF16))

        def consume(c, buf, _):
            r = row_of(c)
            cm = cache_ref[pl.ds(base + r, CHUNK), GROUP_W + D_STATE:]
            eb = ew_bufs[buf][:, :GROUP_W]
            sb = sb_ref[...]
            y = yacc_ref[pl.ds(base + r, CHUNK), :] + _dot(cm, sb.astype(BF16)) * eb
            sb_ref[...] = sb * eb[0:1, :] + st_bufs[buf][...]
            z = z_ref[0, pl.ds(r, CHUNK), :]
            y = y * (z * jax.nn.sigmoid(z)).astype(F32)
            out_ref[0, pl.ds(r, CHUNK), :] = _rmsnorm(y, gain_ref[...]).astype(BF16)

        run_pipelined([prepare], [consume], BWD_WEAVE, lambda k: n_ch - 1 - k, BWD_PER_TRIP, BWD_AHEAD)


def _ssd_selectors():
    onesel = (np.arange(N_DH * CHUNK)[None, :] // CHUNK == np.arange(N_DH)[:, None]).astype(np.float32)
    sel = np.zeros((2, 4 * N_DH, 2 * GROUP_W), np.float32)
    for d in range(2):
        for j in range(HEADS_PER_GROUP):
            dh = d * HEADS_PER_GROUP + j
            for term in range(2):
                sel[d, term * N_DH + dh, j * HEAD_DIM:(j + 1) * HEAD_DIM] = 1.0
                sel[d, (2 + term) * N_DH + dh, GROUP_W + j * HEAD_DIM:GROUP_W + (j + 1) * HEAD_DIM] = 1.0
    return jnp.asarray(onesel, F32), jnp.asarray(sel, BF16)


def _ssd(u, rows, cu, crows, conv_w, conv_b, dskip, gain, tl):
    b, l, _ = u.shape
    ctx_len = cu.shape[1]
    assert ctx_len <= tl
    n_blk = l // tl
    hb = tl // HALO
    last_halo = l // HALO - 1
    assert (tl // CHUNK) % FWD_PER_TRIP == 0 and (tl // CHUNK) % BWD_PER_TRIP == 0
    onesel, sel = _ssd_selectors()
    xg, bg, cg = 0, U_B // D_STATE, U_C // D_STATE

    def in_blk(p, i):
        return i * (1 - p) + (n_blk - 1) * p

    def seq_blk(p, i):
        return i * (1 - p) + (n_blk - 1 - i) * p

    def out_blk(p, i):
        return (n_blk - 1) * (1 - p) + (n_blk - 1 - i) * p

    def prev_halo(p, i):
        return jnp.maximum(in_blk(p, i) * hb - 1, 0)

    def next_halo(p, i):
        return jnp.minimum((in_blk(p, i) + 1) * hb, last_halo)

    def xbc_specs(rows_blk, row_index):
        return [pl.BlockSpec((1, rows_blk, GROUP_W), lambda bb, g, p, i: (bb, row_index(p, i), xg + g)),
                pl.BlockSpec((1, rows_blk, D_STATE), lambda bb, g, p, i: (bb, row_index(p, i), bg + g)),
                pl.BlockSpec((1, rows_blk, D_STATE), lambda bb, g, p, i: (bb, row_index(p, i), cg + g))]

    def param_specs(n_rows):
        return [pl.BlockSpec((n_rows, GROUP_W), lambda bb, g, p, i: (0, xg + g)),
                pl.BlockSpec((n_rows, D_STATE), lambda bb, g, p, i: (0, bg + g)),
                pl.BlockSpec((n_rows, D_STATE), lambda bb, g, p, i: (0, cg + g))]

    grid = (b, N_GROUPS, 2, n_blk)
    in_specs = (
        xbc_specs(tl, in_blk) + xbc_specs(HALO, prev_halo) + xbc_specs(HALO, next_halo)
        + [pl.BlockSpec((1, tl, GROUP_W), lambda bb, g, p, i: (bb, out_blk(p, i), U_Z // GROUP_W + g)),
           pl.BlockSpec((1, ROWS_PER_GROUP, tl), lambda bb, g, p, i: (bb, g, seq_blk(p, i)))]
        + xbc_specs(ctx_len, lambda p, i: 0)[:2]
        + [pl.BlockSpec((1, ROWS_PER_GROUP, ctx_len), lambda bb, g, p, i: (bb, g, 0))]
        + param_specs(SUBLANE) + param_specs(1)
        + [pl.BlockSpec((1, GROUP_W), lambda bb, g, p, i: (0, g)),
           pl.BlockSpec((1, GROUP_W), lambda bb, g, p, i: (0, g)),
           pl.BlockSpec(onesel.shape, lambda bb, g, p, i: (0, 0)),
           pl.BlockSpec(sel.shape, lambda bb, g, p, i: (0, 0, 0))])
    return pl.pallas_call(
        functools.partial(_ssd_kernel, tl=tl, n_blk=n_blk, ctx_len=ctx_len),
        grid=grid,
        in_specs=in_specs,
        out_specs=pl.BlockSpec((1, tl, GROUP_W), lambda bb, g, p, i: (bb, out_blk(p, i), g)),
        out_shape=jax.ShapeDtypeStruct((b, l, D_INNER), BF16),
        scratch_shapes=[pltpu.VMEM((l, N_SLAB * LANE), BF16),
                        pltpu.VMEM((l, GROUP_W), F32),
                        pltpu.VMEM((D_STATE, GROUP_W), F32),
                        pltpu.VMEM((D_STATE, GROUP_W), F32),
                        pltpu.VMEM((N_SLAB, (tl + 2 * HALO) // 2, LANE), jnp.uint32),
                        pltpu.VMEM((N_SLAB, (tl + 2 * HALO) // 2, LANE), jnp.uint32)]
        + [pltpu.VMEM((CHUNK, N_DH * CHUNK), F32)] * N_BUFS
        + [pltpu.VMEM((CHUNK, 2 * GROUP_W), F32)] * N_BUFS
        + [pltpu.VMEM((D_STATE, GROUP_W), F32)] * N_BUFS,
        compiler_params=pltpu.CompilerParams(
            dimension_semantics=("arbitrary", "arbitrary", "arbitrary", "arbitrary"),
            vmem_limit_bytes=VMEM_LIMIT),
        name="ssd",
    )(u, u, u, u, u, u, u, u, u, u, rows, cu, cu, crows,
      conv_w, conv_w, conv_w, conv_b, conv_b, conv_b, dskip, gain, onesel, sel)


def _dft_mats():
    def cs(n):
        k = np.arange(n)
        ang = 2.0 * np.pi * np.outer(k, k) / n
        return np.cos(ang), np.sin(ang)

    c3, s3 = cs(FFT_GROUP_DIM)
    w1 = np.concatenate([c3, -s3], axis=1)
    c2, s2 = cs(GRID_W)
    m2 = np.block([[c2, s2], [-s2, c2]])
    return w1, m2


def _fft12_kernel(f_ref, w1_ref, m2_ref, o_ref, *, tm):
    f = f_ref[0]
    w1 = w1_ref[...]
    ps, qs = [], []
    for g in range(FFT_GROUPS):
        pq = _dot(f[:, g * FFT_GROUP_DIM:(g + 1) * FFT_GROUP_DIM], w1)
        ps.append(pq[:, :FFT_GROUP_DIM])
        qs.append(pq[:, FFT_GROUP_DIM:])
    pr = jnp.concatenate(ps, axis=1).astype(BF16)
    qr = jnp.concatenate(qs, axis=1).astype(BF16)
    m2 = m2_ref[...]
    pad = jnp.zeros((FFT_PITCH - GRID_W, D_MODEL), jnp.uint32)
    for r in range(tm // GRID_W):
        sl = slice(r * GRID_W, (r + 1) * GRID_W)
        o = _dot(m2, jnp.concatenate([pr[sl], qr[sl]], axis=0))
        r0 = r * FFT_PITCH
        o_ref[0, r0:r0 + GRID_W, :] = _pack_complex(o[:GRID_W], o[GRID_W:])
        o_ref[0, r0 + GRID_W:r0 + FFT_PITCH, :] = pad


def _pack_complex(re, im):
    re_bits = pltpu.bitcast(re.astype(BF16).astype(F32), jnp.uint32)
    im_bits = pltpu.bitcast(im.astype(BF16).astype(F32), jnp.uint32)
    return (re_bits >> 16) | (im_bits & jnp.uint32(0xFFFF0000))


def _unpack_complex(word):
    re = pltpu.bitcast(word << 16, F32)
    im = pltpu.bitcast(word & jnp.uint32(0xFFFF0000), F32)
    return re, im


def _fft3_kernel(ri_ref, m3_ref, o_ref, scr_ref, *, rows):
    m3 = m3_ref[...]

    def column_pair(t, carry):
        w = 2 * t
        cols = [jnp.concatenate(_unpack_complex(ri_ref[0, pl.ds(w + dw, rows, stride=FFT_PITCH), :]), axis=0)
                for dw in range(2)]
        o = _dot(m3, jnp.concatenate(cols, axis=1).astype(BF16))
        for dw in range(2):
            scr_ref[pl.ds(w + dw, rows, stride=FFT_PITCH), :] = o[:, dw * LANE:(dw + 1) * LANE]
        return carry

    lax.fori_loop(0, GRID_W // 2, column_pair, 0, unroll=4)
    for k in range(rows):
        o_ref[0, k * GRID_W:(k + 1) * GRID_W, :] = scr_ref[k * FFT_PITCH:k * FFT_PITCH + GRID_W, :].astype(BF16)


def _fourier(u, tm):
    b, l, _ = u.shape
    rows = l // GRID_W
    w1, m2 = _dft_mats()
    ri = pl.pallas_call(
        functools.partial(_fft12_kernel, tm=tm),
        grid=(b, l // tm),
        in_specs=[pl.BlockSpec((1, tm, D_MODEL), lambda i, j: (i, j, U_FFT // D_MODEL)),
                  pl.BlockSpec((FFT_GROUP_DIM, 2 * FFT_GROUP_DIM), lambda i, j: (0, 0)),
                  pl.BlockSpec((2 * GRID_W, 2 * GRID_W), lambda i, j: (0, 0))],
        out_specs=pl.BlockSpec((1, tm // GRID_W * FFT_PITCH, D_MODEL), lambda i, j: (i, j, 0)),
        out_shape=jax.ShapeDtypeStruct((b, rows * FFT_PITCH, D_MODEL), jnp.uint32),
        compiler_params=pltpu.CompilerParams(dimension_semantics=("arbitrary", "arbitrary"),
                                             vmem_limit_bytes=VMEM_LIMIT),
        name="fft12",
    )(u, jnp.asarray(w1, BF16), jnp.asarray(m2, BF16))

    k = np.arange(rows)
    ang = 2.0 * np.pi * np.outer(k, k) / rows
    scale = 1.0 / np.sqrt(float(rows * GRID_W * FFT_GROUP_DIM))
    m3 = np.concatenate([np.cos(ang), np.sin(ang)], axis=1) * scale
    return pl.pallas_call(
        functools.partial(_fft3_kernel, rows=rows),
        grid=(b, D_MODEL // LANE),
        in_specs=[pl.BlockSpec((1, rows * FFT_PITCH, LANE), lambda i, j: (i, 0, j)),
                  pl.BlockSpec((rows, 2 * rows), lambda i, j: (0, 0))],
        out_specs=pl.BlockSpec((1, l, LANE), lambda i, j: (i, 0, j)),
        out_shape=jax.ShapeDtypeStruct((b, l, D_MODEL), BF16),
        scratch_shapes=[pltpu.VMEM((rows * FFT_PITCH, LANE), F32)],
        compiler_params=pltpu.CompilerParams(dimension_semantics=("arbitrary", "arbitrary"),
                                             vmem_limit_bytes=VMEM_LIMIT),
        name="fft3",
    )(ri, jnp.asarray(m3, BF16))


MXU_WIDTH = 256
FF_SLICES = ((0, 6 * MXU_WIDTH), (6 * MXU_WIDTH, D_FF))


def _tail_kernel(y_ref, f_ref, ga_ref, gb_ref, x_ref, g1_ref, sh2_ref, sc2_ref, g2_ref,
                 wso_ref, wfo_ref, wo_ref, n2_ref, wff_ref, wfd_ref, fg_ref, o_ref):
    a = _dot(y_ref[...], wso_ref[...])
    bq = _dot(f_ref[...], wfo_ref[...])
    merged = jax.nn.sigmoid(ga_ref[...].astype(F32)) * a + jax.nn.sigmoid(gb_ref[...].astype(F32)) * bq
    xn = x_ref[...] + g1_ref[0] * _dot(merged.astype(BF16), wo_ref[...])
    h = (_rmsnorm(xn, n2_ref[...]) * (1.0 + sc2_ref[0]) + sh2_ref[0]).astype(BF16)
    acc = None
    for lo, hi in FF_SLICES:
        gate = _dot(h, wff_ref[:, lo:hi])
        up = _dot(h, wff_ref[:, D_FF + lo:D_FF + hi])
        act = (gate * jax.nn.sigmoid(gate) * up).astype(BF16)
        part = _dot(act, wfd_ref[lo:hi, :])
        acc = part if acc is None else acc + part
    o_ref[...] = _rmsnorm(xn + g2_ref[0] * acc, fg_ref[...])


def _tail(y2d, f2d, u2d, x2d, g1, sh2, sc2, g2, wso, wfo, wo, n2, wff, wfd, fg, tm, rows_per_mod):
    m = x2d.shape[0]
    bpm = rows_per_mod // tm

    def const(shape):
        return pl.BlockSpec(shape, lambda i: (0, 0), pipeline_mode=pl.Buffered(1))

    def mod():
        return pl.BlockSpec((1, 1, D_MODEL), lambda i: (i // bpm, 0, 0))

    return pl.pallas_call(
        _tail_kernel,
        grid=(m // tm,),
        in_specs=[pl.BlockSpec((tm, D_INNER), lambda i: (i, 0)),
                  pl.BlockSpec((tm, D_MODEL), lambda i: (i, 0)),
                  pl.BlockSpec((tm, D_MODEL), lambda i: (i, U_GATE // D_MODEL)),
                  pl.BlockSpec((tm, D_MODEL), lambda i: (i, U_GATE // D_MODEL + 1)),
                  pl.BlockSpec((tm, D_MODEL), lambda i: (i, 0)),
                  mod(), mod(), mod(), mod(),
                  const((D_INNER, D_MODEL)), const((D_MODEL, D_MODEL)), const((D_MODEL, D_MODEL)),
                  const((1, D_MODEL)),
                  const((D_MODEL, 2 * D_FF)), const((D_FF, D_MODEL)),
                  const((1, D_MODEL))],
        out_specs=pl.BlockSpec((tm, D_MODEL), lambda i: (i, 0)),
        out_shape=jax.ShapeDtypeStruct((m, D_MODEL), F32),
        compiler_params=pltpu.CompilerParams(dimension_semantics=("arbitrary",),
                                             vmem_limit_bytes=VMEM_LIMIT),
        name="tail",
    )(y2d, f2d, u2d, u2d, x2d, g1, sh2, sc2, g2, wso, wfo, wo, n2, wff, wfd, fg)


def _dt_lanes(a):
    lead = a.shape[:-1]
    a = a.reshape(lead + (2, N_GROUPS, HEADS_PER_GROUP))
    a = jnp.swapaxes(a, -3, -2).reshape(lead + (2 * N_HEADS,))
    return jnp.concatenate([a, jnp.zeros(lead + (LANE - 2 * N_HEADS,), a.dtype)], axis=-1)


def _block(x, c, ctx, c_ctx, w_ada, b_ada, norm1_g, w_in, conv_w, conv_b, dt_bias, a_log, d_skip,
           ssd_norm_g, w_ssd_out, w_fft_out, w_o, norm2_g, w_ffn_in, w_ffn_out, final_g,
           proj_tm, prep_tl, ssd_tl, fft_tm, tail_tm):
    b, l, d = x.shape
    ctx_len = ctx.shape[1]
    assert d == D_MODEL and l % (GRID_W * 2) == 0 and ctx_len % CHUNK == 0 and b + 1 <= SUBLANE

    w_in_t = w_in[0].T
    w_main_t = w_in_t.astype(BF16)
    w_dt = _dt_lanes(w_in_t[REF_DT:REF_Z].T).T
    w_dt_hi = w_dt.astype(BF16)
    w_dt_lo = (w_dt - w_dt_hi.astype(F32)).astype(BF16)
    w_dt2 = jnp.concatenate([w_dt_hi, w_dt_lo], axis=0)
    conv_w_p = jnp.concatenate([conv_w[0], jnp.zeros((SUBLANE - D_CONV, CONV_DIM), F32)], axis=0)
    conv_b_p = conv_b[0].reshape(1, CONV_DIM)
    bias_v = _dt_lanes(dt_bias[0].reshape(-1)).reshape(1, LANE)
    alog_v = _dt_lanes(a_log[0].reshape(-1)).reshape(1, LANE)
    dskip_v = jnp.repeat(d_skip[0], HEAD_DIM).reshape(1, D_INNER)
    gain_v = ssd_norm_g[0].reshape(1, D_INNER)
    wso = w_ssd_out[0].astype(BF16)
    wfo = w_fft_out[0].astype(BF16)
    wo = w_o[0].astype(BF16)
    wff = w_ffn_in[0].astype(BF16)
    wfd = w_ffn_out[0].astype(BF16)

    cvecs = jnp.concatenate([c, c_ctx[None, :], jnp.zeros((SUBLANE - b - 1, D_MODEL), F32)], axis=0)
    mods = _mods(cvecs, b + 1, w_ada[0], b_ada[0])
    lat = mods[:b].reshape(b, 6, 1, D_MODEL)
    sh1, sc1, g1, sh2, sc2, g2 = (lat[:, k] for k in range(6))
    cm = mods[b].reshape(6, 1, 1, D_MODEL)

    x2d = x.reshape(b * l, D_MODEL)
    n1 = norm1_g[0].reshape(1, D_MODEL)
    u2d, dt_raw = _inproj(x2d, sh1, sc1, n1, w_main_t, w_dt2, U_WIDTH, proj_tm, l)
    u = u2d.reshape(b, l, U_WIDTH)
    cu2d, cdt_raw = _inproj(ctx.reshape(b * ctx_len, D_MODEL), cm[0], cm[1], n1, w_main_t, w_dt2,
                            CONV_DIM, b * ctx_len, b * ctx_len)
    cu = cu2d.reshape(b, ctx_len, CONV_DIM)

    rows = _dtprep(dt_raw.reshape(b, l, LANE), bias_v, alog_v, prep_tl)
    crows = _dtprep(cdt_raw.reshape(b, ctx_len, LANE), bias_v, alog_v, ctx_len)
    y_ssd = _ssd(u, rows, cu, crows, conv_w_p, conv_b_p, dskip_v, gain_v, ssd_tl)

    f_mix = _fourier(u, fft_tm)

    out = _tail(y_ssd.reshape(b * l, D_INNER), f_mix.reshape(b * l, D_MODEL), u2d, x2d,
                g1, sh2, sc2, g2, wso, wfo, wo, norm2_g[0].reshape(1, D_MODEL), wff, wfd,
                final_g.reshape(1, D_MODEL), tail_tm, l)
    return out.reshape(b, l, D_MODEL)


def kernel(x, c, ctx, c_ctx, w_ada, b_ada, norm1_g, w_in, conv_w, conv_b, dt_bias, a_log, d_skip,
           ssd_norm_g, w_ssd_out, w_fft_out, w_o, norm2_g, w_ffn_in, w_ffn_out, final_g):
    return _block(x, c, ctx, c_ctx, w_ada, b_ada, norm1_g, w_in, conv_w, conv_b, dt_bias, a_log, d_skip,
                  ssd_norm_g, w_ssd_out, w_fft_out, w_o, norm2_g, w_ffn_in, w_ffn_out, final_g,
                  proj_tm=2048, prep_tl=1024, ssd_tl=4096, fft_tm=1024, tail_tm=512)
```

```python
import functools
import math

import jax
import jax.numpy as jnp
import numpy as np
from jax import lax
from jax.experimental import pallas as pl
from jax.experimental.pallas import tpu as pltpu

F32 = jnp.float32
BF16 = jnp.bfloat16

D_MODEL = 1024
EPS = 1e-6
GRID_W = 64

N_GROUPS = 8
HEADS_PER_GROUP = 4
HEAD_DIM = 64
N_HEADS = N_GROUPS * HEADS_PER_GROUP
D_STATE = 128
D_INNER = N_HEADS * HEAD_DIM
GROUP_W = HEADS_PER_GROUP * HEAD_DIM
GS = N_GROUPS * D_STATE
CONV_DIM = D_INNER + 2 * GS
D_CONV = 5
CHUNK = 128

FFT_GROUPS = 8
FFT_GROUP_DIM = 128
D_FF = 2816

REF_DT = CONV_DIM
REF_Z = REF_DT + 2 * N_HEADS

U_B = D_INNER
U_C = U_B + GS
U_Z = CONV_DIM
U_FFT = U_Z + D_INNER
U_GATE = U_FFT + D_MODEL
U_WIDTH = U_GATE + 2 * D_MODEL

LANE = 128
SUBLANE = 8
HALO = 16
VMEM_LIMIT = 56 * 1024 * 1024

N_DH = 2 * HEADS_PER_GROUP
R_CUM, R_NCX, R_EY, R_WX, R_DIAG = 0, 8, 16, 24, 32
ROWS_PER_GROUP = 40
N_SLAB = (GROUP_W + 2 * D_STATE) // LANE
DT_FLOOR = 1e-37
LOG2E = math.log2(math.e)
N_CUM_TERMS = 3
FWD_PER_TRIP = 4
BWD_PER_TRIP = 16
N_BUFS = 4
FWD_AHEAD = (1, 1)
BWD_AHEAD = (3,)
FWD_WEAVE = "C0 C1 P0 P1"
BWD_WEAVE = "P0 C0"
K_ARGS = 2 * N_CUM_TERMS * N_DH
FFT_PITCH = GRID_W + 4


def _dot(a, b):
    return jnp.dot(a, b, preferred_element_type=F32)


def _dot_tn(a, b):
    return lax.dot_general(a, b, (((0,), (0,)), ((), ())), preferred_element_type=F32)


def _dot_nt(a, b):
    return lax.dot_general(a, b, (((1,), (1,)), ((), ())), preferred_element_type=F32)


def _rmsnorm(x, g):
    return x * lax.rsqrt(jnp.mean(x * x, axis=-1, keepdims=True) + EPS) * g


def _mods_kernel(ct_ref, w_ref, b_ref, o_ref, *, n_vecs):
    s = ct_ref[...]
    s = s * jax.nn.sigmoid(s)
    w = w_ref[...]
    rows = [jnp.sum(w * s[:, r:r + 1], axis=0, keepdims=True) for r in range(n_vecs)]
    rows.append(jnp.zeros((SUBLANE - n_vecs, w.shape[1]), F32))
    o_ref[...] = jnp.concatenate(rows, axis=0) + b_ref[...]


def _mods(cvecs, n_vecs, w_ada, b_ada):
    n = w_ada.shape[1]
    tn = 1536
    return pl.pallas_call(
        functools.partial(_mods_kernel, n_vecs=n_vecs),
        grid=(n // tn,),
        in_specs=[pl.BlockSpec((D_MODEL, SUBLANE), lambda j: (0, 0)),
                  pl.BlockSpec((D_MODEL, tn), lambda j: (0, j)),
                  pl.BlockSpec((1, tn), lambda j: (0, j))],
        out_specs=pl.BlockSpec((SUBLANE, tn), lambda j: (0, j)),
        out_shape=jax.ShapeDtypeStruct((SUBLANE, n), F32),
        compiler_params=pltpu.CompilerParams(dimension_semantics=("arbitrary",),
                                             vmem_limit_bytes=VMEM_LIMIT),
        name="mods",
    )(cvecs.T, w_ada, b_ada.reshape(1, n))


def _inproj_kernel(x_ref, sh_ref, sc_ref, g_ref, w_ref, wdt_ref, u_ref, dt_ref, h_ref):
    @pl.when(pl.program_id(1) == 0)
    def _():
        h = _rmsnorm(x_ref[...], g_ref[...]) * (1.0 + sc_ref[0]) + sh_ref[0]
        hb = h.astype(BF16)
        h_ref[...] = hb
        hl = (h - hb.astype(F32)).astype(BF16)
        a = _dot_nt(hb, wdt_ref[...])
        dt_ref[...] = a[:, :LANE] + a[:, LANE:] + _dot_nt(hl, wdt_ref[:LANE, :])

    u_ref[...] = _dot_nt(h_ref[...], w_ref[...]).astype(BF16)


def _inproj(x2d, shift, scale, gain, w_t, w_dt_t, n_cols, tm, rows_per_mod):
    m = x2d.shape[0]
    tn = 1024
    bpm = rows_per_mod // tm

    def w_rows(i, j):
        return pl.multiple_of(j * tn + jnp.where(j * tn >= REF_DT, REF_Z - REF_DT, 0), 2 * HALO), 0

    return pl.pallas_call(
        _inproj_kernel,
        grid=(m // tm, n_cols // tn),
        in_specs=[pl.BlockSpec((tm, D_MODEL), lambda i, j: (i, 0)),
                  pl.BlockSpec((1, 1, D_MODEL), lambda i, j: (i // bpm, 0, 0)),
                  pl.BlockSpec((1, 1, D_MODEL), lambda i, j: (i // bpm, 0, 0)),
                  pl.BlockSpec((1, D_MODEL), lambda i, j: (0, 0)),
                  pl.BlockSpec((pl.Element(tn), pl.Element(D_MODEL)), w_rows),
                  pl.BlockSpec((2 * LANE, D_MODEL), lambda i, j: (0, 0))],
        out_specs=[pl.BlockSpec((tm, tn), lambda i, j: (i, j)),
                   pl.BlockSpec((tm, LANE), lambda i, j: (i, 0))],
        out_shape=[jax.ShapeDtypeStruct((m, n_cols), BF16),
                   jax.ShapeDtypeStruct((m, LANE), F32)],
        scratch_shapes=[pltpu.VMEM((tm, D_MODEL), BF16)],
        compiler_params=pltpu.CompilerParams(dimension_semantics=("arbitrary", "arbitrary"),
                                             vmem_limit_bytes=VMEM_LIMIT),
        name="inproj",
    )(x2d, shift, scale, gain, w_t, w_dt_t)


def _dtprep_kernel(raw_ref, bias_ref, alog_ref, row_ref, *, n_chunks):
    a = -jnp.exp(alog_ref[...])
    bias = bias_ref[...]
    ii = lax.broadcasted_iota(jnp.int32, (CHUNK, CHUNK), 0)
    jj = lax.broadcasted_iota(jnp.int32, (CHUNK, CHUNK), 1)
    tril = (jj <= ii).astype(BF16)
    triu = (jj >= ii).astype(BF16)
    lane = lax.broadcasted_iota(jnp.int32, (1, LANE), 1)
    is_fwd = (lane % N_DH) < HEADS_PER_GROUP

    def tri_sum(tri, terms):
        s = _dot(tri, terms)
        return s[:, :LANE] + s[:, LANE:2 * LANE] + s[:, 2 * LANE:]

    for c in range(n_chunks):
        sl = slice(c * CHUNK, (c + 1) * CHUNK)
        dt = jax.nn.softplus(raw_ref[0, sl, :] + bias)
        dta = dt * a
        terms = jnp.concatenate(_split_terms(dta, N_CUM_TERMS), axis=1).astype(BF16)
        cf = tri_sum(tril, terms)
        cb = tri_sum(triu, terms)
        cum = jnp.where(is_fwd, cf, cb)
        tot = jnp.where(is_fwd, cf[CHUNK - 1:CHUNK, :], cb[0:1, :])
        wx = dt * jnp.exp(tot - cum)
        ey = jnp.exp(cum)
        ncx = jnp.log(jnp.maximum(dt, DT_FLOOR)) - cum
        diag = jnp.log(dt + pltpu.roll(dt, LANE - HEADS_PER_GROUP, 1))
        for r_off, val in ((R_CUM, cum * LOG2E), (R_NCX, ncx * LOG2E), (R_EY, ey), (R_WX, wx),
                           (R_DIAG, diag * LOG2E)):
            vt = val.T
            for g in range(N_GROUPS):
                r0 = g * ROWS_PER_GROUP + r_off
                row_ref[0, r0:r0 + N_DH, sl] = vt[g * N_DH:(g + 1) * N_DH, :]


def _dtprep(raw, bias, alog, tl):
    b, l, _ = raw.shape
    return pl.pallas_call(
        functools.partial(_dtprep_kernel, n_chunks=tl // CHUNK),
        grid=(b, l // tl),
        in_specs=[pl.BlockSpec((1, tl, LANE), lambda i, j: (i, j, 0)),
                  pl.BlockSpec((1, LANE), lambda i, j: (0, 0)),
                  pl.BlockSpec((1, LANE), lambda i, j: (0, 0))],
        out_specs=pl.BlockSpec((1, N_GROUPS * ROWS_PER_GROUP, tl), lambda i, j: (i, 0, j)),
        out_shape=jax.ShapeDtypeStruct((b, N_GROUPS * ROWS_PER_GROUP, l), F32),
        compiler_params=pltpu.CompilerParams(dimension_semantics=("arbitrary", "arbitrary"),
                                             vmem_limit_bytes=VMEM_LIMIT),
        name="dtprep",
    )(raw, bias, alog)


def _split_terms(x, n):
    terms = []
    for _ in range(n - 1):
        t = x.astype(BF16).astype(F32)
        terms.append(t)
        x = x - t
    terms.append(x)
    return terms


def _scale_rows(rows):
    return jnp.concatenate(_split_terms(rows[R_EY:R_EY + N_DH], 2) + _split_terms(rows[R_WX:R_WX + N_DH], 2),
                           axis=0).astype(BF16)


def _conv_silu(ext_refs, s, p0, cw, cbias):
    sl = slice(s * LANE, (s + 1) * LANE)
    acc = cbias[:, sl].astype(BF16)
    for k in range(D_CONV):
        pair = p0 + (HALO + k - D_CONV // 2 - k % 2) // 2
        tap = pltpu.bitcast(ext_refs[k % 2][s, pl.ds(pair, CHUNK // 2), :], BF16)
        acc = acc + cw[k:k + 1, sl].astype(BF16) * tap
    return acc * jax.nn.sigmoid(acc)


def _ssd_kernel(x_ref, b_ref, c_ref, xp_ref, bp_ref, cp_ref, xn_ref, bn_ref, cn_ref, z_ref, row_ref,
                cx_ref, cb_ref, crow_ref, cwx_ref, cwb_ref, cwc_ref, cbx_ref, cbb_ref, cbc_ref,
                dskip_ref, gain_ref, onesel_ref, sel_ref, out_ref,
                cache_ref, yacc_ref, sf_ref, sb_ref, ext0_ref, ext1_ref, *buf_refs, tl, n_blk, ctx_len):
    p = pl.program_id(2)
    i = pl.program_id(3)
    n_ch = tl // CHUNK
    args_bufs = buf_refs[0:N_BUFS]
    ew_bufs = buf_refs[N_BUFS:2 * N_BUFS]
    st_bufs = buf_refs[2 * N_BUFS:3 * N_BUFS]
    ext_refs = (ext0_ref, ext1_ref)
    cw = jnp.concatenate([cwx_ref[...], cwb_ref[...], cwc_ref[...]], axis=1)
    cbias = jnp.concatenate([cbx_ref[...], cbb_ref[...], cbc_ref[...]], axis=1)

    def slabs_of(x, bm, cm):
        return [x[:, :LANE], x[:, LANE:], bm, cm]

    def slab_refs(xr, br, cr):
        return [(xr, 0), (xr, LANE), (br, 0), (cr, 0)]

    def fill_ext(src, first, last, n_rows):
        half = HALO // 2
        n_piece = n_rows // CHUNK

        def copy_piece(k, carry):
            r = pl.multiple_of(k * CHUNK, CHUNK)
            q = pl.multiple_of(k * (CHUNK // 2), CHUNK // 2)
            for s, (ref, lane0) in enumerate(src):
                ext0_ref[s, pl.ds(half + q, CHUNK // 2), :] = pltpu.bitcast(
                    ref[0, pl.ds(r, CHUNK), lane0:lane0 + LANE], jnp.uint32)
            return carry

        def shift_piece(k, carry):
            q = pl.multiple_of(k * (CHUNK // 2), CHUNK // 2)
            for s in range(len(src)):
                lo = ext0_ref[s, pl.ds(q, CHUNK // 2), :]
                hi = ext0_ref[s, pl.ds(q + 1, CHUNK // 2), :]
                ext1_ref[s, pl.ds(q, CHUNK // 2), :] = (lo >> 16) | (hi << 16)
            return carry

        for s in range(len(src)):
            ext0_ref[s, 0:half, :] = pltpu.bitcast(first[s], jnp.uint32)
            ext0_ref[s, half + n_rows // 2:2 * half + n_rows // 2, :] = pltpu.bitcast(last[s], jnp.uint32)
        lax.fori_loop(0, n_piece, copy_piece, 0, unroll=2)
        lax.fori_loop(0, n_piece, shift_piece, 0, unroll=2)
        for s in range(len(src)):
            q = n_rows // 2
            ext1_ref[s, q:q + half, :] = (ext0_ref[s, q:q + half, :] >> 16) | (ext0_ref[s, q + 1:q + half + 1, :] << 16)

    def row_of(c):
        return c * CHUNK if isinstance(c, int) else pl.multiple_of(c * CHUNK, CHUNK)

    def expand(rowt, d):
        return _dot_tn(_scale_rows(rowt), sel_ref[d])

    def run_pipelined(prepare_parts, consume_parts, weave, order, per_trip, aheads):
        def step(k, d, in_loop):
            handed = None
            for item in weave.split():
                n = int(item[1:])
                if item[0] == "P":
                    if in_loop or k + aheads[n] < n_ch:
                        prepare_parts[n](order(k + aheads[n]), (d + aheads[n]) % N_BUFS)
                else:
                    handed = consume_parts[n](order(k), d % N_BUFS, handed)

        for n, part in enumerate(prepare_parts):
            for a in range(aheads[n]):
                part(order(a), a)

        def trip(t, carry):
            for d in range(per_trip):
                step(per_trip * t + d, d, True)
            return carry

        lax.fori_loop(0, n_ch // per_trip - 1, trip, 0)
        for d in range(per_trip):
            step(n_ch - per_trip + d, d, False)

    @pl.when(jnp.logical_and(p == 0, i == 0))
    def _init_from_context():
        zpad = [jnp.zeros((HALO, LANE), BF16)] * (N_SLAB - 1)
        fill_ext(slab_refs(cx_ref, cb_ref, None)[:N_SLAB - 1], zpad, zpad, ctx_len)
        n_cc = ctx_len // CHUNK
        sf = jnp.zeros((D_STATE, GROUP_W), F32)
        st_b, ey_b = [], []
        for cc in range(n_cc):
            slabs = [_conv_silu(ext_refs, s, cc * CHUNK // 2, cw, cbias) for s in range(N_SLAB - 1)]
            xs = jnp.concatenate(slabs[:2], axis=1).astype(F32)
            bm = slabs[2]
            rowt = crow_ref[0, :, cc * CHUNK:(cc + 1) * CHUNK]
            ewf = expand(rowt, 0)
            ewb = expand(rowt, 1)
            sf = sf * ewf[CHUNK - 1:CHUNK, :GROUP_W] + _dot_tn(bm, (xs * ewf[:, GROUP_W:]).astype(BF16))
            st_b.append(_dot_tn(bm, (xs * ewb[:, GROUP_W:]).astype(BF16)))
            ey_b.append(ewb[0:1, :GROUP_W])
        sb = jnp.zeros((D_STATE, GROUP_W), F32)
        for cc in reversed(range(n_cc)):
            sb = sb * ey_b[cc] + st_b[cc]
        sf_ref[...] = sf
        sb_ref[...] = sb

    @pl.when(p == 0)
    def _forward():
        blk = i
        base = blk * tl
        has_prev = blk > 0
        has_next = blk < n_blk - 1
        fill_ext(slab_refs(x_ref, b_ref, c_ref),
                 [jnp.where(has_prev, v, jnp.zeros_like(v)) for v in slabs_of(xp_ref[0], bp_ref[0], cp_ref[0])],
                 [jnp.where(has_next, v, jnp.zeros_like(v)) for v in slabs_of(xn_ref[0], bn_ref[0], cn_ref[0])],
                 tl)
        ii = lax.broadcasted_iota(jnp.int32, (CHUNK, CHUNK), 0)
        jj = lax.broadcasted_iota(jnp.int32, (CHUNK, CHUNK), 1)
        lower = jj < ii
        upper = jj > ii

        def prepare_conv(c, buf):
            r = row_of(c)
            p0 = c * (CHUNK // 2) if isinstance(c, int) else pl.multiple_of(c * (CHUNK // 2), CHUNK // 2)
            slabs = [_conv_silu(ext_refs, s, p0, cw, cbias) for s in range(N_SLAB)]
            for s in range(N_SLAB):
                cache_ref[pl.ds(base + r, CHUNK), s * LANE:(s + 1) * LANE] = slabs[s]
            yacc_ref[pl.ds(base + r, CHUNK), :] = jnp.concatenate(slabs[:2], axis=1).astype(F32) * dskip_ref[...]

        def prepare_scales(c, buf):
            rowt = row_ref[0, :, pl.ds(row_of(c), CHUNK)]
            onesel = onesel_ref[...]
            ones = jnp.ones((N_DH, CHUNK), F32)
            at = jnp.concatenate(_split_terms(rowt[R_CUM:R_CUM + N_DH], N_CUM_TERMS) + [ones] * N_CUM_TERMS,
                                 axis=0).astype(BF16)
            wide = [jnp.concatenate([t] * N_DH, axis=1) * onesel
                    for t in _split_terms(rowt[R_NCX:R_NCX + N_DH], N_CUM_TERMS)]
            bmat = jnp.concatenate([onesel] * N_CUM_TERMS + wide, axis=0).astype(BF16)
            args_bufs[buf][...] = _dot_tn(at, bmat)
            ew_bufs[buf][...] = expand(rowt, 0)

        def consume_state(c, buf, _):
            r = row_of(c)
            xbc_b = cache_ref[pl.ds(base + r, CHUNK), :]
            xs_b = xbc_b[:, :GROUP_W]
            bm = xbc_b[:, GROUP_W:GROUP_W + D_STATE]
            cm = xbc_b[:, GROUP_W + D_STATE:]
            return r, xs_b, bm, cm, _dot_nt(cm, bm)

        def consume_decay(c, buf, handed):
            r, xs_b, bm, cm, cbm = handed
            yd = []
            for j in range(HEADS_PER_GROUP):
                argf = args_bufs[buf][:, j * CHUNK:(j + 1) * CHUNK]
                argb = args_bufs[buf][:, (HEADS_PER_GROUP + j) * CHUNK:(HEADS_PER_GROUP + j + 1) * CHUNK]
                diag = row_ref[0, R_DIAG + j:R_DIAG + j + 1, pl.ds(r, CHUNK)]
                arg = jnp.where(lower, argf, jnp.where(upper, argb, diag))
                wmat = (cbm * jnp.exp2(arg)).astype(BF16)
                yd.append(_dot(wmat, xs_b[:, j * HEAD_DIM:(j + 1) * HEAD_DIM]))
            ef = ew_bufs[buf][:, :GROUP_W]
            wxf = ew_bufs[buf][:, GROUP_W:]
            sf = sf_ref[...]
            yacc_ref[pl.ds(base + r, CHUNK), :] += jnp.concatenate(yd, axis=1) + _dot(cm, sf.astype(BF16)) * ef
            sf_ref[...] = sf * ef[CHUNK - 1:CHUNK, :] + _dot_tn(bm, xs_b * wxf.astype(BF16))

        run_pipelined([prepare_conv, prepare_scales], [consume_state, consume_decay], FWD_WEAVE,
                      lambda k: k, FWD_PER_TRIP, FWD_AHEAD)

    @pl.when(p == 1)
    def _backward():
        blk = n_blk - 1 - i
        base = blk * tl

        def prepare(c, buf):
            r = row_of(c)
            ew = expand(row_ref[0, :, pl.ds(r, CHUNK)], 1)
            ew_bufs[buf][...] = ew
            xs_b = cache_ref[pl.ds(base + r, CHUNK), 0:GROUP_W]
            bm = cache_ref[pl.ds(base + r, CHUNK), GROUP_W:GROUP_W + D_STATE]
            st_bufs[buf][...] = _dot_tn(bm, xs_b * ew[:, GROUP_W:].astype(BF16))

        def consume(c, buf, _):
            r = row_of(c)
            cm = cache_ref[pl.ds(base + r, CHUNK), GROUP_W + D_STATE:]
            eb = ew_bufs[buf][:, :GROUP_W]
            sb = sb_ref[...]
            y = yacc_ref[pl.ds(base + r, CHUNK), :] + _dot(cm, sb.astype(BF16)) * eb
            sb_ref[...] = sb * eb[0:1, :] + st_bufs[buf][...]
            z = z_ref[0, pl.ds(r, CHUNK), :]
            y = y * (z * jax.nn.sigmoid(z)).astype(F32)
            out_ref[0, pl.ds(r, CHUNK), :] = _rmsnorm(y, gain_ref[...]).astype(BF16)

        run_pipelined([prepare], [consume], BWD_WEAVE, lambda k: n_ch - 1 - k, BWD_PER_TRIP, BWD_AHEAD)


def _ssd_selectors():
    onesel = (np.arange(N_DH * CHUNK)[None, :] // CHUNK == np.arange(N_DH)[:, None]).astype(np.float32)
    sel = np.zeros((2, 4 * N_DH, 2 * GROUP_W), np.float32)
    for d in range(2):
        for j in range(HEADS_PER_GROUP):
            dh = d * HEADS_PER_GROUP + j
            for term in range(2):
                sel[d, term * N_DH + dh, j * HEAD_DIM:(j + 1) * HEAD_DIM] = 1.0
                sel[d, (2 + term) * N_DH + dh, GROUP_W + j * HEAD_DIM:GROUP_W + (j + 1) * HEAD_DIM] = 1.0
    return jnp.asarray(onesel, F32), jnp.asarray(sel, BF16)


def _ssd(u, rows, cu, crows, conv_w, conv_b, dskip, gain, tl):
    b, l, _ = u.shape
    ctx_len = cu.shape[1]
    assert ctx_len <= tl
    n_blk = l // tl
    hb = tl // HALO
    last_halo = l // HALO - 1
    assert (tl // CHUNK) % FWD_PER_TRIP == 0 and (tl // CHUNK) % BWD_PER_TRIP == 0
    onesel, sel = _ssd_selectors()
    xg, bg, cg = 0, U_B // D_STATE, U_C // D_STATE

    def in_blk(p, i):
        return i * (1 - p) + (n_blk - 1) * p

    def seq_blk(p, i):
        return i * (1 - p) + (n_blk - 1 - i) * p

    def out_blk(p, i):
        return (n_blk - 1) * (1 - p) + (n_blk - 1 - i) * p

    def prev_halo(p, i):
        return jnp.maximum(in_blk(p, i) * hb - 1, 0)

    def next_halo(p, i):
        return jnp.minimum((in_blk(p, i) + 1) * hb, last_halo)

    def xbc_specs(rows_blk, row_index):
        return [pl.BlockSpec((1, rows_blk, GROUP_W), lambda bb, g, p, i: (bb, row_index(p, i), xg + g)),
                pl.BlockSpec((1, rows_blk, D_STATE), lambda bb, g, p, i: (bb, row_index(p, i), bg + g)),
                pl.BlockSpec((1, rows_blk, D_STATE), lambda bb, g, p, i: (bb, row_index(p, i), cg + g))]

    def param_specs(n_rows):
        return [pl.BlockSpec((n_rows, GROUP_W), lambda bb, g, p, i: (0, xg + g)),
                pl.BlockSpec((n_rows, D_STATE), lambda bb, g, p, i: (0, bg + g)),
                pl.BlockSpec((n_rows, D_STATE), lambda bb, g, p, i: (0, cg + g))]

    grid = (b, N_GROUPS, 2, n_blk)
    in_specs = (
        xbc_specs(tl, in_blk) + xbc_specs(HALO, prev_halo) + xbc_specs(HALO, next_halo)
        + [pl.BlockSpec((1, tl, GROUP_W), lambda bb, g, p, i: (bb, out_blk(p, i), U_Z // GROUP_W + g)),
           pl.BlockSpec((1, ROWS_PER_GROUP, tl), lambda bb, g, p, i: (bb, g, seq_blk(p, i)))]
        + xbc_specs(ctx_len, lambda p, i: 0)[:2]
        + [pl.BlockSpec((1, ROWS_PER_GROUP, ctx_len), lambda bb, g, p, i: (bb, g, 0))]
        + param_specs(SUBLANE) + param_specs(1)
        + [pl.BlockSpec((1, GROUP_W), lambda bb, g, p, i: (0, g)),
           pl.BlockSpec((1, GROUP_W), lambda bb, g, p, i: (0, g)),
           pl.BlockSpec(onesel.shape, lambda bb, g, p, i: (0, 0)),
           pl.BlockSpec(sel.shape, lambda bb, g, p, i: (0, 0, 0))])
    return pl.pallas_call(
        functools.partial(_ssd_kernel, tl=tl, n_blk=n_blk, ctx_len=ctx_len),
        grid=grid,
        in_specs=in_specs,
        out_specs=pl.BlockSpec((1, tl, GROUP_W), lambda bb, g, p, i: (bb, out_blk(p, i), g)),
        out_shape=jax.ShapeDtypeStruct((b, l, D_INNER), BF16),
        scratch_shapes=[pltpu.VMEM((l, N_SLAB * LANE), BF16),
                        pltpu.VMEM((l, GROUP_W), F32),
                        pltpu.VMEM((D_STATE, GROUP_W), F32),
                        pltpu.VMEM((D_STATE, GROUP_W), F32),
                        pltpu.VMEM((N_SLAB, (tl + 2 * HALO) // 2, LANE), jnp.uint32),
                        pltpu.VMEM((N_SLAB, (tl + 2 * HALO) // 2, LANE), jnp.uint32)]
        + [pltpu.VMEM((CHUNK, N_DH * CHUNK), F32)] * N_BUFS
        + [pltpu.VMEM((CHUNK, 2 * GROUP_W), F32)] * N_BUFS
        + [pltpu.VMEM((D_STATE, GROUP_W), F32)] * N_BUFS,
        compiler_params=pltpu.CompilerParams(
            dimension_semantics=("arbitrary", "arbitrary", "arbitrary", "arbitrary"),
            vmem_limit_bytes=VMEM_LIMIT),
        name="ssd",
    )(u, u, u, u, u, u, u, u, u, u, rows, cu, cu, crows,
      conv_w, conv_w, conv_w, conv_b, conv_b, conv_b, dskip, gain, onesel, sel)


def _dft_mats():
    def cs(n):
        k = np.arange(n)
        ang = 2.0 * np.pi * np.outer(k, k) / n
        return np.cos(ang), np.sin(ang)

    c3, s3 = cs(FFT_GROUP_DIM)
    w1 = np.concatenate([c3, -s3], axis=1)
    c2, s2 = cs(GRID_W)
    m2 = np.block([[c2, s2], [-s2, c2]])
    return w1, m2


def _fft12_kernel(f_ref, w1_ref, m2_ref, o_ref, *, tm):
    f = f_ref[0]
    w1 = w1_ref[...]
    ps, qs = [], []
    for g in range(FFT_GROUPS):
        pq = _dot(f[:, g * FFT_GROUP_DIM:(g + 1) * FFT_GROUP_DIM], w1)
        ps.append(pq[:, :FFT_GROUP_DIM])
        qs.append(pq[:, FFT_GROUP_DIM:])
    pr = jnp.concatenate(ps, axis=1).astype(BF16)
    qr = jnp.concatenate(qs, axis=1).astype(BF16)
    m2 = m2_ref[...]
    pad = jnp.zeros((FFT_PITCH - GRID_W, D_MODEL), jnp.uint32)
    for r in range(tm // GRID_W):
        sl = slice(r * GRID_W, (r + 1) * GRID_W)
        o = _dot(m2, jnp.concatenate([pr[sl], qr[sl]], axis=0))
        r0 = r * FFT_PITCH
        o_ref[0, r0:r0 + GRID_W, :] = _pack_complex(o[:GRID_W], o[GRID_W:])
        o_ref[0, r0 + GRID_W:r0 + FFT_PITCH, :] = pad


def _pack_complex(re, im):
    re_bits = pltpu.bitcast(re.astype(BF16).astype(F32), jnp.uint32)
    im_bits = pltpu.bitcast(im.astype(BF16).astype(F32), jnp.uint32)
    return (re_bits >> 16) | (im_bits & jnp.uint32(0xFFFF0000))


def _unpack_complex(word):
    re = pltpu.bitcast(word << 16, F32)
    im = pltpu.bitcast(word & jnp.uint32(0xFFFF0000), F32)
    return re, im


def _fft3_kernel(ri_ref, m3_ref, o_ref, scr_ref, *, rows):
    m3 = m3_ref[...]

    def column_pair(t, carry):
        w = 2 * t
        cols = [jnp.concatenate(_unpack_complex(ri_ref[0, pl.ds(w + dw, rows, stride=FFT_PITCH), :]), axis=0)
                for dw in range(2)]
        o = _dot(m3, jnp.concatenate(cols, axis=1).astype(BF16))
        for dw in range(2):
            scr_ref[pl.ds(w + dw, rows, stride=FFT_PITCH), :] = o[:, dw * LANE:(dw + 1) * LANE]
        return carry

    lax.fori_loop(0, GRID_W // 2, column_pair, 0, unroll=True)
    for k in range(rows):
        o_ref[0, k * GRID_W:(k + 1) * GRID_W, :] = scr_ref[k * FFT_PITCH:k * FFT_PITCH + GRID_W, :].astype(BF16)


def _fourier(u, tm):
    b, l, _ = u.shape
    rows = l // GRID_W
    w1, m2 = _dft_mats()
    ri = pl.pallas_call(
        functools.partial(_fft12_kernel, tm=tm),
        grid=(b, l // tm),
        in_specs=[pl.BlockSpec((1, tm, D_MODEL), lambda i, j: (i, j, U_FFT // D_MODEL)),
                  pl.BlockSpec((FFT_GROUP_DIM, 2 * FFT_GROUP_DIM), lambda i, j: (0, 0)),
                  pl.BlockSpec((2 * GRID_W, 2 * GRID_W), lambda i, j: (0, 0))],
        out_specs=pl.BlockSpec((1, tm // GRID_W * FFT_PITCH, D_MODEL), lambda i, j: (i, j, 0)),
        out_shape=jax.ShapeDtypeStruct((b, rows * FFT_PITCH, D_MODEL), jnp.uint32),
        compiler_params=pltpu.CompilerParams(dimension_semantics=("arbitrary", "arbitrary"),
                                             vmem_limit_bytes=VMEM_LIMIT),
        name="fft12",
    )(u, jnp.asarray(w1, BF16), jnp.asarray(m2, BF16))

    k = np.arange(rows)
    ang = 2.0 * np.pi * np.outer(k, k) / rows
    scale = 1.0 / np.sqrt(float(rows * GRID_W * FFT_GROUP_DIM))
    m3 = np.concatenate([np.cos(ang), np.sin(ang)], axis=1) * scale
    return pl.pallas_call(
        functools.partial(_fft3_kernel, rows=rows),
        grid=(b, D_MODEL // LANE),
        in_specs=[pl.BlockSpec((1, rows * FFT_PITCH, LANE), lambda i, j: (i, 0, j)),
                  pl.BlockSpec((rows, 2 * rows), lambda i, j: (0, 0))],
        out_specs=pl.BlockSpec((1, l, LANE), lambda i, j: (i, 0, j)),
        out_shape=jax.ShapeDtypeStruct((b, l, D_MODEL), BF16),
        scratch_shapes=[pltpu.VMEM((rows * FFT_PITCH, LANE), F32)],
        compiler_params=pltpu.CompilerParams(dimension_semantics=("arbitrary", "arbitrary"),
                                             vmem_limit_bytes=VMEM_LIMIT),
        name="fft3",
    )(ri, jnp.asarray(m3, BF16))


MXU_WIDTH = 256
FF_SLICES = ((0, 6 * MXU_WIDTH), (6 * MXU_WIDTH, D_FF))


def _tail_kernel(y_ref, f_ref, ga_ref, gb_ref, x_ref, g1_ref, sh2_ref, sc2_ref, g2_ref,
                 wso_ref, wfo_ref, wo_ref, n2_ref, wff_ref, wfd_ref, fg_ref, o_ref):
    a = _dot(y_ref[...], wso_ref[...])
    bq = _dot(f_ref[...], wfo_ref[...])
    merged = jax.nn.sigmoid(ga_ref[...].astype(F32)) * a + jax.nn.sigmoid(gb_ref[...].astype(F32)) * bq
    xn = x_ref[...] + g1_ref[0] * _dot(merged.astype(BF16), wo_ref[...])
    h = (_rmsnorm(xn, n2_ref[...]) * (1.0 + sc2_ref[0]) + sh2_ref[0]).astype(BF16)
    acc = None
    for lo, hi in FF_SLICES:
        gate = _dot(h, wff_ref[:, lo:hi])
        up = _dot(h, wff_ref[:, D_FF + lo:D_FF + hi])
        act = (gate * jax.nn.sigmoid(gate) * up).astype(BF16)
        part = _dot(act, wfd_ref[lo:hi, :])
        acc = part if acc is None else acc + part
    o_ref[...] = _rmsnorm(xn + g2_ref[0] * acc, fg_ref[...])


def _tail(y2d, f2d, u2d, x2d, g1, sh2, sc2, g2, wso, wfo, wo, n2, wff, wfd, fg, tm, rows_per_mod):
    m = x2d.shape[0]
    bpm = rows_per_mod // tm

    def const(shape):
        return pl.BlockSpec(shape, lambda i: (0, 0), pipeline_mode=pl.Buffered(1))

    def mod():
        return pl.BlockSpec((1, 1, D_MODEL), lambda i: (i // bpm, 0, 0))

    return pl.pallas_call(
        _tail_kernel,
        grid=(m // tm,),
        in_specs=[pl.BlockSpec((tm, D_INNER), lambda i: (i, 0)),
                  pl.BlockSpec((tm, D_MODEL), lambda i: (i, 0)),
                  pl.BlockSpec((tm, D_MODEL), lambda i: (i, U_GATE // D_MODEL)),
                  pl.BlockSpec((tm, D_MODEL), lambda i: (i, U_GATE // D_MODEL + 1)),
                  pl.BlockSpec((tm, D_MODEL), lambda i: (i, 0)),
                  mod(), mod(), mod(), mod(),
                  const((D_INNER, D_MODEL)), const((D_MODEL, D_MODEL)), const((D_MODEL, D_MODEL)),
                  const((1, D_MODEL)),
                  const((D_MODEL, 2 * D_FF)), const((D_FF, D_MODEL)),
                  const((1, D_MODEL))],
        out_specs=pl.BlockSpec((tm, D_MODEL), lambda i: (i, 0)),
        out_shape=jax.ShapeDtypeStruct((m, D_MODEL), F32),
        compiler_params=pltpu.CompilerParams(dimension_semantics=("arbitrary",),
                                             vmem_limit_bytes=VMEM_LIMIT),
        name="tail",
    )(y2d, f2d, u2d, u2d, x2d, g1, sh2, sc2, g2, wso, wfo, wo, n2, wff, wfd, fg)


def _dt_lanes(a):
    lead = a.shape[:-1]
    a = a.reshape(lead + (2, N_GROUPS, HEADS_PER_GROUP))
    a = jnp.swapaxes(a, -3, -2).reshape(lead + (2 * N_HEADS,))
    return jnp.concatenate([a, jnp.zeros(lead + (LANE - 2 * N_HEADS,), a.dtype)], axis=-1)


def _block(x, c, ctx, c_ctx, w_ada, b_ada, norm1_g, w_in, conv_w, conv_b, dt_bias, a_log, d_skip,
           ssd_norm_g, w_ssd_out, w_fft_out, w_o, norm2_g, w_ffn_in, w_ffn_out, final_g,
           proj_tm, prep_tl, ssd_tl, fft_tm, tail_tm):
    b, l, d = x.shape
    ctx_len = ctx.shape[1]
    assert d == D_MODEL and l % (GRID_W * 2) == 0 and ctx_len % CHUNK == 0 and b + 1 <= SUBLANE

    w_in_t = w_in[0].T
    w_main_t = w_in_t.astype(BF16)
    w_dt = _dt_lanes(w_in_t[REF_DT:REF_Z].T).T
    w_dt_hi = w_dt.astype(BF16)
    w_dt_lo = (w_dt - w_dt_hi.astype(F32)).astype(BF16)
    w_dt2 = jnp.concatenate([w_dt_hi, w_dt_lo], axis=0)
    conv_w_p = jnp.concatenate([conv_w[0], jnp.zeros((SUBLANE - D_CONV, CONV_DIM), F32)], axis=0)
    conv_b_p = conv_b[0].reshape(1, CONV_DIM)
    bias_v = _dt_lanes(dt_bias[0].reshape(-1)).reshape(1, LANE)
    alog_v = _dt_lanes(a_log[0].reshape(-1)).reshape(1, LANE)
    dskip_v = jnp.repeat(d_skip[0], HEAD_DIM).reshape(1, D_INNER)
    gain_v = ssd_norm_g[0].reshape(1, D_INNER)
    wso = w_ssd_out[0].astype(BF16)
    wfo = w_fft_out[0].astype(BF16)
    wo = w_o[0].astype(BF16)
    wff = w_ffn_in[0].astype(BF16)
    wfd = w_ffn_out[0].astype(BF16)

    cvecs = jnp.concatenate([c, c_ctx[None, :], jnp.zeros((SUBLANE - b - 1, D_MODEL), F32)], axis=0)
    mods = _mods(cvecs, b + 1, w_ada[0], b_ada[0])
    lat = mods[:b].reshape(b, 6, 1, D_MODEL)
    sh1, sc1, g1, sh2, sc2, g2 = (lat[:, k] for k in range(6))
    cm = mods[b].reshape(6, 1, 1, D_MODEL)

    x2d = x.reshape(b * l, D_MODEL)
    n1 = norm1_g[0].reshape(1, D_MODEL)
    u2d, dt_raw = _inproj(x2d, sh1, sc1, n1, w_main_t, w_dt2, U_WIDTH, proj_tm, l)
    u = u2d.reshape(b, l, U_WIDTH)
    cu2d, cdt_raw = _inproj(ctx.reshape(b * ctx_len, D_MODEL), cm[0], cm[1], n1, w_main_t, w_dt2,
                            CONV_DIM, b * ctx_len, b * ctx_len)
    cu = cu2d.reshape(b, ctx_len, CONV_DIM)

    rows = _dtprep(dt_raw.reshape(b, l, LANE), bias_v, alog_v, prep_tl)
    crows = _dtprep(cdt_raw.reshape(b, ctx_len, LANE), bias_v, alog_v, ctx_len)
    y_ssd = _ssd(u, rows, cu, crows, conv_w_p, conv_b_p, dskip_v, gain_v, ssd_tl)

    f_mix = _fourier(u, fft_tm)

    out = _tail(y_ssd.reshape(b * l, D_INNER), f_mix.reshape(b * l, D_MODEL), u2d, x2d,
                g1, sh2, sc2, g2, wso, wfo, wo, norm2_g[0].reshape(1, D_MODEL), wff, wfd,
                final_g.reshape(1, D_MODEL), tail_tm, l)
    return out.reshape(b, l, D_MODEL)


def kernel(x, c, ctx, c_ctx, w_ada, b_ada, norm1_g, w_in, conv_w, conv_b, dt_bias, a_log, d_skip,
           ssd_norm_g, w_ssd_out, w_fft_out, w_o, norm2_g, w_ffn_in, w_ffn_out, final_g):
    return _block(x, c, ctx, c_ctx, w_ada, b_ada, norm1_g, w_in, conv_w, conv_b, dt_bias, a_log, d_skip,
                  ssd_norm_g, w_ssd_out, w_fft_out, w_o, norm2_g, w_ffn_in, w_ffn_out, final_g,
                  proj_tm=2048, prep_tl=1024, ssd_tl=4096, fft_tm=1024, tail_tm=512)
```

```python
import functools
import math

import jax
import jax.numpy as jnp
import numpy as np
from jax import lax
from jax.experimental import pallas as pl
from jax.experimental.pallas import tpu as pltpu

F32 = jnp.float32
BF16 = jnp.bfloat16

D_MODEL = 1024
EPS = 1e-6
GRID_W = 64

N_GROUPS = 8
HEADS_PER_GROUP = 4
HEAD_DIM = 64
N_HEADS = N_GROUPS * HEADS_PER_GROUP
D_STATE = 128
D_INNER = N_HEADS * HEAD_DIM
GROUP_W = HEADS_PER_GROUP * HEAD_DIM
GS = N_GROUPS * D_STATE
CONV_DIM = D_INNER + 2 * GS
D_CONV = 5
CHUNK = 128

FFT_GROUPS = 8
FFT_GROUP_DIM = 128
D_FF = 2816

REF_DT = CONV_DIM
REF_Z = REF_DT + 2 * N_HEADS

U_B = D_INNER
U_C = U_B + GS
U_Z = CONV_DIM
U_FFT = U_Z + D_INNER
U_GATE = U_FFT + D_MODEL
U_WIDTH = U_GATE + 2 * D_MODEL

LANE = 128
SUBLANE = 8
HALO = 16
VMEM_LIMIT = 56 * 1024 * 1024

N_DH = 2 * HEADS_PER_GROUP
R_CUM, R_NCX, R_EY, R_WX, R_DIAG = 0, 8, 16, 24, 32
ROWS_PER_GROUP = 40
N_SLAB = (GROUP_W + 2 * D_STATE) // LANE
DT_FLOOR = 1e-37
LOG2E = math.log2(math.e)
N_CUM_TERMS = 3
FWD_PER_TRIP = 4
BWD_PER_TRIP = 16
N_BUFS = 4
FWD_AHEAD = (1, 1)
BWD_AHEAD = (3,)
FWD_WEAVE = "C0 C1 P0 P1"
BWD_WEAVE = "P0 C0"
K_ARGS = 2 * N_CUM_TERMS * N_DH
FFT_PITCH = GRID_W + 4


def _dot(a, b):
    return jnp.dot(a, b, preferred_element_type=F32)


def _dot_tn(a, b):
    return lax.dot_general(a, b, (((0,), (0,)), ((), ())), preferred_element_type=F32)


def _dot_nt(a, b):
    return lax.dot_general(a, b, (((1,), (1,)), ((), ())), preferred_element_type=F32)


def _rmsnorm(x, g):
    return x * lax.rsqrt(jnp.mean(x * x, axis=-1, keepdims=True) + EPS) * g


def _mods_kernel(ct_ref, w_ref, b_ref, o_ref, *, n_vecs):
    s = ct_ref[...]
    s = s * jax.nn.sigmoid(s)
    w = w_ref[...]
    rows = [jnp.sum(w * s[:, r:r + 1], axis=0, keepdims=True) for r in range(n_vecs)]
    rows.append(jnp.zeros((SUBLANE - n_vecs, w.shape[1]), F32))
    o_ref[...] = jnp.concatenate(rows, axis=0) + b_ref[...]


def _mods(cvecs, n_vecs, w_ada, b_ada):
    n = w_ada.shape[1]
    tn = 1536
    return pl.pallas_call(
        functools.partial(_mods_kernel, n_vecs=n_vecs),
        grid=(n // tn,),
        in_specs=[pl.BlockSpec((D_MODEL, SUBLANE), lambda j: (0, 0)),
                  pl.BlockSpec((D_MODEL, tn), lambda j: (0, j)),
                  pl.BlockSpec((1, tn), lambda j: (0, j))],
        out_specs=pl.BlockSpec((SUBLANE, tn), lambda j: (0, j)),
        out_shape=jax.ShapeDtypeStruct((SUBLANE, n), F32),
        compiler_params=pltpu.CompilerParams(dimension_semantics=("arbitrary",),
                                             vmem_limit_bytes=VMEM_LIMIT),
        name="mods",
    )(cvecs.T, w_ada, b_ada.reshape(1, n))


PROLOGUE_ROWS = 512


def _inproj_kernel(x_ref, sh_ref, sc_ref, g_ref, w_ref, wdt_ref, u_ref, dt_ref, h_ref):
    @pl.when(pl.program_id(1) == 0)
    def _():
        piece = min(PROLOGUE_ROWS, x_ref.shape[0])
        for r0 in range(0, x_ref.shape[0], piece):
            rows = slice(r0, r0 + piece)
            h = _rmsnorm(x_ref[rows, :], g_ref[...]) * (1.0 + sc_ref[0]) + sh_ref[0]
            hb = h.astype(BF16)
            h_ref[rows, :] = hb
            hl = (h - hb.astype(F32)).astype(BF16)
            a = _dot_nt(hb, wdt_ref[...])
            dt_ref[rows, :] = a[:, :LANE] + a[:, LANE:] + _dot_nt(hl, wdt_ref[:LANE, :])

    u_ref[...] = _dot_nt(h_ref[...], w_ref[...]).astype(BF16)


def _inproj(x2d, shift, scale, gain, w_t, w_dt_t, n_cols, tm, rows_per_mod):
    m = x2d.shape[0]
    tn = 1024
    bpm = rows_per_mod // tm

    def w_rows(i, j):
        return pl.multiple_of(j * tn + jnp.where(j * tn >= REF_DT, REF_Z - REF_DT, 0), 2 * HALO), 0

    return pl.pallas_call(
        _inproj_kernel,
        grid=(m // tm, n_cols // tn),
        in_specs=[pl.BlockSpec((tm, D_MODEL), lambda i, j: (i, 0)),
                  pl.BlockSpec((1, 1, D_MODEL), lambda i, j: (i // bpm, 0, 0)),
                  pl.BlockSpec((1, 1, D_MODEL), lambda i, j: (i // bpm, 0, 0)),
                  pl.BlockSpec((1, D_MODEL), lambda i, j: (0, 0)),
                  pl.BlockSpec((pl.Element(tn), pl.Element(D_MODEL)), w_rows),
                  pl.BlockSpec((2 * LANE, D_MODEL), lambda i, j: (0, 0))],
        out_specs=[pl.BlockSpec((tm, tn), lambda i, j: (i, j)),
                   pl.BlockSpec((tm, LANE), lambda i, j: (i, 0))],
        out_shape=[jax.ShapeDtypeStruct((m, n_cols), BF16),
                   jax.ShapeDtypeStruct((m, LANE), F32)],
        scratch_shapes=[pltpu.VMEM((tm, D_MODEL), BF16)],
        compiler_params=pltpu.CompilerParams(dimension_semantics=("arbitrary", "arbitrary"),
                                             vmem_limit_bytes=VMEM_LIMIT),
        name="inproj",
    )(x2d, shift, scale, gain, w_t, w_dt_t)


def _dtprep_kernel(raw_ref, bias_ref, alog_ref, row_ref, *, n_chunks):
    a = -jnp.exp(alog_ref[...])
    bias = bias_ref[...]
    ii = lax.broadcasted_iota(jnp.int32, (CHUNK, CHUNK), 0)
    jj = lax.broadcasted_iota(jnp.int32, (CHUNK, CHUNK), 1)
    tril = (jj <= ii).astype(BF16)
    triu = (jj >= ii).astype(BF16)
    lane = lax.broadcasted_iota(jnp.int32, (1, LANE), 1)
    is_fwd = (lane % N_DH) < HEADS_PER_GROUP

    def tri_sum(tri, terms):
        s = _dot(tri, terms)
        return s[:, :LANE] + s[:, LANE:2 * LANE] + s[:, 2 * LANE:]

    for c in range(n_chunks):
        sl = slice(c * CHUNK, (c + 1) * CHUNK)
        dt = jax.nn.softplus(raw_ref[0, sl, :] + bias)
        dta = dt * a
        terms = jnp.concatenate(_split_terms(dta, N_CUM_TERMS), axis=1).astype(BF16)
        cf = tri_sum(tril, terms)
        cb = tri_sum(triu, terms)
        cum = jnp.where(is_fwd, cf, cb)
        tot = jnp.where(is_fwd, cf[CHUNK - 1:CHUNK, :], cb[0:1, :])
        wx = dt * jnp.exp(tot - cum)
        ey = jnp.exp(cum)
        ncx = jnp.log(jnp.maximum(dt, DT_FLOOR)) - cum
        diag = jnp.log(dt + pltpu.roll(dt, LANE - HEADS_PER_GROUP, 1))
        for r_off, val in ((R_CUM, cum * LOG2E), (R_NCX, ncx * LOG2E), (R_EY, ey), (R_WX, wx),
                           (R_DIAG, diag * LOG2E)):
            vt = val.T
            for g in range(N_GROUPS):
                r0 = g * ROWS_PER_GROUP + r_off
                row_ref[0, r0:r0 + N_DH, sl] = vt[g * N_DH:(g + 1) * N_DH, :]


def _dtprep(raw, bias, alog, tl):
    b, l, _ = raw.shape
    return pl.pallas_call(
        functools.partial(_dtprep_kernel, n_chunks=tl // CHUNK),
        grid=(b, l // tl),
        in_specs=[pl.BlockSpec((1, tl, LANE), lambda i, j: (i, j, 0)),
                  pl.BlockSpec((1, LANE), lambda i, j: (0, 0)),
                  pl.BlockSpec((1, LANE), lambda i, j: (0, 0))],
        out_specs=pl.BlockSpec((1, N_GROUPS * ROWS_PER_GROUP, tl), lambda i, j: (i, 0, j)),
        out_shape=jax.ShapeDtypeStruct((b, N_GROUPS * ROWS_PER_GROUP, l), F32),
        compiler_params=pltpu.CompilerParams(dimension_semantics=("arbitrary", "arbitrary"),
                                             vmem_limit_bytes=VMEM_LIMIT),
        name="dtprep",
    )(raw, bias, alog)


def _split_terms(x, n):
    terms = []
    for _ in range(n - 1):
        t = x.astype(BF16).astype(F32)
        terms.append(t)
        x = x - t
    terms.append(x)
    return terms


def _scale_rows(rows):
    return jnp.concatenate(_split_terms(rows[R_EY:R_EY + N_DH], 2) + _split_terms(rows[R_WX:R_WX + N_DH], 2),
                           axis=0).astype(BF16)


def _conv_silu(ext_refs, s, p0, cw, cbias):
    sl = slice(s * LANE, (s + 1) * LANE)
    acc = cbias[:, sl].astype(BF16)
    for k in range(D_CONV):
        pair = p0 + (HALO + k - D_CONV // 2 - k % 2) // 2
        tap = pltpu.bitcast(ext_refs[k % 2][s, pl.ds(pair, CHUNK // 2), :], BF16)
        acc = acc + cw[k:k + 1, sl].astype(BF16) * tap
    return acc * jax.nn.sigmoid(acc)


def _ssd_kernel(x_ref, b_ref, c_ref, xp_ref, bp_ref, cp_ref, xn_ref, bn_ref, cn_ref, z_ref, row_ref,
                cx_ref, cb_ref, crow_ref, cwx_ref, cwb_ref, cwc_ref, cbx_ref, cbb_ref, cbc_ref,
                dskip_ref, gain_ref, onesel_ref, sel_ref, out_ref,
                cache_ref, yacc_ref, sf_ref, sb_ref, ext0_ref, ext1_ref, *buf_refs, tl, n_blk, ctx_len):
    p = pl.program_id(2)
    i = pl.program_id(3)
    n_ch = tl // CHUNK
    args_bufs = buf_refs[0:N_BUFS]
    ew_bufs = buf_refs[N_BUFS:2 * N_BUFS]
    st_bufs = buf_refs[2 * N_BUFS:3 * N_BUFS]
    ext_refs = (ext0_ref, ext1_ref)
    cw = jnp.concatenate([cwx_ref[...], cwb_ref[...], cwc_ref[...]], axis=1)
    cbias = jnp.concatenate([cbx_ref[...], cbb_ref[...], cbc_ref[...]], axis=1)

    def slabs_of(x, bm, cm):
        return [x[:, :LANE], x[:, LANE:], bm, cm]

    def slab_refs(xr, br, cr):
        return [(xr, 0), (xr, LANE), (br, 0), (cr, 0)]

    def fill_ext(src, first, last, n_rows):
        half = HALO // 2
        n_piece = n_rows // CHUNK

        def copy_piece(k, carry):
            r = pl.multiple_of(k * CHUNK, CHUNK)
            q = pl.multiple_of(k * (CHUNK // 2), CHUNK // 2)
            for s, (ref, lane0) in enumerate(src):
                ext0_ref[s, pl.ds(half + q, CHUNK // 2), :] = pltpu.bitcast(
                    ref[0, pl.ds(r, CHUNK), lane0:lane0 + LANE], jnp.uint32)
            return carry

        def shift_piece(k, carry):
            q = pl.multiple_of(k * (CHUNK // 2), CHUNK // 2)
            for s in range(len(src)):
                lo = ext0_ref[s, pl.ds(q, CHUNK // 2), :]
                hi = ext0_ref[s, pl.ds(q + 1, CHUNK // 2), :]
                ext1_ref[s, pl.ds(q, CHUNK // 2), :] = (lo >> 16) | (hi << 16)
            return carry

        for s in range(len(src)):
            ext0_ref[s, 0:half, :] = pltpu.bitcast(first[s], jnp.uint32)
            ext0_ref[s, half + n_rows // 2:2 * half + n_rows // 2, :] = pltpu.bitcast(last[s], jnp.uint32)
        lax.fori_loop(0, n_piece, copy_piece, 0, unroll=2)
        lax.fori_loop(0, n_piece, shift_piece, 0, unroll=2)
        for s in range(len(src)):
            q = n_rows // 2
            ext1_ref[s, q:q + half, :] = (ext0_ref[s, q:q + half, :] >> 16) | (ext0_ref[s, q + 1:q + half + 1, :] << 16)

    def row_of(c):
        return c * CHUNK if isinstance(c, int) else pl.multiple_of(c * CHUNK, CHUNK)

    def expand(rowt, d):
        return _dot_tn(_scale_rows(rowt), sel_ref[d])

    def run_pipelined(prepare_parts, consume_parts, weave, order, per_trip, aheads):
        def step(k, d, in_loop):
            handed = None
            for item in weave.split():
                n = int(item[1:])
                if item[0] == "P":
                    if in_loop or k + aheads[n] < n_ch:
                        prepare_parts[n](order(k + aheads[n]), (d + aheads[n]) % N_BUFS)
                else:
                    handed = consume_parts[n](order(k), d % N_BUFS, handed)

        for n, part in enumerate(prepare_parts):
            for a in range(aheads[n]):
                part(order(a), a)

        def trip(t, carry):
            for d in range(per_trip):
                step(per_trip * t + d, d, True)
            return carry

        lax.fori_loop(0, n_ch // per_trip - 1, trip, 0)
        for d in range(per_trip):
            step(n_ch - per_trip + d, d, False)

    @pl.when(jnp.logical_and(p == 0, i == 0))
    def _init_from_context():
        zpad = [jnp.zeros((HALO, LANE), BF16)] * (N_SLAB - 1)
        fill_ext(slab_refs(cx_ref, cb_ref, None)[:N_SLAB - 1], zpad, zpad, ctx_len)
        n_cc = ctx_len // CHUNK
        sf = jnp.zeros((D_STATE, GROUP_W), F32)
        st_b, ey_b = [], []
        for cc in range(n_cc):
            slabs = [_conv_silu(ext_refs, s, cc * CHUNK // 2, cw, cbias) for s in range(N_SLAB - 1)]
            xs = jnp.concatenate(slabs[:2], axis=1).astype(F32)
            bm = slabs[2]
            rowt = crow_ref[0, :, cc * CHUNK:(cc + 1) * CHUNK]
            ewf = expand(rowt, 0)
            ewb = expand(rowt, 1)
            sf = sf * ewf[CHUNK - 1:CHUNK, :GROUP_W] + _dot_tn(bm, (xs * ewf[:, GROUP_W:]).astype(BF16))
            st_b.append(_dot_tn(bm, (xs * ewb[:, GROUP_W:]).astype(BF16)))
            ey_b.append(ewb[0:1, :GROUP_W])
        sb = jnp.zeros((D_STATE, GROUP_W), F32)
        for cc in reversed(range(n_cc)):
            sb = sb * ey_b[cc] + st_b[cc]
        sf_ref[...] = sf
        sb_ref[...] = sb

    @pl.when(p == 0)
    def _forward():
        blk = i
        base = blk * tl
        has_prev = blk > 0
        has_next = blk < n_blk - 1
        fill_ext(slab_refs(x_ref, b_ref, c_ref),
                 [jnp.where(has_prev, v, jnp.zeros_like(v)) for v in slabs_of(xp_ref[0], bp_ref[0], cp_ref[0])],
                 [jnp.where(has_next, v, jnp.zeros_like(v)) for v in slabs_of(xn_ref[0], bn_ref[0], cn_ref[0])],
                 tl)
        ii = lax.broadcasted_iota(jnp.int32, (CHUNK, CHUNK), 0)
        jj = lax.broadcasted_iota(jnp.int32, (CHUNK, CHUNK), 1)
        lower = jj < ii
        upper = jj > ii

        def prepare_conv(c, buf):
            r = row_of(c)
            p0 = c * (CHUNK // 2) if isinstance(c, int) else pl.multiple_of(c * (CHUNK // 2), CHUNK // 2)
            slabs = [_conv_silu(ext_refs, s, p0, cw, cbias) for s in range(N_SLAB)]
            for s in range(N_SLAB):
                cache_ref[pl.ds(base + r, CHUNK), s * LANE:(s + 1) * LANE] = slabs[s]
            yacc_ref[pl.ds(base + r, CHUNK), :] = jnp.concatenate(slabs[:2], axis=1).astype(F32) * dskip_ref[...]

        def prepare_scales(c, buf):
            rowt = row_ref[0, :, pl.ds(row_of(c), CHUNK)]
            onesel = onesel_ref[...]
            ones = jnp.ones((N_DH, CHUNK), F32)
            at = jnp.concatenate(_split_terms(rowt[R_CUM:R_CUM + N_DH], N_CUM_TERMS) + [ones] * N_CUM_TERMS,
                                 axis=0).astype(BF16)
            wide = [jnp.concatenate([t] * N_DH, axis=1) * onesel
                    for t in _split_terms(rowt[R_NCX:R_NCX + N_DH], N_CUM_TERMS)]
            bmat = jnp.concatenate([onesel] * N_CUM_TERMS + wide, axis=0).astype(BF16)
            args_bufs[buf][...] = _dot_tn(at, bmat)
            ew_bufs[buf][...] = expand(rowt, 0)

        def consume_state(c, buf, _):
            r = row_of(c)
            xbc_b = cache_ref[pl.ds(base + r, CHUNK), :]
            xs_b = xbc_b[:, :GROUP_W]
            bm = xbc_b[:, GROUP_W:GROUP_W + D_STATE]
            cm = xbc_b[:, GROUP_W + D_STATE:]
            return r, xs_b, bm, cm, _dot_nt(cm, bm)

        def consume_decay(c, buf, handed):
            r, xs_b, bm, cm, cbm = handed
            yd = []
            for j in range(HEADS_PER_GROUP):
                argf = args_bufs[buf][:, j * CHUNK:(j + 1) * CHUNK]
                argb = args_bufs[buf][:, (HEADS_PER_GROUP + j) * CHUNK:(HEADS_PER_GROUP + j + 1) * CHUNK]
                diag = row_ref[0, R_DIAG + j:R_DIAG + j + 1, pl.ds(r, CHUNK)]
                arg = jnp.where(lower, argf, jnp.where(upper, argb, diag))
                wmat = (cbm * jnp.exp2(arg)).astype(BF16)
                yd.append(_dot(wmat, xs_b[:, j * HEAD_DIM:(j + 1) * HEAD_DIM]))
            ef = ew_bufs[buf][:, :GROUP_W]
            wxf = ew_bufs[buf][:, GROUP_W:]
            sf = sf_ref[...]
            yacc_ref[pl.ds(base + r, CHUNK), :] += jnp.concatenate(yd, axis=1) + _dot(cm, sf.astype(BF16)) * ef
            sf_ref[...] = sf * ef[CHUNK - 1:CHUNK, :] + _dot_tn(bm, xs_b * wxf.astype(BF16))

        run_pipelined([prepare_conv, prepare_scales], [consume_state, consume_decay], FWD_WEAVE,
                      lambda k: k, FWD_PER_TRIP, FWD_AHEAD)

    @pl.when(p == 1)
    def _backward():
        blk = n_blk - 1 - i
        base = blk * tl

        def prepare(c, buf):
            r = row_of(c)
            ew = expand(row_ref[0, :, pl.ds(r, CHUNK)], 1)
            ew_bufs[buf][...] = ew
            xs_b = cache_ref[pl.ds(base + r, CHUNK), 0:GROUP_W]
            bm = cache_ref[pl.ds(base + r, CHUNK), GROUP_W:GROUP_W + D_STATE]
            st_bufs[buf][...] = _dot_tn(bm, xs_b * ew[:, GROUP_W:].astype(BF16))

        def consume(c, buf, _):
            r = row_of(c)
            cm = cache_ref[pl.ds(base + r, CHUNK), GROUP_W + D_STATE:]
            eb = ew_bufs[buf][:, :GROUP_W]
            sb = sb_ref[...]
            y = yacc_ref[pl.ds(base + r, CHUNK), :] + _dot(cm, sb.astype(BF16)) * eb
            sb_ref[...] = sb * eb[0:1, :] + st_bufs[buf][...]
            z = z_ref[0, pl.ds(r, CHUNK), :]
            y = y * (z * jax.nn.sigmoid(z)).astype(F32)
            out_ref[0, pl.ds(r, CHUNK), :] = _rmsnorm(y, gain_ref[...]).astype(BF16)

        run_pipelined([prepare], [consume], BWD_WEAVE, lambda k: n_ch - 1 - k, BWD_PER_TRIP, BWD_AHEAD)


def _ssd_selectors():
    onesel = (np.arange(N_DH * CHUNK)[None, :] // CHUNK == np.arange(N_DH)[:, None]).astype(np.float32)
    sel = np.zeros((2, 4 * N_DH, 2 * GROUP_W), np.float32)
    for d in range(2):
        for j in range(HEADS_PER_GROUP):
            dh = d * HEADS_PER_GROUP + j
            for term in range(2):
                sel[d, term * N_DH + dh, j * HEAD_DIM:(j + 1) * HEAD_DIM] = 1.0
                sel[d, (2 + term) * N_DH + dh, GROUP_W + j * HEAD_DIM:GROUP_W + (j + 1) * HEAD_DIM] = 1.0
    return jnp.asarray(onesel, F32), jnp.asarray(sel, BF16)


def _ssd(u, rows, cu, crows, conv_w, conv_b, dskip, gain, tl):
    b, l, _ = u.shape
    ctx_len = cu.shape[1]
    assert ctx_len <= tl
    n_blk = l // tl
    hb = tl // HALO
    last_halo = l // HALO - 1
    assert (tl // CHUNK) % FWD_PER_TRIP == 0 and (tl // CHUNK) % BWD_PER_TRIP == 0
    onesel, sel = _ssd_selectors()
    xg, bg, cg = 0, U_B // D_STATE, U_C // D_STATE

    def in_blk(p, i):
        return i * (1 - p) + (n_blk - 1) * p

    def seq_blk(p, i):
        return i * (1 - p) + (n_blk - 1 - i) * p

    def out_blk(p, i):
        return (n_blk - 1) * (1 - p) + (n_blk - 1 - i) * p

    def prev_halo(p, i):
        return jnp.maximum(in_blk(p, i) * hb - 1, 0)

    def next_halo(p, i):
        return jnp.minimum((in_blk(p, i) + 1) * hb, last_halo)

    def xbc_specs(rows_blk, row_index):
        return [pl.BlockSpec((1, rows_blk, GROUP_W), lambda bb, g, p, i: (bb, row_index(p, i), xg + g)),
                pl.BlockSpec((1, rows_blk, D_STATE), lambda bb, g, p, i: (bb, row_index(p, i), bg + g)),
                pl.BlockSpec((1, rows_blk, D_STATE), lambda bb, g, p, i: (bb, row_index(p, i), cg + g))]

    def param_specs(n_rows):
        return [pl.BlockSpec((n_rows, GROUP_W), lambda bb, g, p, i: (0, xg + g)),
                pl.BlockSpec((n_rows, D_STATE), lambda bb, g, p, i: (0, bg + g)),
                pl.BlockSpec((n_rows, D_STATE), lambda bb, g, p, i: (0, cg + g))]

    grid = (b, N_GROUPS, 2, n_blk)
    in_specs = (
        xbc_specs(tl, in_blk) + xbc_specs(HALO, prev_halo) + xbc_specs(HALO, next_halo)
        + [pl.BlockSpec((1, tl, GROUP_W), lambda bb, g, p, i: (bb, out_blk(p, i), U_Z // GROUP_W + g)),
           pl.BlockSpec((1, ROWS_PER_GROUP, tl), lambda bb, g, p, i: (bb, g, seq_blk(p, i)))]
        + xbc_specs(ctx_len, lambda p, i: 0)[:2]
        + [pl.BlockSpec((1, ROWS_PER_GROUP, ctx_len), lambda bb, g, p, i: (bb, g, 0))]
        + param_specs(SUBLANE) + param_specs(1)
        + [pl.BlockSpec((1, GROUP_W), lambda bb, g, p, i: (0, g)),
           pl.BlockSpec((1, GROUP_W), lambda bb, g, p, i: (0, g)),
           pl.BlockSpec(onesel.shape, lambda bb, g, p, i: (0, 0)),
           pl.BlockSpec(sel.shape, lambda bb, g, p, i: (0, 0, 0))])
    return pl.pallas_call(
        functools.partial(_ssd_kernel, tl=tl, n_blk=n_blk, ctx_len=ctx_len),
        grid=grid,
        in_specs=in_specs,
        out_specs=pl.BlockSpec((1, tl, GROUP_W), lambda bb, g, p, i: (bb, out_blk(p, i), g)),
        out_shape=jax.ShapeDtypeStruct((b, l, D_INNER), BF16),
        scratch_shapes=[pltpu.VMEM((l, N_SLAB * LANE), BF16),
                        pltpu.VMEM((l, GROUP_W), F32),
                        pltpu.VMEM((D_STATE, GROUP_W), F32),
                        pltpu.VMEM((D_STATE, GROUP_W), F32),
                        pltpu.VMEM((N_SLAB, (tl + 2 * HALO) // 2, LANE), jnp.uint32),
                        pltpu.VMEM((N_SLAB, (tl + 2 * HALO) // 2, LANE), jnp.uint32)]
        + [pltpu.VMEM((CHUNK, N_DH * CHUNK), F32)] * N_BUFS
        + [pltpu.VMEM((CHUNK, 2 * GROUP_W), F32)] * N_BUFS
        + [pltpu.VMEM((D_STATE, GROUP_W), F32)] * N_BUFS,
        compiler_params=pltpu.CompilerParams(
            dimension_semantics=("arbitrary", "arbitrary", "arbitrary", "arbitrary"),
            vmem_limit_bytes=VMEM_LIMIT),
        name="ssd",
    )(u, u, u, u, u, u, u, u, u, u, rows, cu, cu, crows,
      conv_w, conv_w, conv_w, conv_b, conv_b, conv_b, dskip, gain, onesel, sel)


def _dft_mats():
    def cs(n):
        k = np.arange(n)
        ang = 2.0 * np.pi * np.outer(k, k) / n
        return np.cos(ang), np.sin(ang)

    c3, s3 = cs(FFT_GROUP_DIM)
    w1 = np.concatenate([c3, -s3], axis=1)
    c2, s2 = cs(GRID_W)
    m2 = np.block([[c2, s2], [-s2, c2]])
    return w1, m2


def _fft12_kernel(f_ref, w1_ref, m2_ref, o_ref, *, tm):
    f = f_ref[0]
    w1 = w1_ref[...]
    ps, qs = [], []
    for g in range(FFT_GROUPS):
        pq = _dot(f[:, g * FFT_GROUP_DIM:(g + 1) * FFT_GROUP_DIM], w1)
        ps.append(pq[:, :FFT_GROUP_DIM])
        qs.append(pq[:, FFT_GROUP_DIM:])
    pr = jnp.concatenate(ps, axis=1).astype(BF16)
    qr = jnp.concatenate(qs, axis=1).astype(BF16)
    m2 = m2_ref[...]
    pad = jnp.zeros((FFT_PITCH - GRID_W, D_MODEL), jnp.uint32)
    for r in range(tm // GRID_W):
        sl = slice(r * GRID_W, (r + 1) * GRID_W)
        o = _dot(m2, jnp.concatenate([pr[sl], qr[sl]], axis=0))
        r0 = r * FFT_PITCH
        o_ref[0, r0:r0 + GRID_W, :] = _pack_complex(o[:GRID_W], o[GRID_W:])
        o_ref[0, r0 + GRID_W:r0 + FFT_PITCH, :] = pad


def _pack_complex(re, im):
    re_bits = pltpu.bitcast(re.astype(BF16).astype(F32), jnp.uint32)
    im_bits = pltpu.bitcast(im.astype(BF16).astype(F32), jnp.uint32)
    return (re_bits >> 16) | (im_bits & jnp.uint32(0xFFFF0000))


def _unpack_complex(word):
    re = pltpu.bitcast(word << 16, F32)
    im = pltpu.bitcast(word & jnp.uint32(0xFFFF0000), F32)
    return re, im


def _fft3_kernel(ri_ref, m3_ref, o_ref, scr_ref, *, rows):
    m3 = m3_ref[...]

    def column_pair(t, carry):
        w = 2 * t
        cols = [jnp.concatenate(_unpack_complex(ri_ref[0, pl.ds(w + dw, rows, stride=FFT_PITCH), :]), axis=0)
                for dw in range(2)]
        o = _dot(m3, jnp.concatenate(cols, axis=1).astype(BF16))
        for dw in range(2):
            scr_ref[pl.ds(w + dw, rows, stride=FFT_PITCH), :] = o[:, dw * LANE:(dw + 1) * LANE]
        return carry

    lax.fori_loop(0, GRID_W // 2, column_pair, 0, unroll=True)
    for k in range(rows):
        o_ref[0, k * GRID_W:(k + 1) * GRID_W, :] = scr_ref[k * FFT_PITCH:k * FFT_PITCH + GRID_W, :].astype(BF16)


def _fourier(u, tm):
    b, l, _ = u.shape
    rows = l // GRID_W
    w1, m2 = _dft_mats()
    ri = pl.pallas_call(
        functools.partial(_fft12_kernel, tm=tm),
        grid=(b, l // tm),
        in_specs=[pl.BlockSpec((1, tm, D_MODEL), lambda i, j: (i, j, U_FFT // D_MODEL)),
                  pl.BlockSpec((FFT_GROUP_DIM, 2 * FFT_GROUP_DIM), lambda i, j: (0, 0)),
                  pl.BlockSpec((2 * GRID_W, 2 * GRID_W), lambda i, j: (0, 0))],
        out_specs=pl.BlockSpec((1, tm // GRID_W * FFT_PITCH, D_MODEL), lambda i, j: (i, j, 0)),
        out_shape=jax.ShapeDtypeStruct((b, rows * FFT_PITCH, D_MODEL), jnp.uint32),
        compiler_params=pltpu.CompilerParams(dimension_semantics=("arbitrary", "arbitrary"),
                                             vmem_limit_bytes=VMEM_LIMIT),
        name="fft12",
    )(u, jnp.asarray(w1, BF16), jnp.asarray(m2, BF16))

    k = np.arange(rows)
    ang = 2.0 * np.pi * np.outer(k, k) / rows
    scale = 1.0 / np.sqrt(float(rows * GRID_W * FFT_GROUP_DIM))
    m3 = np.concatenate([np.cos(ang), np.sin(ang)], axis=1) * scale
    return pl.pallas_call(
        functools.partial(_fft3_kernel, rows=rows),
        grid=(b, D_MODEL // LANE),
        in_specs=[pl.BlockSpec((1, rows * FFT_PITCH, LANE), lambda i, j: (i, 0, j)),
                  pl.BlockSpec((rows, 2 * rows), lambda i, j: (0, 0))],
        out_specs=pl.BlockSpec((1, l, LANE), lambda i, j: (i, 0, j)),
        out_shape=jax.ShapeDtypeStruct((b, l, D_MODEL), BF16),
        scratch_shapes=[pltpu.VMEM((rows * FFT_PITCH, LANE), F32)],
        compiler_params=pltpu.CompilerParams(dimension_semantics=("arbitrary", "arbitrary"),
                                             vmem_limit_bytes=VMEM_LIMIT),
        name="fft3",
    )(ri, jnp.asarray(m3, BF16))


MXU_WIDTH = 256
FF_SLICES = ((0, 6 * MXU_WIDTH), (6 * MXU_WIDTH, D_FF))


def _tail_kernel(y_ref, f_ref, ga_ref, gb_ref, x_ref, g1_ref, sh2_ref, sc2_ref, g2_ref,
                 wso_ref, wfo_ref, wo_ref, n2_ref, wff_ref, wfd_ref, fg_ref, o_ref):
    a = _dot(y_ref[...], wso_ref[...])
    bq = _dot(f_ref[...], wfo_ref[...])
    merged = jax.nn.sigmoid(ga_ref[...].astype(F32)) * a + jax.nn.sigmoid(gb_ref[...].astype(F32)) * bq
    xn = x_ref[...] + g1_ref[0] * _dot(merged.astype(BF16), wo_ref[...])
    h = (_rmsnorm(xn, n2_ref[...]) * (1.0 + sc2_ref[0]) + sh2_ref[0]).astype(BF16)
    acc = None
    for lo, hi in FF_SLICES:
        gate = _dot(h, wff_ref[:, lo:hi])
        up = _dot(h, wff_ref[:, D_FF + lo:D_FF + hi])
        act = (gate * jax.nn.sigmoid(gate) * up).astype(BF16)
        part = _dot(act, wfd_ref[lo:hi, :])
        acc = part if acc is None else acc + part
    o_ref[...] = _rmsnorm(xn + g2_ref[0] * acc, fg_ref[...])


def _tail(y2d, f2d, u2d, x2d, g1, sh2, sc2, g2, wso, wfo, wo, n2, wff, wfd, fg, tm, rows_per_mod):
    m = x2d.shape[0]
    bpm = rows_per_mod // tm

    def const(shape):
        return pl.BlockSpec(shape, lambda i: (0, 0), pipeline_mode=pl.Buffered(1))

    def mod():
        return pl.BlockSpec((1, 1, D_MODEL), lambda i: (i // bpm, 0, 0))

    return pl.pallas_call(
        _tail_kernel,
        grid=(m // tm,),
        in_specs=[pl.BlockSpec((tm, D_INNER), lambda i: (i, 0)),
                  pl.BlockSpec((tm, D_MODEL), lambda i: (i, 0)),
                  pl.BlockSpec((tm, D_MODEL), lambda i: (i, U_GATE // D_MODEL)),
                  pl.BlockSpec((tm, D_MODEL), lambda i: (i, U_GATE // D_MODEL + 1)),
                  pl.BlockSpec((tm, D_MODEL), lambda i: (i, 0)),
                  mod(), mod(), mod(), mod(),
                  const((D_INNER, D_MODEL)), const((D_MODEL, D_MODEL)), const((D_MODEL, D_MODEL)),
                  const((1, D_MODEL)),
                  const((D_MODEL, 2 * D_FF)), const((D_FF, D_MODEL)),
                  const((1, D_MODEL))],
        out_specs=pl.BlockSpec((tm, D_MODEL), lambda i: (i, 0)),
        out_shape=jax.ShapeDtypeStruct((m, D_MODEL), F32),
        compiler_params=pltpu.CompilerParams(dimension_semantics=("arbitrary",),
                                             vmem_limit_bytes=VMEM_LIMIT),
        name="tail",
    )(y2d, f2d, u2d, u2d, x2d, g1, sh2, sc2, g2, wso, wfo, wo, n2, wff, wfd, fg)


def _dt_lanes(a):
    lead = a.shape[:-1]
    a = a.reshape(lead + (2, N_GROUPS, HEADS_PER_GROUP))
    a = jnp.swapaxes(a, -3, -2).reshape(lead + (2 * N_HEADS,))
    return jnp.concatenate([a, jnp.zeros(lead + (LANE - 2 * N_HEADS,), a.dtype)], axis=-1)


def _block(x, c, ctx, c_ctx, w_ada, b_ada, norm1_g, w_in, conv_w, conv_b, dt_bias, a_log, d_skip,
           ssd_norm_g, w_ssd_out, w_fft_out, w_o, norm2_g, w_ffn_in, w_ffn_out, final_g,
           proj_tm, prep_tl, ssd_tl, fft_tm, tail_tm):
    b, l, d = x.shape
    ctx_len = ctx.shape[1]
    assert d == D_MODEL and l % (GRID_W * 2) == 0 and ctx_len % CHUNK == 0 and b + 1 <= SUBLANE

    w_in_t = w_in[0].T
    w_main_t = w_in_t.astype(BF16)
    w_dt = _dt_lanes(w_in_t[REF_DT:REF_Z].T).T
    w_dt_hi = w_dt.astype(BF16)
    w_dt_lo = (w_dt - w_dt_hi.astype(F32)).astype(BF16)
    w_dt2 = jnp.concatenate([w_dt_hi, w_dt_lo], axis=0)
    conv_w_p = jnp.concatenate([conv_w[0], jnp.zeros((SUBLANE - D_CONV, CONV_DIM), F32)], axis=0)
    conv_b_p = conv_b[0].reshape(1, CONV_DIM)
    bias_v = _dt_lanes(dt_bias[0].reshape(-1)).reshape(1, LANE)
    alog_v = _dt_lanes(a_log[0].reshape(-1)).reshape(1, LANE)
    dskip_v = jnp.repeat(d_skip[0], HEAD_DIM).reshape(1, D_INNER)
    gain_v = ssd_norm_g[0].reshape(1, D_INNER)
    wso = w_ssd_out[0].astype(BF16)
    wfo = w_fft_out[0].astype(BF16)
    wo = w_o[0].astype(BF16)
    wff = w_ffn_in[0].astype(BF16)
    wfd = w_ffn_out[0].astype(BF16)

    cvecs = jnp.concatenate([c, c_ctx[None, :], jnp.zeros((SUBLANE - b - 1, D_MODEL), F32)], axis=0)
    mods = _mods(cvecs, b + 1, w_ada[0], b_ada[0])
    lat = mods[:b].reshape(b, 6, 1, D_MODEL)
    sh1, sc1, g1, sh2, sc2, g2 = (lat[:, k] for k in range(6))
    cm = mods[b].reshape(6, 1, 1, D_MODEL)

    x2d = x.reshape(b * l, D_MODEL)
    n1 = norm1_g[0].reshape(1, D_MODEL)
    u2d, dt_raw = _inproj(x2d, sh1, sc1, n1, w_main_t, w_dt2, U_WIDTH, proj_tm, l)
    u = u2d.reshape(b, l, U_WIDTH)
    cu2d, cdt_raw = _inproj(ctx.reshape(b * ctx_len, D_MODEL), cm[0], cm[1], n1, w_main_t, w_dt2,
                            CONV_DIM, b * ctx_len, b * ctx_len)
    cu = cu2d.reshape(b, ctx_len, CONV_DIM)

    rows = _dtprep(dt_raw.reshape(b, l, LANE), bias_v, alog_v, prep_tl)
    crows = _dtprep(cdt_raw.reshape(b, ctx_len, LANE), bias_v, alog_v, ctx_len)
    y_ssd = _ssd(u, rows, cu, crows, conv_w_p, conv_b_p, dskip_v, gain_v, ssd_tl)

    f_mix = _fourier(u, fft_tm)

    out = _tail(y_ssd.reshape(b * l, D_INNER), f_mix.reshape(b * l, D_MODEL), u2d, x2d,
                g1, sh2, sc2, g2, wso, wfo, wo, norm2_g[0].reshape(1, D_MODEL), wff, wfd,
                final_g.reshape(1, D_MODEL), tail_tm, l)
    return out.reshape(b, l, D_MODEL)


def kernel(x, c, ctx, c_ctx, w_ada, b_ada, norm1_g, w_in, conv_w, conv_b, dt_bias, a_log, d_skip,
           ssd_norm_g, w_ssd_out, w_fft_out, w_o, norm2_g, w_ffn_in, w_ffn_out, final_g):
    return _block(x, c, ctx, c_ctx, w_ada, b_ada, norm1_g, w_in, conv_w, conv_b, dt_bias, a_log, d_skip,
                  ssd_norm_g, w_ssd_out, w_fft_out, w_o, norm2_g, w_ffn_in, w_ffn_out, final_g,
                  proj_tm=2048, prep_tl=1024, ssd_tl=4096, fft_tm=1024, tail_tm=512)
```
